```python
import math
import jax
import jax.numpy as jnp
from jax import lax
import numpy as np

D_MODEL = 1024
BATCH = 16
SEQ = 2048
DEPTH = 2

GRID_W = 64
CTX_LEN = 256
HEAD_DIM = 64
Q_BLOCK = 128
D_FF = 2816
N_MOD = 9
ROPE_BASE = 10000.0
EPS = 1e-6
NEG = -1e30
A_HEADS = 8
A_KV_HEADS = 2
A_WINDOW = 128
B_HEADS = 4
B_V_DIM = 2 * HEAD_DIM
C_HEADS = 8
C_Q_RANK = 768
C_KV_RANK = 256
C_NOPE = 64
C_ROPE = 32
C_V = 64
D_HEADS = 8
D_WIN_ROWS = 8
D_WIN_COLS = 16
N_EVEN = (DEPTH + 1) // 2
N_ODD = DEPTH // 2
AB_SPLITS = (A_HEADS * HEAD_DIM, B_HEADS * 2 * HEAD_DIM, A_KV_HEADS * HEAD_DIM, A_KV_HEADS * HEAD_DIM, B_HEADS * 2 * HEAD_DIM, B_HEADS * B_V_DIM)
AB_Q_WIDTH = AB_SPLITS[0] + AB_SPLITS[1]
AB_IN = sum(AB_SPLITS)
AB_OUT = A_HEADS * HEAD_DIM + B_HEADS * B_V_DIM
CD_SPLITS = (C_Q_RANK, D_HEADS * HEAD_DIM, C_KV_RANK + C_ROPE, D_HEADS * HEAD_DIM, D_HEADS * HEAD_DIM)
CD_Q_WIDTH = CD_SPLITS[0] + CD_SPLITS[1]
CD_IN = sum(CD_SPLITS)
CD_OUT = C_HEADS * C_V + D_HEADS * HEAD_DIM

kernel_name = "hybrid_dit_window_diff_mla_natten_macaron"


def split_cols(t, sizes):
    cuts = [int(v) for v in np.cumsum(sizes)[:-1]]
    return jnp.split(t, cuts, axis=-1)


def rms_norm(x, gain=None):
    xf = x.astype(jnp.float32)
    y = xf * lax.rsqrt(jnp.mean(xf * xf, axis=-1, keepdims=True) + EPS)
    if gain is not None:
        y = y * gain.astype(jnp.float32)
    return y.astype(x.dtype)


def modulate(h, shift, scale):
    return h * (1.0 + scale) + shift


def swiglu(h, w_gate, w_up, w_down):
    return (jax.nn.silu(h @ w_gate) * (h @ w_up)) @ w_down


def axial_rope_tables(n, rot_dim):
    nf = rot_dim // 4
    inv = ROPE_BASE ** (-jnp.arange(nf, dtype=jnp.float32) / nf)
    t = jnp.arange(n)
    row = (t // GRID_W).astype(jnp.float32)
    col = (t % GRID_W).astype(jnp.float32)
    ang = jnp.concatenate([row[:, None] * inv, col[:, None] * inv], axis=-1)
    return jnp.cos(ang), jnp.sin(ang)


def apply_rope2d(x, cos, sin):
    nf = x.shape[-1] // 4
    xf = x.astype(jnp.float32).reshape(x.shape[:-1] + (2, 2, nf))
    x1, x2 = xf[..., 0, :], xf[..., 1, :]
    c = cos.reshape(cos.shape[0], 1, 2, nf)
    s = sin.reshape(sin.shape[0], 1, 2, nf)
    out = jnp.stack([x1 * c - x2 * s, x2 * c + x1 * s], axis=-2)
    return out.reshape(x.shape).astype(x.dtype)


def joint_softmax(scores, sink=None):
    m = scores[0].max(axis=-1, keepdims=True)
    for s in scores[1:]:
        m = jnp.maximum(m, s.max(axis=-1, keepdims=True))
    if sink is not None:
        m = jnp.maximum(m, sink)
    es = [jnp.exp(s - m) for s in scores]
    denom = es[0].sum(axis=-1, keepdims=True)
    for e in es[1:]:
        denom = denom + e.sum(axis=-1, keepdims=True)
    if sink is not None:
        denom = denom + jnp.exp(sink - m)
    return [e / denom for e in es]


def sweep_blocks(fn, q, block):
    B, S = q.shape[0], q.shape[1]
    nb = S // block
    qb = jnp.moveaxis(q.reshape((B, nb, block) + q.shape[2:]), 1, 0)
    out = lax.map(lambda a: fn(a[0], a[1]), (qb, jnp.arange(nb)))
    return jnp.moveaxis(out, 0, 1).reshape((B, S) + out.shape[3:])


def gqa_attend(q, keyvals, sink=None):
    B, Q, H, d = q.shape
    kvh = keyvals[0][0].shape[2]
    g = H // kvh
    qg = q.reshape(B, Q, kvh, g, d)
    scores = []
    for k, _, mask in keyvals:
        s = jnp.einsum('bqhgd,bkhd->bhgqk', qg, k).astype(jnp.float32) * (d ** -0.5)
        if mask is not None:
            s = jnp.where(mask, s, NEG)
        scores.append(s)
    sink_b = None if sink is None else sink.astype(jnp.float32).reshape(kvh, g, 1, 1)
    probs = joint_softmax(scores, sink_b)
    out = None
    for p, (_, v, _) in zip(probs, keyvals):
        o = jnp.einsum('bhgqk,bkhd->bqhgd', p.astype(v.dtype), v)
        out = o if out is None else out + o
    return out.reshape(B, Q, H, d)


def windowed_sink_attention(q, k, v, k_ctx, v_ctx, sink):
    S = q.shape[1]
    span = Q_BLOCK + 2 * A_WINDOW
    pad = ((0, 0), (A_WINDOW, A_WINDOW), (0, 0), (0, 0))
    k_pad = jnp.pad(k, pad)
    v_pad = jnp.pad(v, pad)

    def block(q_blk, n):
        start = n * Q_BLOCK
        k_blk = lax.dynamic_slice_in_dim(k_pad, start, span, axis=1)
        v_blk = lax.dynamic_slice_in_dim(v_pad, start, span, axis=1)
        q_pos = start + jnp.arange(Q_BLOCK)
        k_pos = start - A_WINDOW + jnp.arange(span)
        mask = ((jnp.abs(q_pos[:, None] - k_pos[None, :]) <= A_WINDOW)
                & (k_pos >= 0)[None, :] & (k_pos < S)[None, :])
        return gqa_attend(q_blk, [(k_blk, v_blk, mask), (k_ctx, v_ctx, None)], sink)

    return sweep_blocks(block, q, Q_BLOCK)


def diff_lambda(lq1, lk1, lq2, lk2, lam_init):
    f = jnp.float32
    return (jnp.exp(jnp.sum(lq1.astype(f) * lk1.astype(f)))
            - jnp.exp(jnp.sum(lq2.astype(f) * lk2.astype(f))) + lam_init)


def diff_qk(t, gain, rope):
    B, n, _ = t.shape
    t = rms_norm(t.reshape(B, n, 2 * B_HEADS, HEAD_DIM), gain)
    if rope is not None:
        t = apply_rope2d(t, *rope)
    return t.reshape(B, n, B_HEADS, 2, HEAD_DIM)


def diff_attend(q, k, v, lam):
    s = jnp.einsum('bqhmd,bkhmd->bhmqk', q, k).astype(jnp.float32) * (HEAD_DIM ** -0.5)
    p = jax.nn.softmax(s, axis=-1)
    w = p[:, :, 0] - lam * p[:, :, 1]
    return jnp.einsum('bhqk,bkhe->bqhe', w.astype(v.dtype), v)


def mla_queries(dq, q_a_norm, w_uq, qn_nope, qn_rope, rope):
    B, n, _ = dq.shape
    q = (rms_norm(dq, q_a_norm) @ w_uq).reshape(B, n, C_HEADS, C_NOPE + C_ROPE)
    q_nope = rms_norm(q[..., :C_NOPE], qn_nope)
    q_rope = rms_norm(q[..., C_NOPE:], qn_rope)
    if rope is not None:
        q_rope = apply_rope2d(q_rope, *rope)
    return jnp.concatenate([q_nope, q_rope], axis=-1)


def mla_keys_values(dkv, kv_a_norm, w_ukv, kn_nope, kn_rope, rope):
    B, n, _ = dkv.shape
    c_kv, k_rope = dkv[..., :C_KV_RANK], dkv[..., C_KV_RANK:]
    kv = (rms_norm(c_kv, kv_a_norm) @ w_ukv).reshape(B, n, C_HEADS, C_NOPE + C_V)
    k_nope = rms_norm(kv[..., :C_NOPE], kn_nope)
    k_rope = rms_norm(k_rope, kn_rope)[:, :, None, :]
    if rope is not None:
        k_rope = apply_rope2d(k_rope, *rope)
    return k_nope, k_rope[:, :, 0, :], kv[..., C_NOPE:]


def mla_attend(q, k_nope, k_rope, v):
    q_nope, q_rope = q[..., :C_NOPE], q[..., C_NOPE:]
    s = (jnp.einsum('bqhd,bkhd->bhqk', q_nope, k_nope)
         + jnp.einsum('bqhr,bkr->bhqk', q_rope, k_rope)).astype(jnp.float32) * ((C_NOPE + C_ROPE) ** -0.5)
    p = jax.nn.softmax(s, axis=-1)
    return jnp.einsum('bhqk,bkhd->bqhd', p.astype(v.dtype), v)


def neighbourhood_attention(q, k, v, k_ctx, v_ctx, rpb):
    B, S, H, d = q.shape
    rows = S // GRID_W
    kh = min(D_WIN_ROWS, rows)
    kw = D_WIN_COLS
    k_grid = k.reshape(B, rows, GRID_W, H, d)
    v_grid = v.reshape(B, rows, GRID_W, H, d)
    cols = jnp.arange(GRID_W)
    col_start = jnp.clip(cols - kw // 2, 0, GRID_W - kw)
    col_mask = (cols[None, :] >= col_start[:, None]) & (cols[None, :] < col_start[:, None] + kw)
    dc_idx = jnp.clip(cols[None, :] - cols[:, None], -(kw - 1), kw - 1) + (D_WIN_COLS - 1)
    scale = d ** -0.5
    rpb_f = rpb.astype(jnp.float32)

    def row_block(q_row, r):
        row_start = jnp.clip(r - kh // 2, 0, rows - kh)
        k_band = lax.dynamic_slice_in_dim(k_grid, row_start, kh, axis=1)
        v_band = lax.dynamic_slice_in_dim(v_grid, row_start, kh, axis=1)
        dr_idx = row_start + jnp.arange(kh) - r + (D_WIN_ROWS - 1)
        bias = rpb_f[:, dr_idx[None, :, None], dc_idx[:, None, :]]
        s = jnp.einsum('bqhd,bjkhd->bhqjk', q_row, k_band).astype(jnp.float32) * scale + bias
        s = jnp.where(col_mask[:, None, :], s, NEG).reshape(B, H, GRID_W, kh * GRID_W)
        s_ctx = jnp.einsum('bqhd,bchd->bhqc', q_row, k_ctx).astype(jnp.float32) * scale
        p, p_ctx = joint_softmax([s, s_ctx])
        p = p.reshape(B, H, GRID_W, kh, GRID_W).astype(v.dtype)
        return (jnp.einsum('bhqjk,bjkhd->bqhd', p, v_band)
                + jnp.einsum('bhqc,bchd->bqhd', p_ctx.astype(v.dtype), v_ctx))

    return sweep_blocks(row_block, q, GRID_W)


def mix_window_diff(h, hc, w_in, w_out, a_q_norm, a_k_norm, a_sink, b_q_norm, b_k_norm,
                    b_lq1, b_lk1, b_lq2, b_lk2, b_sub_norm, lam_init, rope, need_ctx):
    B, S, _ = h.shape
    L = hc.shape[1]
    aq, bq, ak, av, bk, bv = split_cols(h @ w_in, AB_SPLITS)
    ak_c, av_c, bk_c, bv_c = split_cols(hc @ w_in[:, AB_Q_WIDTH:], AB_SPLITS[2:])
    aq = apply_rope2d(rms_norm(aq.reshape(B, S, A_HEADS, HEAD_DIM), a_q_norm), *rope)
    ak = apply_rope2d(rms_norm(ak.reshape(B, S, A_KV_HEADS, HEAD_DIM), a_k_norm), *rope)
    av = av.reshape(B, S, A_KV_HEADS, HEAD_DIM)
    ak_c = rms_norm(ak_c.reshape(B, L, A_KV_HEADS, HEAD_DIM), a_k_norm)
    av_c = av_c.reshape(B, L, A_KV_HEADS, HEAD_DIM)
    y_a = windowed_sink_attention(aq, ak, av, ak_c, av_c, a_sink)
    lam = diff_lambda(b_lq1, b_lk1, b_lq2, b_lk2, lam_init)
    bq = diff_qk(bq, b_q_norm, rope)
    bk = diff_qk(bk, b_k_norm, rope)
    bv = bv.reshape(B, S, B_HEADS, B_V_DIM)
    bk_c = diff_qk(bk_c, b_k_norm, None)
    bv_c = bv_c.reshape(B, L, B_HEADS, B_V_DIM)
    k_all = jnp.concatenate([bk_c, bk], axis=1)
    v_all = jnp.concatenate([bv_c, bv], axis=1)
    y_b = sweep_blocks(lambda q_blk, n: diff_attend(q_blk, k_all, v_all, lam), bq, Q_BLOCK)
    y_b = rms_norm(y_b, b_sub_norm) * (1.0 - lam_init)
    y = jnp.concatenate([y_a.reshape(B, S, -1), y_b.reshape(B, S, -1)], axis=-1) @ w_out
    if not need_ctx:
        return y, None
    aq_c, bq_c = split_cols(hc @ w_in[:, :AB_Q_WIDTH], AB_SPLITS[:2])
    aq_c = rms_norm(aq_c.reshape(B, L, A_HEADS, HEAD_DIM), a_q_norm)
    y_a_c = gqa_attend(aq_c, [(ak_c, av_c, None)], a_sink)
    y_b_c = rms_norm(diff_attend(diff_qk(bq_c, b_q_norm, None), bk_c, bv_c, lam), b_sub_norm) * (1.0 - lam_init)
    y_c = jnp.concatenate([y_a_c.reshape(B, L, -1), y_b_c.reshape(B, L, -1)], axis=-1) @ w_out
    return y, y_c


def mix_mla_neighbourhood(h, hc, w_in, w_out, c_q_a_norm, c_kv_a_norm, c_w_uq, c_w_ukv,
                          c_q_nope_norm, c_q_rope_norm, c_k_nope_norm, c_k_rope_norm,
                          d_q_norm, d_k_norm, d_rpb, rope, need_ctx):
    B, S, _ = h.shape
    L = hc.shape[1]
    cq, dq, ckv, dk, dv = split_cols(h @ w_in, CD_SPLITS)
    ckv_c, dk_c, dv_c = split_cols(hc @ w_in[:, CD_Q_WIDTH:], CD_SPLITS[2:])
    q = mla_queries(cq, c_q_a_norm, c_w_uq, c_q_nope_norm, c_q_rope_norm, rope)
    kn, kr, v = mla_keys_values(ckv, c_kv_a_norm, c_w_ukv, c_k_nope_norm, c_k_rope_norm, rope)
    kn_c, kr_c, v_c = mla_keys_values(ckv_c, c_kv_a_norm, c_w_ukv, c_k_nope_norm, c_k_rope_norm, None)
    kn_all = jnp.concatenate([kn_c, kn], axis=1)
    kr_all = jnp.concatenate([kr_c, kr], axis=1)
    v_all = jnp.concatenate([v_c, v], axis=1)
    y_mla = sweep_blocks(lambda q_blk, n: mla_attend(q_blk, kn_all, kr_all, v_all), q, Q_BLOCK)
    dq = rms_norm(dq.reshape(B, S, D_HEADS, HEAD_DIM), d_q_norm)
    dk = rms_norm(dk.reshape(B, S, D_HEADS, HEAD_DIM), d_k_norm)
    dv = dv.reshape(B, S, D_HEADS, HEAD_DIM)
    dk_c = rms_norm(dk_c.reshape(B, L, D_HEADS, HEAD_DIM), d_k_norm)
    dv_c = dv_c.reshape(B, L, D_HEADS, HEAD_DIM)
    y_nat = neighbourhood_attention(dq, dk, dv, dk_c, dv_c, d_rpb)
    y = jnp.concatenate([y_mla.reshape(B, S, -1), y_nat.reshape(B, S, -1)], axis=-1) @ w_out
    if not need_ctx:
        return y, None
    cq_c, dq_c = split_cols(hc @ w_in[:, :CD_Q_WIDTH], CD_SPLITS[:2])
    y_mla_c = mla_attend(mla_queries(cq_c, c_q_a_norm, c_w_uq, c_q_nope_norm, c_q_rope_norm, None), kn_c, kr_c, v_c)
    y_nat_c = gqa_attend(rms_norm(dq_c.reshape(B, L, D_HEADS, HEAD_DIM), d_q_norm), [(dk_c, dv_c, None)])
    y_c = jnp.concatenate([y_mla_c.reshape(B, L, -1), y_nat_c.reshape(B, L, -1)], axis=-1) @ w_out
    return y, y_c


def setup_inputs(seed: int = 0) -> dict:
    key = jax.random.key(seed)
    ks = jax.random.split(key, 37)
    f32 = jnp.float32

    def nrm(i, shape, scale):
        return scale * jax.random.normal(ks[i], shape, f32)

    def gain(i, shape):
        return 1.0 + 0.05 * jax.random.normal(ks[i], shape, f32)

    D, F, NE, NO = D_MODEL, D_FF, N_EVEN, N_ODD
    return {
        "x": nrm(0, (BATCH, SEQ, D), 1.0),
        "c": nrm(1, (BATCH, D), 1.0),
        "ctx": nrm(2, (BATCH, CTX_LEN, D), 1.0),
        "c_ctx": nrm(3, (D,), 1.0),
        "w_mod": nrm(4, (DEPTH, D, N_MOD * D), 0.5 * D ** -0.5),
        "b_mod": nrm(5, (DEPTH, N_MOD * D), 0.02),
        "ffn1_w_gate": nrm(6, (DEPTH, D, F), D ** -0.5),
        "ffn1_w_up": nrm(7, (DEPTH, D, F), D ** -0.5),
        "ffn1_w_down": nrm(8, (DEPTH, F, D), F ** -0.5),
        "ffn2_w_gate": nrm(9, (DEPTH, D, F), D ** -0.5),
        "ffn2_w_up": nrm(10, (DEPTH, D, F), D ** -0.5),
        "ffn2_w_down": nrm(11, (DEPTH, F, D), F ** -0.5),
        "ab_w_in": nrm(12, (NE, D, AB_IN), D ** -0.5),
        "ab_w_out": nrm(13, (NE, AB_OUT, D), AB_OUT ** -0.5),
        "a_q_norm": gain(14, (NE, HEAD_DIM)),
        "a_k_norm": gain(15, (NE, HEAD_DIM)),
        "a_sink": nrm(16, (NE, A_HEADS), 0.5),
        "b_q_norm": gain(17, (NE, HEAD_DIM)),
        "b_k_norm": gain(18, (NE, HEAD_DIM)),
        "b_lambda_q1": nrm(19, (NE, HEAD_DIM), 0.1),
        "b_lambda_k1": nrm(20, (NE, HEAD_DIM), 0.1),
        "b_lambda_q2": nrm(21, (NE, HEAD_DIM), 0.1),
        "b_lambda_k2": nrm(22, (NE, HEAD_DIM), 0.1),
        "b_sub_norm": gain(23, (NE, B_V_DIM)),
        "cd_w_in": nrm(24, (NO, D, CD_IN), D ** -0.5),
        "cd_w_out": nrm(25, (NO, CD_OUT, D), CD_OUT ** -0.5),
        "c_q_a_norm": gain(26, (NO, C_Q_RANK)),
        "c_kv_a_norm": gain(27, (NO, C_KV_RANK)),
        "c_w_uq": nrm(28, (NO, C_Q_RANK, C_HEADS * (C_NOPE + C_ROPE)), C_Q_RANK ** -0.5),
        "c_w_ukv": nrm(29, (NO, C_KV_RANK, C_HEADS * (C_NOPE + C_V)), C_KV_RANK ** -0.5),
        "c_q_nope_norm": gain(30, (NO, C_NOPE)),
        "c_q_rope_norm": gain(31, (NO, C_ROPE)),
        "c_k_nope_norm": gain(32, (NO, C_NOPE)),
        "c_k_rope_norm": gain(33, (NO, C_ROPE)),
        "d_q_norm": gain(34, (NO, HEAD_DIM)),
        "d_k_norm": gain(35, (NO, HEAD_DIM)),
        "d_rpb": nrm(36, (NO, D_HEADS, 2 * D_WIN_ROWS - 1, 2 * D_WIN_COLS - 1), 0.5),
    }


def reference(x, c, ctx, c_ctx, w_mod, b_mod,
              ffn1_w_gate, ffn1_w_up, ffn1_w_down, ffn2_w_gate, ffn2_w_up, ffn2_w_down,
              ab_w_in, ab_w_out, a_q_norm, a_k_norm, a_sink, b_q_norm, b_k_norm,
              b_lambda_q1, b_lambda_k1, b_lambda_q2, b_lambda_k2, b_sub_norm,
              cd_w_in, cd_w_out, c_q_a_norm, c_kv_a_norm, c_w_uq, c_w_ukv,
              c_q_nope_norm, c_q_rope_norm, c_k_nope_norm, c_k_rope_norm,
              d_q_norm, d_k_norm, d_rpb):
    S = x.shape[1]
    rope_head = axial_rope_tables(S, HEAD_DIM)
    rope_mla = axial_rope_tables(S, C_ROPE)
    c_act = jax.nn.silu(c)
    c_ctx_act = jax.nn.silu(c_ctx)
    xc = ctx
    for l in range(DEPTH):
        need_ctx = l < DEPTH - 1
        mx = jnp.split((c_act @ w_mod[l] + b_mod[l])[:, None, :], N_MOD, axis=-1)
        mc = jnp.split((c_ctx_act @ w_mod[l] + b_mod[l])[None, None, :], N_MOD, axis=-1)
        ffn1 = (ffn1_w_gate[l], ffn1_w_up[l], ffn1_w_down[l])
        ffn2 = (ffn2_w_gate[l], ffn2_w_up[l], ffn2_w_down[l])
        x = x + 0.5 * mx[2] * swiglu(modulate(rms_norm(x), mx[0], mx[1]), *ffn1)
        xc = xc + 0.5 * mc[2] * swiglu(modulate(rms_norm(xc), mc[0], mc[1]), *ffn1)
        h = modulate(rms_norm(x), mx[3], mx[4])
        hc = modulate(rms_norm(xc), mc[3], mc[4])
        i = l // 2
        if l % 2 == 0:
            lam_init = 0.8 - 0.6 * math.exp(-0.3 * l)
            y, y_c = mix_window_diff(h, hc, ab_w_in[i], ab_w_out[i], a_q_norm[i], a_k_norm[i], a_sink[i],
                                     b_q_norm[i], b_k_norm[i], b_lambda_q1[i], b_lambda_k1[i],
                                     b_lambda_q2[i], b_lambda_k2[i], b_sub_norm[i], lam_init,
                                     rope_head, need_ctx)
        else:
            y, y_c = mix_mla_neighbourhood(h, hc, cd_w_in[i], cd_w_out[i], c_q_a_norm[i], c_kv_a_norm[i],
                                           c_w_uq[i], c_w_ukv[i], c_q_nope_norm[i], c_q_rope_norm[i],
                                           c_k_nope_norm[i], c_k_rope_norm[i], d_q_norm[i], d_k_norm[i],
                                           d_rpb[i], rope_mla, need_ctx)
        x = x + mx[5] * y
        x = x + 0.5 * mx[8] * swiglu(modulate(rms_norm(x), mx[6], mx[7]), *ffn2)
        if need_ctx:
            xc = xc + mc[5] * y_c
            xc = xc + 0.5 * mc[8] * swiglu(modulate(rms_norm(xc), mc[6], mc[7]), *ffn2)
    return x
```

```python
import functools
import math

import jax
import jax.numpy as jnp
import numpy as np
from jax import lax
from jax.experimental import pallas as pl
from jax.experimental.pallas import tpu as pltpu

F32 = jnp.float32
BF16 = jnp.bfloat16

D_MODEL = 1024
GRID_W = 64
HEAD_DIM = 64
D_FF = 2816
N_MOD = 9
ROPE_BASE = 10000.0
EPS = 1e-6
NEG = -1e30
A_HEADS = 8
A_KV_HEADS = 2
A_WINDOW = 128
B_HEADS = 4
B_V_DIM = 2 * HEAD_DIM
C_HEADS = 8
C_Q_RANK = 768
C_KV_RANK = 256
C_NOPE = 64
C_ROPE = 32
C_V = 64
C_PAD = 128
D_HEADS = 8
D_WIN_ROWS = 8
D_WIN_COLS = 16

LANES = 128
VMEM_LIMIT_BYTES = 56 * 1024 * 1024

FFN_CHUNK = 256
TOKEN_TILE = 512
ATTN_Q_TILE = 256
NAT_Q_ROWS = 2
NAT_WIN_ROWS = 10


def _params():
    return pltpu.CompilerParams(vmem_limit_bytes=VMEM_LIMIT_BYTES)


def _resident(block_shape, index_map):
    return pl.BlockSpec(block_shape, index_map, pipeline_mode=pl.Buffered(1))


def _sigmoid(x):
    return 1.0 / (1.0 + jnp.exp(-x))


def _rms_rows(z, gain):
    ms = jnp.mean(z * z, axis=0, keepdims=True)
    return z * lax.rsqrt(ms + EPS) * gain


def _norm_modulate(x, shift, scale):
    r = lax.rsqrt(jnp.mean(x * x, axis=-1, keepdims=True) + EPS)
    return (x * r) * (1.0 + scale) + shift


def _rope_rows(y, cos, sin, nf):
    part = jnp.concatenate([y[nf:2 * nf], y[0:nf], y[3 * nf:4 * nf], y[2 * nf:3 * nf]], axis=0)
    return y * cos + part * sin


def _mod_kernel(c_ref, w_ref, b_ref, o_ref):
    c = c_ref[...]
    a = (c * _sigmoid(c)).astype(BF16)
    o_ref[0] = jnp.dot(a, w_ref[0].astype(BF16), preferred_element_type=F32) + b_ref[0]


def _mod_vectors(c_rows, w_mod, b_mod):
    depth, d, n = w_mod.shape
    rows = c_rows.shape[0]
    tn = 1152
    return pl.pallas_call(
        _mod_kernel,
        out_shape=jax.ShapeDtypeStruct((depth, rows, n), F32),
        grid=(depth, n // tn),
        in_specs=[
            pl.BlockSpec((rows, d), lambda l, j: (0, 0)),
            pl.BlockSpec((1, d, tn), lambda l, j: (l, 0, j)),
            pl.BlockSpec((1, 1, tn), lambda l, j: (l, 0, j)),
        ],
        out_specs=pl.BlockSpec((1, rows, tn), lambda l, j: (l, 0, j)),
        compiler_params=_params(),
        name="mod_vectors",
    )(c_rows, w_mod, b_mod.reshape(depth, 1, n))


def _ffn_kernel(*refs, j0, fuse_out):
    if fuse_out:
        x_ref, y1_ref, y2_ref, mod_ref, wo_ref, wg_ref, wu_ref, wd_ref, o_ref, acc_ref = refs
    else:
        x_ref, mod_ref, wg_ref, wu_ref, wd_ref, o_ref, acc_ref = refs
    x = x_ref[0]
    if fuse_out:
        half = y1_ref.shape[-1]
        y = (jnp.dot(y1_ref[0], wo_ref[0:half, :], preferred_element_type=F32)
             + jnp.dot(y2_ref[0], wo_ref[half:, :], preferred_element_type=F32))
        x = x + mod_ref[0, 5:6, :] * y
    shift = mod_ref[0, j0:j0 + 1, :]
    scale = mod_ref[0, j0 + 1:j0 + 2, :]
    gate = mod_ref[0, j0 + 2:j0 + 3, :]
    h = _norm_modulate(x, shift, scale).astype(BF16)
    d_ff = wg_ref.shape[1]
    for c in range(d_ff // FFN_CHUNK):
        lo, hi = c * FFN_CHUNK, (c + 1) * FFN_CHUNK
        g = jnp.dot(h, wg_ref[:, lo:hi], preferred_element_type=F32)
        u = jnp.dot(h, wu_ref[:, lo:hi], preferred_element_type=F32)
        a = (g * _sigmoid(g) * u).astype(BF16)
        part = jnp.dot(a, wd_ref[lo:hi, :], preferred_element_type=F32)
        if c == 0:
            acc_ref[...] = part
        else:
            acc_ref[...] += part
    o_ref[0] = x + (0.5 * gate) * acc_ref[...]


def _ffn(x, mods, mod_row, wg, wu, wd, *, j0, y=None, w_out=None):
    nb, t, d = x.shape
    tm = min(TOKEN_TILE, t)
    d_ff = wg.shape[1]
    fuse_out = y is not None
    tok = lambda b, i: (b, i, 0)
    const = lambda b, i: (0, 0)
    in_specs = [pl.BlockSpec((1, tm, d), tok)]
    args = [x]
    if fuse_out:
        half = y[0].shape[-1]
        in_specs += [pl.BlockSpec((1, tm, half), tok), pl.BlockSpec((1, tm, half), tok)]
        args += [y[0], y[1]]
    in_specs.append(pl.BlockSpec((1, N_MOD, d), lambda b, i: (mod_row(b), 0, 0)))
    args.append(mods)
    if fuse_out:
        in_specs.append(_resident(w_out.shape, const))
        args.append(w_out)
    in_specs += [_resident((d, d_ff), const), _resident((d, d_ff), const), _resident((d_ff, d), const)]
    args += [wg, wu, wd]
    return pl.pallas_call(
        functools.partial(_ffn_kernel, j0=j0, fuse_out=fuse_out),
        out_shape=jax.ShapeDtypeStruct(x.shape, F32),
        grid=(nb, t // tm),
        in_specs=in_specs,
        out_specs=pl.BlockSpec((1, tm, d), tok),
        scratch_shapes=[pltpu.VMEM((tm, d), F32)],
        compiler_params=_params(),
        name="ffn_out" if fuse_out else "ffn",
    )(*args)


def _proj_ab_kernel(*refs, rope, need_q):
    x_ref, mod_ref, wt_ref, wv_ref, gain_ref = refs[:5]
    refs = refs[5:]
    if rope:
        cos_ref, sin_ref = refs[:2]
        refs = refs[2:]
    if need_q:
        aq_ref, bq_ref = refs[:2]
        refs = refs[2:]
    akt_ref, bkt_ref, av_ref, bv_ref = refs

    h = _norm_modulate(x_ref[0], mod_ref[0, 3:4, :], mod_ref[0, 4:5, :])
    hb = h.astype(BF16)
    ht = h.T.astype(BF16)
    v = jnp.dot(hb, wv_ref[...], preferred_element_type=F32)
    n_av = av_ref.shape[-1]
    av_ref[0] = v[:, :n_av].astype(BF16)
    bv_ref[0] = v[:, n_av:].astype(BF16)

    zt = jnp.dot(wt_ref[...], ht, preferred_element_type=F32)
    hd = HEAD_DIM
    g_aq = gain_ref[0 * hd:1 * hd]
    g_bq = gain_ref[1 * hd:2 * hd]
    g_ak = gain_ref[2 * hd:3 * hd]
    g_bk = gain_ref[3 * hd:4 * hd]
    if rope:
        cos = cos_ref[...]
        sin = sin_ref[...]

    def head(row0, gain):
        y = _rms_rows(zt[row0:row0 + hd], gain)
        if rope:
            y = _rope_rows(y, cos, sin, hd // 4)
        return y

    n_aq = A_HEADS * hd
    n_bq = B_HEADS * 2 * hd
    n_ak = A_KV_HEADS * hd
    if need_q:
        scale = hd ** -0.5
        for j in range(n_aq // LANES):
            pair = jnp.concatenate([head(j * LANES, g_aq), head(j * LANES + hd, g_aq)], axis=0) * scale
            aq_ref[0, :, j * LANES:(j + 1) * LANES] = pair.T.astype(BF16)
        for j in range(n_bq // LANES):
            r0 = n_aq + j * LANES
            pair = jnp.concatenate([head(r0, g_bq), head(r0 + hd, g_bq)], axis=0) * scale
            bq_ref[0, :, j * LANES:(j + 1) * LANES] = pair.T.astype(BF16)
    k0 = n_aq + n_bq
    for j in range(n_ak // hd):
        akt_ref[0, j * hd:(j + 1) * hd, :] = head(k0 + j * hd, g_ak).astype(BF16)
    k0 += n_ak
    for j in range(n_bq // hd):
        bkt_ref[0, j * hd:(j + 1) * hd, :] = head(k0 + j * hd, g_bk).astype(BF16)


def _proj_ab(x, mods, mod_row, wt, wv, gains, rope_tabs, *, need_q):
    nb, t, d = x.shape
    tm = min(TOKEN_TILE, t)
    rope = rope_tabs is not None
    hd = HEAD_DIM
    n_aq, n_bq, n_ak = A_HEADS * hd, B_HEADS * 2 * hd, A_KV_HEADS * hd
    n_av, n_bv = A_KV_HEADS * hd, B_HEADS * B_V_DIM
    tok = lambda b, i: (b, i, 0)
    ftok = lambda b, i: (b, 0, i)
    const = lambda b, i: (0, 0)
    in_specs = [
        pl.BlockSpec((1, tm, d), tok),
        pl.BlockSpec((1, N_MOD, d), lambda b, i: (mod_row(b), 0, 0)),
        _resident(wt.shape, const),
        _resident(wv.shape, const),
        _resident(gains.shape, const),
    ]
    args = [x, mods, wt, wv, gains]
    if rope:
        in_specs += [pl.BlockSpec((hd, tm), lambda b, i: (0, i))] * 2
        args += list(rope_tabs)
    out_shape, out_specs = [], []
    if need_q:
        out_shape += [jax.ShapeDtypeStruct((nb, t, n_aq), BF16), jax.ShapeDtypeStruct((nb, t, n_bq), BF16)]
        out_specs += [pl.BlockSpec((1, tm, n_aq), tok), pl.BlockSpec((1, tm, n_bq), tok)]
    out_shape += [jax.ShapeDtypeStruct((nb, n_ak, t), BF16), jax.ShapeDtypeStruct((nb, n_bq, t), BF16),
                  jax.ShapeDtypeStruct((nb, t, n_av), BF16), jax.ShapeDtypeStruct((nb, t, n_bv), BF16)]
    out_specs += [pl.BlockSpec((1, n_ak, tm), ftok), pl.BlockSpec((1, n_bq, tm), ftok),
                  pl.BlockSpec((1, tm, n_av), tok), pl.BlockSpec((1, tm, n_bv), tok)]
    return pl.pallas_call(
        functools.partial(_proj_ab_kernel, rope=rope, need_q=need_q),
        out_shape=out_shape,
        grid=(nb, t // tm),
        in_specs=in_specs,
        out_specs=out_specs,
        compiler_params=_params(),
        name="proj_ab",
    )(*args)


def _proj_cd_kernel(*refs, rope, need_q):
    (x_ref, mod_ref, wt_ref, wdv_ref, wuq_ref, wuk_ref, wuv_ref,
     g_qa_ref, g_kva_ref, g_head_ref) = refs[:10]
    refs = refs[10:]
    if rope:
        cos_ref, sin_ref = refs[:2]
        refs = refs[2:]
    if need_q:
        cq_ref, dq_ref = refs[:2]
        refs = refs[2:]
    ckt_ref, cv_ref, dkt_ref, dv_ref = refs

    h = _norm_modulate(x_ref[0], mod_ref[0, 3:4, :], mod_ref[0, 4:5, :])
    hb = h.astype(BF16)
    ht = h.T.astype(BF16)
    tm = hb.shape[0]
    dv_ref[0] = jnp.dot(hb, wdv_ref[...], preferred_element_type=F32).astype(BF16)

    hd = HEAD_DIM
    n_cq = C_Q_RANK
    n_dq = D_HEADS * hd
    n_ckv = C_KV_RANK + C_ROPE
    r_cq, r_dq, r_ckv, r_dk = 0, n_cq, n_cq + n_dq, n_cq + n_dq + n_ckv
    if not need_q:
        r_ckv, r_dk = 0, n_ckv
    zt = jnp.dot(wt_ref[...], ht, preferred_element_type=F32)

    g_qn = g_head_ref[0:64]
    g_qr = g_head_ref[64:96]
    g_kn = g_head_ref[96:160]
    g_kr = g_head_ref[160:192]
    g_dq = g_head_ref[192:256]
    g_dk = g_head_ref[256:320]
    if rope:
        cos = cos_ref[...]
        sin = sin_ref[...]
    zpad = jnp.zeros((C_PAD - C_NOPE - C_ROPE, tm), F32)

    if need_q:
        cqn = _rms_rows(zt[r_cq:r_cq + n_cq], g_qa_ref[...]).astype(BF16)
        qt = jnp.dot(wuq_ref[...], cqn, preferred_element_type=F32)
        cscale = (C_NOPE + C_ROPE) ** -0.5
        for hh in range(C_HEADS):
            qn = _rms_rows(qt[hh * C_NOPE:(hh + 1) * C_NOPE], g_qn)
            r0 = C_HEADS * C_NOPE + hh * C_ROPE
            qr = _rms_rows(qt[r0:r0 + C_ROPE], g_qr)
            if rope:
                qr = _rope_rows(qr, cos, sin, C_ROPE // 4)
            qh = jnp.concatenate([qn, qr, zpad], axis=0) * cscale
            cq_ref[0, :, hh * C_PAD:(hh + 1) * C_PAD] = qh.T.astype(BF16)
        dscale = hd ** -0.5
        for j in range(n_dq // LANES):
            r0 = r_dq + j * LANES
            pair = jnp.concatenate([_rms_rows(zt[r0:r0 + hd], g_dq),
                                    _rms_rows(zt[r0 + hd:r0 + 2 * hd], g_dq)], axis=0) * dscale
            dq_ref[0, :, j * LANES:(j + 1) * LANES] = pair.T.astype(BF16)

    cn = _rms_rows(zt[r_ckv:r_ckv + C_KV_RANK], g_kva_ref[...])
    kr = _rms_rows(zt[r_ckv + C_KV_RANK:r_ckv + n_ckv], g_kr)
    if rope:
        kr = _rope_rows(kr, cos, sin, C_ROPE // 4)
    knt = jnp.dot(wuk_ref[...], cn.astype(BF16), preferred_element_type=F32)
    for hh in range(C_HEADS):
        kn = _rms_rows(knt[hh * C_NOPE:(hh + 1) * C_NOPE], g_kn)
        ckt_ref[0, hh * C_PAD:(hh + 1) * C_PAD, :] = jnp.concatenate([kn, kr, zpad], axis=0).astype(BF16)
    cv_ref[0] = jnp.dot(cn.T.astype(BF16), wuv_ref[...], preferred_element_type=F32).astype(BF16)
    for hh in range(D_HEADS):
        r0 = r_dk + hh * hd
        dkt_ref[0, hh * hd:(hh + 1) * hd, :] = _rms_rows(zt[r0:r0 + hd], g_dk).astype(BF16)


def _proj_cd(x, mods, mod_row, wt, wdv, wuq, wuk, wuv, g_qa, g_kva, g_head, rope_tabs, *, need_q):
    nb, t, d = x.shape
    tm = min(TOKEN_TILE, t)
    rope = rope_tabs is not None
    hd = HEAD_DIM
    n_c = C_HEADS * C_PAD
    n_d = D_HEADS * hd
    n_cv = C_HEADS * C_V
    tok = lambda b, i: (b, i, 0)
    ftok = lambda b, i: (b, 0, i)
    const = lambda b, i: (0, 0)
    consts = [wt, wdv, wuq, wuk, wuv, g_qa, g_kva, g_head]
    in_specs = [pl.BlockSpec((1, tm, d), tok),
                pl.BlockSpec((1, N_MOD, d), lambda b, i: (mod_row(b), 0, 0))]
    in_specs += [_resident(a.shape, const) for a in consts]
    args = [x, mods] + consts
    if rope:
        in_specs += [pl.BlockSpec((C_ROPE, tm), lambda b, i: (0, i))] * 2
        args += list(rope_tabs)
    out_shape, out_specs = [], []
    if need_q:
        out_shape += [jax.ShapeDtypeStruct((nb, t, n_c), BF16), jax.ShapeDtypeStruct((nb, t, n_d), BF16)]
        out_specs += [pl.BlockSpec((1, tm, n_c), tok), pl.BlockSpec((1, tm, n_d), tok)]
    out_shape += [jax.ShapeDtypeStruct((nb, n_c, t), BF16), jax.ShapeDtypeStruct((nb, t, n_cv), BF16),
                  jax.ShapeDtypeStruct((nb, n_d, t), BF16), jax.ShapeDtypeStruct((nb, t, n_d), BF16)]
    out_specs += [pl.BlockSpec((1, n_c, tm), ftok), pl.BlockSpec((1, tm, n_cv), tok),
                  pl.BlockSpec((1, n_d, tm), ftok), pl.BlockSpec((1, tm, n_d), tok)]
    return pl.pallas_call(
        functools.partial(_proj_cd_kernel, rope=rope, need_q=need_q),
        out_shape=out_shape,
        grid=(nb, t // tm),
        in_specs=in_specs,
        out_specs=out_specs,
        compiler_params=_params(),
        name="proj_cd",
    )(*args)


def _softmax_parts(scores, extra=None):
    m = scores[0].max(axis=-1, keepdims=True)
    for s in scores[1:]:
        m = jnp.maximum(m, s.max(axis=-1, keepdims=True))
    if extra is not None:
        m = jnp.maximum(m, extra)
    es = [jnp.exp(s - m) for s in scores]
    denom = es[0].sum(axis=-1, keepdims=True)
    for e in es[1:]:
        denom = denom + e.sum(axis=-1, keepdims=True)
    if extra is not None:
        denom = denom + jnp.exp(extra - m)
    return es, 1.0 / denom


def _low_lanes(shape):
    return lax.broadcasted_iota(jnp.int32, shape, 1) < HEAD_DIM


def _attn_a_kernel(*refs, latent, seq):
    if latent:
        (sink_ref, q_ref, kp_ref, kc_ref, kn_ref, kx_ref, vp_ref, vc_ref, vn_ref, vx_ref, o_ref) = refs
    else:
        sink_ref, q_ref, kx_ref, vx_ref, o_ref = refs
    q = q_ref[0]
    tq = q.shape[0]
    hd = HEAD_DIM
    if latent:
        kwin = jnp.concatenate([kp_ref[0], kc_ref[0], kn_ref[0], kx_ref[0]], axis=1)
        vwin = jnp.concatenate([vp_ref[0], vc_ref[0], vn_ref[0], vx_ref[0]], axis=0)
        n_lat = tq + 2 * A_WINDOW
        ncols = kwin.shape[1]
        s0 = pl.program_id(1) * tq
        col = lax.broadcasted_iota(jnp.int32, (tq, ncols), 1)
        row = lax.broadcasted_iota(jnp.int32, (tq, ncols), 0)
        kpos = col + (s0 - A_WINDOW)
        inside = jnp.where(jnp.abs(col - A_WINDOW - row) <= A_WINDOW,
                           jnp.where(kpos >= 0, jnp.where(kpos < seq, 0.0, NEG), NEG), NEG)
        bias = jnp.where(col >= n_lat, 0.0, inside)
    else:
        kwin = kx_ref[0]
        vwin = vx_ref[0]
    vswap = pltpu.roll(vwin.astype(F32), hd, 1).astype(BF16)
    low = _low_lanes((tq, LANES))
    group = A_HEADS // A_KV_HEADS
    for j in range(A_HEADS // 2):
        g = (2 * j) // group
        outs = []
        for hh in (2 * j, 2 * j + 1):
            s = jnp.dot(q[:, hh * hd:(hh + 1) * hd], kwin[g * hd:(g + 1) * hd, :],
                        preferred_element_type=F32)
            if latent:
                s = s + bias
            (e,), inv = _softmax_parts([s], sink_ref[hh])
            want_low = hh % 2 == 0
            v_use = vwin if (g == 0) == want_low else vswap
            outs.append(jnp.dot(e.astype(BF16), v_use, preferred_element_type=F32) * inv)
        o_ref[0, :, j * LANES:(j + 1) * LANES] = jnp.where(low, outs[0], outs[1]).astype(BF16)


def _attn_a(sink, q, kt, v, ktx, vx, *, latent):
    nb, t, nq = q.shape
    n_kv = A_KV_HEADS * HEAD_DIM
    lx = ktx.shape[2]
    smem = pl.BlockSpec(memory_space=pltpu.SMEM)
    if latent:
        tq = ATTN_Q_TILE
        seq = kt.shape[2]
        w = A_WINDOW
        r = tq // w
        last = seq // w - 1
        prev = lambda i: jnp.maximum(i * r - 1, 0)
        nxt = lambda i: jnp.minimum((i + 1) * r, last)
        in_specs = [
            smem,
            pl.BlockSpec((1, tq, nq), lambda b, i: (b, i, 0)),
            pl.BlockSpec((1, n_kv, w), lambda b, i: (b, 0, prev(i))),
            pl.BlockSpec((1, n_kv, tq), lambda b, i: (b, 0, i)),
            pl.BlockSpec((1, n_kv, w), lambda b, i: (b, 0, nxt(i))),
            pl.BlockSpec((1, n_kv, lx), lambda b, i: (b, 0, 0)),
            pl.BlockSpec((1, w, n_kv), lambda b, i: (b, prev(i), 0)),
            pl.BlockSpec((1, tq, n_kv), lambda b, i: (b, i, 0)),
            pl.BlockSpec((1, w, n_kv), lambda b, i: (b, nxt(i), 0)),
            pl.BlockSpec((1, lx, n_kv), lambda b, i: (b, 0, 0)),
        ]
        args = [sink, q, kt, kt, kt, ktx, v, v, v, vx]
    else:
        tq = t
        seq = 0
        in_specs = [
            smem,
            pl.BlockSpec((1, tq, nq), lambda b, i: (b, i, 0)),
            pl.BlockSpec((1, n_kv, lx), lambda b, i: (b, 0, 0)),
            pl.BlockSpec((1, lx, n_kv), lambda b, i: (b, 0, 0)),
        ]
        args = [sink, q, ktx, vx]
    return pl.pallas_call(
        functools.partial(_attn_a_kernel, latent=latent, seq=seq),
        out_shape=jax.ShapeDtypeStruct((nb, t, nq), BF16),
        grid=(nb, t // tq),
        in_specs=in_specs,
        out_specs=pl.BlockSpec((1, tq, nq), lambda b, i: (b, i, 0)),
        compiler_params=_params(),
        name="attn_window",
    )(*args)


def _attn_b_kernel(*refs, latent, lam_init):
    if latent:
        lam_ref, gain_ref, q_ref, kt_ref, ktx_ref, v_ref, vx_ref, o_ref = refs
    else:
        lam_ref, gain_ref, q_ref, ktx_ref, vx_ref, o_ref = refs
    q = q_ref[0]
    hd = HEAD_DIM
    lv = lam_ref[...]
    lam = (jnp.exp(jnp.sum(lv[0:1] * lv[1:2], axis=-1, keepdims=True))
           - jnp.exp(jnp.sum(lv[2:3] * lv[3:4], axis=-1, keepdims=True)) + lam_init)
    gain = gain_ref[...] * (1.0 - lam_init)
    for hh in range(B_HEADS):
        parts = []
        for m in range(2):
            r0 = (2 * hh + m) * hd
            qm = q[:, r0:r0 + hd]
            scores = [jnp.dot(qm, ktx_ref[0, r0:r0 + hd, :], preferred_element_type=F32)]
            if latent:
                scores.append(jnp.dot(qm, kt_ref[0, r0:r0 + hd, :], preferred_element_type=F32))
            parts.append(_softmax_parts(scores))
        (e1, i1), (e2, i2) = parts
        i2 = i2 * lam
        c0 = hh * B_V_DIM
        wx = (e1[0] * i1 - e2[0] * i2).astype(BF16)
        o = jnp.dot(wx, vx_ref[0, :, c0:c0 + B_V_DIM], preferred_element_type=F32)
        if latent:
            wl = (e1[1] * i1 - e2[1] * i2).astype(BF16)
            o = o + jnp.dot(wl, v_ref[0, :, c0:c0 + B_V_DIM], preferred_element_type=F32)
        r = lax.rsqrt(jnp.mean(o * o, axis=-1, keepdims=True) + EPS)
        o_ref[0, :, c0:c0 + B_V_DIM] = (o * r * gain).astype(BF16)


def _attn_b(lam_vecs, sub_gain, q, kt, v, ktx, vx, *, latent, lam_init):
    nb, t, nq = q.shape
    lx = ktx.shape[2]
    nk = ktx.shape[1]
    nv = vx.shape[2]
    tq = min(ATTN_Q_TILE, t)
    const = lambda b, i: (0, 0)
    in_specs = [pl.BlockSpec(lam_vecs.shape, const), pl.BlockSpec(sub_gain.shape, const),
                pl.BlockSpec((1, tq, nq), lambda b, i: (b, i, 0))]
    args = [lam_vecs, sub_gain, q]
    if latent:
        seq = kt.shape[2]
        in_specs += [pl.BlockSpec((1, nk, seq), lambda b, i: (b, 0, 0)),
                     pl.BlockSpec((1, nk, lx), lambda b, i: (b, 0, 0)),
                     pl.BlockSpec((1, seq, nv), lambda b, i: (b, 0, 0)),
                     pl.BlockSpec((1, lx, nv), lambda b, i: (b, 0, 0))]
        args += [kt, ktx, v, vx]
    else:
        in_specs += [pl.BlockSpec((1, nk, lx), lambda b, i: (b, 0, 0)),
                     pl.BlockSpec((1, lx, nv), lambda b, i: (b, 0, 0))]
        args += [ktx, vx]
    return pl.pallas_call(
        functools.partial(_attn_b_kernel, latent=latent, lam_init=lam_init),
        out_shape=jax.ShapeDtypeStruct((nb, t, nv), BF16),
        grid=(nb, t // tq),
        in_specs=in_specs,
        out_specs=pl.BlockSpec((1, tq, nv), lambda b, i: (b, i, 0)),
        compiler_params=_params(),
        name="attn_diff",
    )(*args)


def _attn_c_kernel(q_ref, kt_ref, ktx_ref, v_ref, vx_ref, o_ref):
    q = q_ref[0]
    tq = q.shape[0]
    low = _low_lanes((tq, LANES))
    for j in range(C_HEADS // 2):
        outs = []
        for hh in (2 * j, 2 * j + 1):
            r0 = hh * C_PAD
            qh = q[:, r0:r0 + C_PAD]
            sx = jnp.dot(qh, ktx_ref[0, r0:r0 + C_PAD, :], preferred_element_type=F32)
            sl = jnp.dot(qh, kt_ref[0, r0:r0 + C_PAD, :], preferred_element_type=F32)
            (ex, el), inv = _softmax_parts([sx, sl])
            c0 = j * LANES
            o = (jnp.dot(ex.astype(BF16), vx_ref[0, :, c0:c0 + LANES], preferred_element_type=F32)
                 + jnp.dot(el.astype(BF16), v_ref[0, :, c0:c0 + LANES], preferred_element_type=F32))
            outs.append(o * inv)
        o_ref[0, :, j * LANES:(j + 1) * LANES] = jnp.where(low, outs[0], outs[1]).astype(BF16)


def _attn_c(q, kt, v, ktx, vx):
    nb, t, nq = q.shape
    lx = ktx.shape[2]
    nv = v.shape[2]
    tq = ATTN_Q_TILE
    return pl.pallas_call(
        _attn_c_kernel,
        out_shape=jax.ShapeDtypeStruct((nb, t, nv), BF16),
        grid=(nb, t // tq),
        in_specs=[pl.BlockSpec((1, tq, nq), lambda b, i: (b, i, 0)),
                  pl.BlockSpec((1, nq, t), lambda b, i: (b, 0, 0)),
                  pl.BlockSpec((1, nq, lx), lambda b, i: (b, 0, 0)),
                  pl.BlockSpec((1, t, nv), lambda b, i: (b, 0, 0)),
                  pl.BlockSpec((1, lx, nv), lambda b, i: (b, 0, 0))],
        out_specs=pl.BlockSpec((1, tq, nv), lambda b, i: (b, i, 0)),
        compiler_params=_params(),
        name="attn_latent",
    )(q, kt, ktx, v, vx)


def _rpb_table_kernel(rpb_ref, o_ref, *, n_dr, n_dc):
    hh = pl.program_id(0)
    d_left = pl.program_id(1) - 1
    shape = (GRID_W, LANES)
    qc = lax.broadcasted_iota(jnp.int32, shape, 0)
    lane = lax.broadcasted_iota(jnp.int32, shape, 1)
    right = lane >= GRID_W
    kc = jnp.where(right, lane - GRID_W, lane)
    dc = jnp.clip(kc - qc, -(D_WIN_COLS - 1), D_WIN_COLS - 1) + (D_WIN_COLS - 1)
    cs = jnp.clip(qc - D_WIN_COLS // 2, 0, GRID_W - D_WIN_COLS)
    acc = jnp.zeros(shape, F32)
    base_l = (hh * n_dr + jnp.clip(d_left, 0, n_dr - 1)) * n_dc
    base_r = (hh * n_dr + jnp.clip(d_left + 1, 0, n_dr - 1)) * n_dc
    for c in range(n_dc):
        val = jnp.where(right, rpb_ref[base_r + c], rpb_ref[base_l + c])
        acc = jnp.where(dc == c, val, acc)
    ok_l = jnp.where(d_left >= 0, 0.0, NEG)
    ok_r = jnp.where(d_left + 1 <= n_dr - 1, 0.0, NEG)
    acc = acc + jnp.where(right, ok_r, ok_l)
    in_cols = jnp.where(kc >= cs, jnp.where(kc < cs + D_WIN_COLS, 0.0, NEG), NEG)
    o_ref[0, 0] = acc + in_cols


def _rpb_table(rpb):
    n_h, n_dr, n_dc = rpb.shape
    return pl.pallas_call(
        functools.partial(_rpb_table_kernel, n_dr=n_dr, n_dc=n_dc),
        out_shape=jax.ShapeDtypeStruct((n_h, n_dr + 1, GRID_W, LANES), F32),
        grid=(n_h, n_dr + 1),
        in_specs=[pl.BlockSpec(memory_space=pltpu.SMEM)],
        out_specs=pl.BlockSpec((1, 1, GRID_W, LANES), lambda h, d: (h, d, 0, 0)),
        name="rpb_table",
    )(rpb.reshape(-1))


def _attn_d_kernel(tab_ref, q_ref, kt_ref, ktx_ref, v_ref, vx_ref, o_ref, *, rows):
    q = q_ref[0]
    tq = q.shape[0]
    hd = HEAD_DIM
    kh = min(D_WIN_ROWS, rows)
    m2 = pl.program_id(1) * NAT_Q_ROWS
    ws = jnp.clip(m2 - kh // 2, 0, rows - kh)
    wl = jnp.minimum(ws, rows - NAT_WIN_ROWS)
    nwin = NAT_WIN_ROWS * GRID_W
    tok0 = pl.multiple_of(wl * GRID_W, LANES)
    ktw = kt_ref[0, :, pl.ds(tok0, nwin)]
    vw = v_ref[0, pl.ds(tok0, nwin), :]
    lane = lax.broadcasted_iota(jnp.int32, (GRID_W, LANES), 1)
    right = lane >= GRID_W
    n_pairs = NAT_WIN_ROWS // 2
    n_dd = tab_ref.shape[1]

    row_add = []
    for a in range(NAT_Q_ROWS):
        qr = m2 + a
        rs = jnp.clip(qr - kh // 2, 0, rows - kh)
        tiles = []
        for jp in range(n_pairs):
            kr = wl + 2 * jp
            ok_l = jnp.where((kr >= rs) & (kr < rs + kh), 0.0, NEG)
            ok_r = jnp.where((kr + 1 >= rs) & (kr + 1 < rs + kh), 0.0, NEG)
            tiles.append(jnp.where(right, ok_r, ok_l))
        row_add.append(jnp.concatenate(tiles, axis=1))
    row_add = jnp.concatenate(row_add, axis=0)

    low = _low_lanes((tq, LANES))
    for j in range(D_HEADS // 2):
        outs = []
        for hh in (2 * j, 2 * j + 1):
            qh = q[:, hh * hd:(hh + 1) * hd]
            sw = jnp.dot(qh, ktw[hh * hd:(hh + 1) * hd, :], preferred_element_type=F32)
            bias = []
            for a in range(NAT_Q_ROWS):
                tiles = []
                for jp in range(n_pairs):
                    d_left = (wl + 2 * jp) - (m2 + a) + (D_WIN_ROWS - 1)
                    dd = jnp.clip(d_left + 1, 0, n_dd - 1)
                    tiles.append(tab_ref[hh, dd])
                bias.append(jnp.concatenate(tiles, axis=1))
            sw = sw + jnp.concatenate(bias, axis=0) + row_add
            sx = jnp.dot(qh, ktx_ref[0, hh * hd:(hh + 1) * hd, :], preferred_element_type=F32)
            (ew, ex), inv = _softmax_parts([sw, sx])
            c0 = j * LANES
            o = (jnp.dot(ew.astype(BF16), vw[:, c0:c0 + LANES], preferred_element_type=F32)
                 + jnp.dot(ex.astype(BF16), vx_ref[0, :, c0:c0 + LANES], preferred_element_type=F32))
            outs.append(o * inv)
        o_ref[0, :, j * LANES:(j + 1) * LANES] = jnp.where(low, outs[0], outs[1]).astype(BF16)


def _attn_d(table, q, kt, v, ktx, vx):
    nb, t, nq = q.shape
    lx = ktx.shape[2]
    tq = NAT_Q_ROWS * GRID_W
    rows = t // GRID_W
    return pl.pallas_call(
        functools.partial(_attn_d_kernel, rows=rows),
        out_shape=jax.ShapeDtypeStruct((nb, t, nq), BF16),
        grid=(nb, t // tq),
        in_specs=[_resident(table.shape, lambda b, i: (0, 0, 0, 0)),
                  pl.BlockSpec((1, tq, nq), lambda b, i: (b, i, 0)),
                  pl.BlockSpec((1, nq, t), lambda b, i: (b, 0, 0)),
                  pl.BlockSpec((1, nq, lx), lambda b, i: (b, 0, 0)),
                  pl.BlockSpec((1, t, nq), lambda b, i: (b, 0, 0)),
                  pl.BlockSpec((1, lx, nq), lambda b, i: (b, 0, 0))],
        out_specs=pl.BlockSpec((1, tq, nq), lambda b, i: (b, i, 0)),
        compiler_params=_params(),
        name="attn_neighbourhood",
    )(table, q, kt, ktx, v, vx)


def _rope_tables_t(n, rot_dim):
    nf = rot_dim // 4
    inv = ROPE_BASE ** (-jnp.arange(nf, dtype=F32) / nf)
    t = jnp.arange(n)
    row = (t // GRID_W).astype(F32)
    col = (t % GRID_W).astype(F32)
    ang = jnp.concatenate([row[:, None] * inv, col[:, None] * inv], axis=-1)
    cos, sin = jnp.cos(ang).T, jnp.sin(ang).T
    cos_e = jnp.concatenate([cos[:nf], cos[:nf], cos[nf:], cos[nf:]], axis=0)
    sin_e = jnp.concatenate([-sin[:nf], sin[:nf], -sin[nf:], sin[nf:]], axis=0)
    return cos_e, sin_e


def _col(v):
    return v.astype(F32).reshape(-1, 1)


def kernel(x, c, ctx, c_ctx, w_mod, b_mod,
           ffn1_w_gate, ffn1_w_up, ffn1_w_down, ffn2_w_gate, ffn2_w_up, ffn2_w_down,
           ab_w_in, ab_w_out, a_q_norm, a_k_norm, a_sink, b_q_norm, b_k_norm,
           b_lambda_q1, b_lambda_k1, b_lambda_q2, b_lambda_k2, b_sub_norm,
           cd_w_in, cd_w_out, c_q_a_norm, c_kv_a_norm, c_w_uq, c_w_ukv,
           c_q_nope_norm, c_q_rope_norm, c_k_nope_norm, c_k_rope_norm,
           d_q_norm, d_k_norm, d_rpb):
    nb, seq, d = x.shape
    lctx = ctx.shape[1]
    depth = w_mod.shape[0]
    hd = HEAD_DIM

    n_rows = -(-(nb + 1) // 8) * 8
    c_rows = jnp.concatenate([c, c_ctx[None, :], jnp.zeros((n_rows - nb - 1, d), F32)], axis=0)
    mods_all = _mod_vectors(c_rows, w_mod, b_mod).reshape(depth, n_rows, N_MOD, d)
    x_row = lambda b: b
    ctx_row = lambda b: nb

    rope_head = _rope_tables_t(seq, hd)
    rope_mla = _rope_tables_t(seq, C_ROPE)

    xc = ctx.reshape(1, nb * lctx, d)
    for l in range(depth):
        need_ctx = l < depth - 1
        mods = mods_all[l]
        w1 = (ffn1_w_gate[l].astype(BF16), ffn1_w_up[l].astype(BF16), ffn1_w_down[l].astype(BF16))
        w2 = (ffn2_w_gate[l].astype(BF16), ffn2_w_up[l].astype(BF16), ffn2_w_down[l].astype(BF16))
        x = _ffn(x, mods, x_row, *w1, j0=0)
        xc = _ffn(xc, mods, ctx_row, *w1, j0=0)
        xc_b = xc.reshape(nb, lctx, d)
        i = l // 2
        if l % 2 == 0:
            lam_init = 0.8 - 0.6 * math.exp(-0.3 * l)
            w_in = ab_w_in[i]
            n_q = A_HEADS * hd + B_HEADS * 2 * hd
            n_ak = A_KV_HEADS * hd
            n_bk = B_HEADS * 2 * hd
            q_cols = w_in[:, :n_q]
            ak_cols = w_in[:, n_q:n_q + n_ak]
            av_cols = w_in[:, n_q + n_ak:n_q + 2 * n_ak]
            bk_cols = w_in[:, n_q + 2 * n_ak:n_q + 2 * n_ak + n_bk]
            bv_cols = w_in[:, n_q + 2 * n_ak + n_bk:]
            wt = jnp.concatenate([q_cols, ak_cols, bk_cols], axis=1).T.astype(BF16)
            wv = jnp.concatenate([av_cols, bv_cols], axis=1).astype(BF16)
            gains = jnp.concatenate([_col(a_q_norm[i]), _col(b_q_norm[i]),
                                     _col(a_k_norm[i]), _col(b_k_norm[i])], axis=0)
            aq, bq, akt, bkt, av, bv = _proj_ab(x, mods, x_row, wt, wv, gains, rope_head, need_q=True)
            ctx_out = _proj_ab(xc_b, mods, ctx_row, wt, wv, gains, None, need_q=need_ctx)
            aktx, bktx, avx, bvx = ctx_out[-4:]
            sink = a_sink[i].astype(F32)
            lam_vecs = jnp.stack([b_lambda_q1[i], b_lambda_k1[i], b_lambda_q2[i], b_lambda_k2[i]]).astype(F32)
            sub_gain = b_sub_norm[i].astype(F32).reshape(1, -1)
            y1 = _attn_a(sink, aq, akt, av, aktx, avx, latent=True)
            y2 = _attn_b(lam_vecs, sub_gain, bq, bkt, bv, bktx, bvx, latent=True, lam_init=lam_init)
            w_out = ab_w_out[i].astype(BF16)
            if need_ctx:
                aqx, bqx = ctx_out[:2]
                y1x = _attn_a(sink, aqx, None, None, aktx, avx, latent=False)
                y2x = _attn_b(lam_vecs, sub_gain, bqx, None, None, bktx, bvx, latent=False, lam_init=lam_init)
        else:
            w_in = cd_w_in[i]
            n_cq, n_dq = C_Q_RANK, D_HEADS * hd
            n_ckv = C_KV_RANK + C_ROPE
            o_ckv = n_cq + n_dq
            o_dk = o_ckv + n_ckv
            o_dv = o_dk + n_dq
            wt_q = w_in[:, :o_ckv]
            wt_k = jnp.concatenate([w_in[:, o_ckv:o_dk], w_in[:, o_dk:o_dv]], axis=1)
            wt_full = jnp.concatenate([wt_q, wt_k], axis=1).T.astype(BF16)
            wt_kv = wt_k.T.astype(BF16)
            wdv = w_in[:, o_dv:].astype(BF16)
            uq = c_w_uq[i].reshape(C_Q_RANK, C_HEADS, C_NOPE + C_ROPE)
            wuq = jnp.concatenate([uq[:, :, :C_NOPE].reshape(C_Q_RANK, -1),
                                   uq[:, :, C_NOPE:].reshape(C_Q_RANK, -1)], axis=1).T.astype(BF16)
            ukv = c_w_ukv[i].reshape(C_KV_RANK, C_HEADS, C_NOPE + C_V)
            wuk = ukv[:, :, :C_NOPE].reshape(C_KV_RANK, -1).T.astype(BF16)
            wuv = ukv[:, :, C_NOPE:].reshape(C_KV_RANK, -1).astype(BF16)
            g_qa = _col(c_q_a_norm[i])
            g_kva = _col(c_kv_a_norm[i])
            g_head = jnp.concatenate([_col(c_q_nope_norm[i]), _col(c_q_rope_norm[i]),
                                      _col(c_k_nope_norm[i]), _col(c_k_rope_norm[i]),
                                      _col(d_q_norm[i]), _col(d_k_norm[i])], axis=0)
            cq, dq, ckt, cv, dkt, dv = _proj_cd(x, mods, x_row, wt_full, wdv, wuq, wuk, wuv,
                                                g_qa, g_kva, g_head, rope_mla, need_q=True)
            ctx_out = _proj_cd(xc_b, mods, ctx_row, wt_full if need_ctx else wt_kv, wdv, wuq, wuk, wuv,
                               g_qa, g_kva, g_head, None, need_q=need_ctx)
            cktx, cvx, dktx, dvx = ctx_out[-4:]
            y1 = _attn_c(cq, ckt, cv, cktx, cvx)
            y2 = _attn_d(_rpb_table(d_rpb[i].astype(F32)), dq, dkt, dv, dktx, dvx)
            w_out = cd_w_out[i].astype(BF16)
            if need_ctx:
                raise NotImplementedError("context queries of an odd layer are only needed for depth > 2")
        x = _ffn(x, mods, x_row, *w2, j0=6, y=(y1, y2), w_out=w_out)
        if need_ctx:
            half = y1x.shape[-1]
            xc = _ffn(xc, mods, ctx_row, *w2, j0=6,
                      y=(y1x.reshape(1, nb * lctx, half), y2x.reshape(1, nb * lctx, half)), w_out=w_out)
    return x
```

```python
import functools
import math

import jax
import jax.numpy as jnp
from jax import lax
from jax.experimental import pallas as pl
from jax.experimental.pallas import tpu as pltpu

F32 = jnp.float32
BF16 = jnp.bfloat16

D_MODEL = 1024
GRID_W = 64
HEAD_DIM = 64
D_FF = 2816
N_MOD = 9
ROPE_BASE = 10000.0
EPS = 1e-6
NEG = -1e30
LOG2E = math.log2(math.e)
A_HEADS = 8
A_KV_HEADS = 2
A_WINDOW = 128
B_HEADS = 4
B_V_DIM = 2 * HEAD_DIM
C_HEADS = 8
C_Q_RANK = 768
C_KV_RANK = 256
C_NOPE = 64
C_ROPE = 32
C_V = 64
C_PAD = 128
D_HEADS = 8
D_WIN_ROWS = 8
D_WIN_COLS = 16

LANES = 128
VMEM_LIMIT_BYTES = 56 * 1024 * 1024

FFN_CHUNK = 256
TOKEN_TILE = 512
ATTN_Q_TILE = 256
KEY_CHUNK = 256
NAT_Q_ROWS = ATTN_Q_TILE // GRID_W
NAT_WIN_ROWS = 12


def _params():
    return pltpu.CompilerParams(vmem_limit_bytes=VMEM_LIMIT_BYTES)


def _resident(block_shape, index_map):
    return pl.BlockSpec(block_shape, index_map, pipeline_mode=pl.Buffered(1))


def _sigmoid(x):
    return 1.0 / (1.0 + jnp.exp(-x))


def _rms_rows(z, gain):
    ms = jnp.mean(z * z, axis=0, keepdims=True)
    return z * lax.rsqrt(ms + EPS) * gain


def _norm_modulate(x, shift, scale):
    r = lax.rsqrt(jnp.mean(x * x, axis=-1, keepdims=True) + EPS)
    return (x * r) * (1.0 + scale) + shift


def _rope_rows(y, cos, sin, nf):
    part = jnp.concatenate([y[nf:2 * nf], y[0:nf], y[3 * nf:4 * nf], y[2 * nf:3 * nf]], axis=0)
    return y * cos + part * sin


def _mod_kernel(c_ref, w_ref, b_ref, o_ref):
    c = c_ref[...]
    a = (c * _sigmoid(c)).astype(BF16)
    o_ref[0] = jnp.dot(a, w_ref[0].astype(BF16), preferred_element_type=F32) + b_ref[0]


def _mod_vectors(c_rows, w_mod, b_mod):
    depth, d, n = w_mod.shape
    rows = c_rows.shape[0]
    tn = 1152
    return pl.pallas_call(
        _mod_kernel,
        out_shape=jax.ShapeDtypeStruct((depth, rows, n), F32),
        grid=(depth, n // tn),
        in_specs=[
            pl.BlockSpec((rows, d), lambda l, j: (0, 0)),
            pl.BlockSpec((1, d, tn), lambda l, j: (l, 0, j)),
            pl.BlockSpec((1, 1, tn), lambda l, j: (l, 0, j)),
        ],
        out_specs=pl.BlockSpec((1, rows, tn), lambda l, j: (l, 0, j)),
        compiler_params=_params(),
        name="mod_vectors",
    )(c_rows, w_mod, b_mod.reshape(depth, 1, n))


def _ffn_kernel(*refs, j0, fuse_out):
    if fuse_out:
        x_ref, y1_ref, y2_ref, mod_ref, wo_ref, wg_ref, wu_ref, wd_ref, o_ref, acc_ref = refs
    else:
        x_ref, mod_ref, wg_ref, wu_ref, wd_ref, o_ref, acc_ref = refs
    x = x_ref[0]
    if fuse_out:
        half = y1_ref.shape[-1]
        y = (jnp.dot(y1_ref[0], wo_ref[0:half, :], preferred_element_type=F32)
             + jnp.dot(y2_ref[0], wo_ref[half:, :], preferred_element_type=F32))
        x = x + mod_ref[0, 5:6, :] * y
    shift = mod_ref[0, j0:j0 + 1, :]
    scale = mod_ref[0, j0 + 1:j0 + 2, :]
    gate = mod_ref[0, j0 + 2:j0 + 3, :]
    h = _norm_modulate(x, shift, scale).astype(BF16)
    d_ff = wg_ref.shape[1]
    for c in range(d_ff // FFN_CHUNK):
        lo, hi = c * FFN_CHUNK, (c + 1) * FFN_CHUNK
        g = jnp.dot(h, wg_ref[:, lo:hi], preferred_element_type=F32)
        u = jnp.dot(h, wu_ref[:, lo:hi], preferred_element_type=F32)
        a = (g * _sigmoid(g) * u).astype(BF16)
        part = jnp.dot(a, wd_ref[lo:hi, :], preferred_element_type=F32)
        if c == 0:
            acc_ref[...] = part
        else:
            acc_ref[...] += part
    o_ref[0] = x + (0.5 * gate) * acc_ref[...]


def _ffn(x, mods, mod_row, wg, wu, wd, *, j0, y=None, w_out=None):
    nb, t, d = x.shape
    tm = min(TOKEN_TILE, t)
    d_ff = wg.shape[1]
    fuse_out = y is not None
    tok = lambda b, i: (b, i, 0)
    const = lambda b, i: (0, 0)
    in_specs = [pl.BlockSpec((1, tm, d), tok)]
    args = [x]
    if fuse_out:
        half = y[0].shape[-1]
        in_specs += [pl.BlockSpec((1, tm, half), tok), pl.BlockSpec((1, tm, half), tok)]
        args += [y[0], y[1]]
    in_specs.append(pl.BlockSpec((1, N_MOD, d), lambda b, i: (mod_row(b), 0, 0)))
    args.append(mods)
    if fuse_out:
        in_specs.append(_resident(w_out.shape, const))
        args.append(w_out)
    in_specs += [_resident((d, d_ff), const), _resident((d, d_ff), const), _resident((d_ff, d), const)]
    args += [wg, wu, wd]
    return pl.pallas_call(
        functools.partial(_ffn_kernel, j0=j0, fuse_out=fuse_out),
        out_shape=jax.ShapeDtypeStruct(x.shape, F32),
        grid=(nb, t // tm),
        in_specs=in_specs,
        out_specs=pl.BlockSpec((1, tm, d), tok),
        scratch_shapes=[pltpu.VMEM((tm, d), F32)],
        compiler_params=_params(),
        name="ffn_out" if fuse_out else "ffn",
    )(*args)


def _store_token_major(ref, col0, blocks):
    width = sum(b.shape[0] for b in blocks)
    ref[0, :, col0:col0 + width] = jnp.concatenate(blocks, axis=0).T.astype(BF16)


def _proj_ab_kernel(*refs, rope, need_q):
    x_ref, mod_ref, wt_ref, gain_ref = refs[:4]
    refs = refs[4:]
    if rope:
        cos_ref, sin_ref = refs[:2]
        refs = refs[2:]
    if need_q:
        aqt_ref, bqt_ref = refs[:2]
        refs = refs[2:]
    ak_ref, bk_ref, avt_ref, bvt_ref = refs

    h = _norm_modulate(x_ref[0], mod_ref[0, 3:4, :], mod_ref[0, 4:5, :])
    zt = jnp.dot(wt_ref[...], h.T.astype(BF16), preferred_element_type=F32)
    hd = HEAD_DIM
    g_aq = gain_ref[0 * hd:1 * hd]
    g_bq = gain_ref[1 * hd:2 * hd]
    g_ak = gain_ref[2 * hd:3 * hd]
    g_bk = gain_ref[3 * hd:4 * hd]
    if rope:
        cos = cos_ref[...]
        sin = sin_ref[...]

    def head(row0, gain):
        y = _rms_rows(zt[row0:row0 + hd], gain)
        if rope:
            y = _rope_rows(y, cos, sin, hd // 4)
        return y

    n_aq = A_HEADS * hd
    n_bq = B_HEADS * 2 * hd
    n_ak = A_KV_HEADS * hd
    r0 = 0
    if need_q:
        qscale = hd ** -0.5 * LOG2E
        for j in range(A_HEADS):
            aqt_ref[0, j * hd:(j + 1) * hd, :] = (head(j * hd, g_aq) * qscale).astype(BF16)
        for j in range(n_bq // hd):
            bqt_ref[0, j * hd:(j + 1) * hd, :] = (head(n_aq + j * hd, g_bq) * qscale).astype(BF16)
        r0 = n_aq + n_bq
    _store_token_major(ak_ref, 0, [head(r0 + j * hd, g_ak) for j in range(A_KV_HEADS)])
    r0 += n_ak
    for j in range(n_bq // LANES):
        _store_token_major(bk_ref, j * LANES, [head(r0 + j * LANES, g_bk), head(r0 + j * LANES + hd, g_bk)])
    r0 += n_bq
    avt_ref[0] = zt[r0:r0 + n_ak].astype(BF16)
    bvt_ref[0] = zt[r0 + n_ak:].astype(BF16)


def _proj_ab(x, mods, mod_row, wt, gains, rope_tabs, *, need_q):
    nb, t, d = x.shape
    tm = min(TOKEN_TILE, t)
    rope = rope_tabs is not None
    hd = HEAD_DIM
    n_aq, n_bq, n_ak = A_HEADS * hd, B_HEADS * 2 * hd, A_KV_HEADS * hd
    n_av, n_bv = A_KV_HEADS * hd, B_HEADS * B_V_DIM
    tok = lambda b, i: (b, i, 0)
    ftok = lambda b, i: (b, 0, i)
    const = lambda b, i: (0, 0)
    in_specs = [
        pl.BlockSpec((1, tm, d), tok),
        pl.BlockSpec((1, N_MOD, d), lambda b, i: (mod_row(b), 0, 0)),
        _resident(wt.shape, const),
        _resident(gains.shape, const),
    ]
    args = [x, mods, wt, gains]
    if rope:
        in_specs += [pl.BlockSpec((hd, tm), lambda b, i: (0, i))] * 2
        args += list(rope_tabs)
    out_shape, out_specs = [], []
    if need_q:
        out_shape += [jax.ShapeDtypeStruct((nb, n_aq, t), BF16), jax.ShapeDtypeStruct((nb, n_bq, t), BF16)]
        out_specs += [pl.BlockSpec((1, n_aq, tm), ftok), pl.BlockSpec((1, n_bq, tm), ftok)]
    out_shape += [jax.ShapeDtypeStruct((nb, t, n_ak), BF16), jax.ShapeDtypeStruct((nb, t, n_bq), BF16),
                  jax.ShapeDtypeStruct((nb, n_av, t), BF16), jax.ShapeDtypeStruct((nb, n_bv, t), BF16)]
    out_specs += [pl.BlockSpec((1, tm, n_ak), tok), pl.BlockSpec((1, tm, n_bq), tok),
                  pl.BlockSpec((1, n_av, tm), ftok), pl.BlockSpec((1, n_bv, tm), ftok)]
    return pl.pallas_call(
        functools.partial(_proj_ab_kernel, rope=rope, need_q=need_q),
        out_shape=out_shape,
        grid=(nb, t // tm),
        in_specs=in_specs,
        out_specs=out_specs,
        compiler_params=_params(),
        name="proj_ab",
    )(*args)


def _proj_cd_kernel(*refs, rope, need_q):
    (x_ref, mod_ref, wt_ref, wuq_ref, wuk_ref, wuv_ref, g_qa_ref, g_kva_ref, g_head_ref) = refs[:9]
    refs = refs[9:]
    if rope:
        cos_ref, sin_ref = refs[:2]
        refs = refs[2:]
    if need_q:
        cqt_ref, dqt_ref = refs[:2]
        refs = refs[2:]
    ck_ref, cvt_ref, dk_ref, dvt_ref = refs

    h = _norm_modulate(x_ref[0], mod_ref[0, 3:4, :], mod_ref[0, 4:5, :])
    tm = h.shape[0]
    zt = jnp.dot(wt_ref[...], h.T.astype(BF16), preferred_element_type=F32)

    hd = HEAD_DIM
    n_cq = C_Q_RANK
    n_dq = D_HEADS * hd
    n_ckv = C_KV_RANK + C_ROPE
    g_qn = g_head_ref[0:64]
    g_qr = g_head_ref[64:96]
    g_kn = g_head_ref[96:160]
    g_kr = g_head_ref[160:192]
    g_dq = g_head_ref[192:256]
    g_dk = g_head_ref[256:320]
    if rope:
        cos = cos_ref[...]
        sin = sin_ref[...]
    zpad = jnp.zeros((C_PAD - C_NOPE - C_ROPE, tm), F32)

    r0 = 0
    if need_q:
        cqn = _rms_rows(zt[0:n_cq], g_qa_ref[...]).astype(BF16)
        qt = jnp.dot(wuq_ref[...], cqn, preferred_element_type=F32)
        cscale = (C_NOPE + C_ROPE) ** -0.5 * LOG2E
        for hh in range(C_HEADS):
            qn = _rms_rows(qt[hh * C_NOPE:(hh + 1) * C_NOPE], g_qn)
            rr = C_HEADS * C_NOPE + hh * C_ROPE
            qr = _rms_rows(qt[rr:rr + C_ROPE], g_qr)
            if rope:
                qr = _rope_rows(qr, cos, sin, C_ROPE // 4)
            qh = jnp.concatenate([qn, qr, zpad], axis=0) * cscale
            cqt_ref[0, hh * C_PAD:(hh + 1) * C_PAD, :] = qh.astype(BF16)
        dscale = hd ** -0.5 * LOG2E
        for hh in range(D_HEADS):
            rr = n_cq + hh * hd
            dqt_ref[0, hh * hd:(hh + 1) * hd, :] = (_rms_rows(zt[rr:rr + hd], g_dq) * dscale).astype(BF16)
        r0 = n_cq + n_dq

    cn = _rms_rows(zt[r0:r0 + C_KV_RANK], g_kva_ref[...]).astype(BF16)
    kr = _rms_rows(zt[r0 + C_KV_RANK:r0 + n_ckv], g_kr)
    if rope:
        kr = _rope_rows(kr, cos, sin, C_ROPE // 4)
    knt = jnp.dot(wuk_ref[...], cn, preferred_element_type=F32)
    for hh in range(C_HEADS):
        kn = _rms_rows(knt[hh * C_NOPE:(hh + 1) * C_NOPE], g_kn)
        _store_token_major(ck_ref, hh * C_PAD, [kn, kr, zpad])
    cvt_ref[0] = jnp.dot(wuv_ref[...], cn, preferred_element_type=F32).astype(BF16)
    r0 += n_ckv
    for j in range(n_dq // LANES):
        rr = r0 + j * LANES
        _store_token_major(dk_ref, j * LANES, [_rms_rows(zt[rr:rr + hd], g_dk),
                                               _rms_rows(zt[rr + hd:rr + 2 * hd], g_dk)])
    dvt_ref[0] = zt[r0 + n_dq:].astype(BF16)


def _proj_cd(x, mods, mod_row, wt, wuq, wuk, wuv, g_qa, g_kva, g_head, rope_tabs, *, need_q):
    nb, t, d = x.shape
    tm = min(TOKEN_TILE, t)
    rope = rope_tabs is not None
    hd = HEAD_DIM
    n_c = C_HEADS * C_PAD
    n_d = D_HEADS * hd
    n_cv = C_HEADS * C_V
    tok = lambda b, i: (b, i, 0)
    ftok = lambda b, i: (b, 0, i)
    const = lambda b, i: (0, 0)
    consts = [wt, wuq, wuk, wuv, g_qa, g_kva, g_head]
    in_specs = [pl.BlockSpec((1, tm, d), tok),
                pl.BlockSpec((1, N_MOD, d), lambda b, i: (mod_row(b), 0, 0))]
    in_specs += [_resident(a.shape, const) for a in consts]
    args = [x, mods] + consts
    if rope:
        in_specs += [pl.BlockSpec((C_ROPE, tm), lambda b, i: (0, i))] * 2
        args += list(rope_tabs)
    out_shape, out_specs = [], []
    if need_q:
        out_shape += [jax.ShapeDtypeStruct((nb, n_c, t), BF16), jax.ShapeDtypeStruct((nb, n_d, t), BF16)]
        out_specs += [pl.BlockSpec((1, n_c, tm), ftok), pl.BlockSpec((1, n_d, tm), ftok)]
    out_shape += [jax.ShapeDtypeStruct((nb, t, n_c), BF16), jax.ShapeDtypeStruct((nb, n_cv, t), BF16),
                  jax.ShapeDtypeStruct((nb, t, n_d), BF16), jax.ShapeDtypeStruct((nb, n_d, t), BF16)]
    out_specs += [pl.BlockSpec((1, tm, n_c), tok), pl.BlockSpec((1, n_cv, tm), ftok),
                  pl.BlockSpec((1, tm, n_d), tok), pl.BlockSpec((1, n_d, tm), ftok)]
    return pl.pallas_call(
        functools.partial(_proj_cd_kernel, rope=rope, need_q=need_q),
        out_shape=out_shape,
        grid=(nb, t // tm),
        in_specs=in_specs,
        out_specs=out_specs,
        compiler_params=_params(),
        name="proj_cd",
    )(*args)


def _attend_units(n_units, score_pieces, values_t, s_ref, p_ref, extra=None):
    stats = {}
    outs = [None] * n_units

    def stage_scores(u):
        slot = u % 2
        r, m = 0, None
        for piece in score_pieces(u):
            n = piece.shape[0]
            s_ref[slot, r:r + n, :] = piece
            for c in range(0, n, KEY_CHUNK):
                pm = jnp.max(piece[c:c + KEY_CHUNK], axis=0, keepdims=True)
                m = pm if m is None else jnp.maximum(m, pm)
            r += n
        if extra is not None:
            m = jnp.maximum(m, extra(u))
        stats[u] = m

    def stage_numerators(u):
        slot = u % 2
        m = stats[u]
        total = None
        for c in range(0, s_ref.shape[1], KEY_CHUNK):
            e = jnp.exp2(s_ref[slot, c:c + KEY_CHUNK, :] - m)
            part = jnp.sum(e, axis=0, keepdims=True)
            total = part if total is None else total + part
            p_ref[slot, c:c + KEY_CHUNK, :] = e.astype(BF16)
        if extra is not None:
            total = total + jnp.exp2(extra(u) - m)
        stats[u] = 1.0 / total

    def stage_values(u):
        slot = u % 2
        r, o = 0, None
        for vt in values_t(u):
            n = vt.shape[1]
            part = jnp.dot(vt, p_ref[slot, r:r + n, :], preferred_element_type=F32)
            o = part if o is None else o + part
            r += n
        outs[u] = o * stats[u]

    for t in range(n_units + 2):
        if t < n_units:
            stage_scores(t)
        if 0 <= t - 1 < n_units:
            stage_numerators(t - 1)
        if 0 <= t - 2 < n_units:
            stage_values(t - 2)
    return outs


def _half_rhs(q_head, upper):
    zeros = jnp.zeros_like(q_head)
    return jnp.concatenate([zeros, q_head] if upper else [q_head, zeros], axis=0)


def _store_outputs(o_ref, outs):
    rows = jnp.concatenate(outs, axis=0)
    for j in range(rows.shape[0] // LANES):
        o_ref[0, :, j * LANES:(j + 1) * LANES] = rows[j * LANES:(j + 1) * LANES].T.astype(BF16)


def _attn_scratch(n_keys, tq):
    return [pltpu.VMEM((2, n_keys, tq), F32), pltpu.VMEM((2, n_keys, tq), BF16)]


def _attn_a_kernel(*refs, latent, seq):
    if latent:
        (sink_ref, qt_ref, kp_ref, kc_ref, kn_ref, kx_ref, vp_ref, vc_ref, vn_ref, vx_ref,
         o_ref, s_ref, p_ref) = refs
    else:
        sink_ref, qt_ref, kx_ref, vx_ref, o_ref, s_ref, p_ref = refs
    qt = qt_ref[0]
    tq = qt.shape[1]
    hd = HEAD_DIM
    if latent:
        kwin = jnp.concatenate([kp_ref[0], kc_ref[0], kn_ref[0], kx_ref[0]], axis=0)
        vwin = jnp.concatenate([vp_ref[0], vc_ref[0], vn_ref[0], vx_ref[0]], axis=1)
        n_lat = tq + 2 * A_WINDOW
        nk = kwin.shape[0]
        s0 = pl.program_id(1) * tq
        krow = lax.broadcasted_iota(jnp.int32, (nk, tq), 0)
        qcol = lax.broadcasted_iota(jnp.int32, (nk, tq), 1)
        kpos = krow + (s0 - A_WINDOW)
        inside = jnp.where(jnp.abs(krow - A_WINDOW - qcol) <= A_WINDOW,
                           jnp.where(kpos >= 0, jnp.where(kpos < seq, 0.0, NEG), NEG), NEG)
        bias = jnp.where(krow >= n_lat, 0.0, inside)
    else:
        kwin = kx_ref[0]
        vwin = vx_ref[0]
    group = A_HEADS // A_KV_HEADS

    def scores(u):
        s = jnp.dot(kwin, _half_rhs(qt[u * hd:(u + 1) * hd], u // group == 1), preferred_element_type=F32)
        return [s + bias] if latent else [s]

    def values(u):
        g = u // group
        return [vwin[g * hd:(g + 1) * hd]]

    outs = _attend_units(A_HEADS, scores, values, s_ref, p_ref, extra=lambda u: sink_ref[u] * LOG2E)
    _store_outputs(o_ref, outs)


def _attn_a(sink, qt, k, vt, kx, vxt, *, latent):
    nb, nq, t = qt.shape
    n_kv = A_KV_HEADS * HEAD_DIM
    lx = kx.shape[1]
    smem = pl.BlockSpec(memory_space=pltpu.SMEM)
    if latent:
        tq = ATTN_Q_TILE
        seq = k.shape[1]
        w = A_WINDOW
        r = tq // w
        last = seq // w - 1
        prev = lambda i: jnp.maximum(i * r - 1, 0)
        nxt = lambda i: jnp.minimum((i + 1) * r, last)
        in_specs = [
            smem,
            pl.BlockSpec((1, nq, tq), lambda b, i: (b, 0, i)),
            pl.BlockSpec((1, w, n_kv), lambda b, i: (b, prev(i), 0)),
            pl.BlockSpec((1, tq, n_kv), lambda b, i: (b, i, 0)),
            pl.BlockSpec((1, w, n_kv), lambda b, i: (b, nxt(i), 0)),
            pl.BlockSpec((1, lx, n_kv), lambda b, i: (b, 0, 0)),
            pl.BlockSpec((1, n_kv, w), lambda b, i: (b, 0, prev(i))),
            pl.BlockSpec((1, n_kv, tq), lambda b, i: (b, 0, i)),
            pl.BlockSpec((1, n_kv, w), lambda b, i: (b, 0, nxt(i))),
            pl.BlockSpec((1, n_kv, lx), lambda b, i: (b, 0, 0)),
        ]
        args = [sink, qt, k, k, k, kx, vt, vt, vt, vxt]
        n_keys = tq + 2 * w + lx
    else:
        tq = t
        seq = 0
        in_specs = [
            smem,
            pl.BlockSpec((1, nq, tq), lambda b, i: (b, 0, i)),
            pl.BlockSpec((1, lx, n_kv), lambda b, i: (b, 0, 0)),
            pl.BlockSpec((1, n_kv, lx), lambda b, i: (b, 0, 0)),
        ]
        args = [sink, qt, kx, vxt]
        n_keys = lx
    return pl.pallas_call(
        functools.partial(_attn_a_kernel, latent=latent, seq=seq),
        out_shape=jax.ShapeDtypeStruct((nb, t, nq), BF16),
        grid=(nb, t // tq),
        in_specs=in_specs,
        out_specs=pl.BlockSpec((1, tq, nq), lambda b, i: (b, i, 0)),
        scratch_shapes=_attn_scratch(n_keys, tq),
        compiler_params=_params(),
        name="attn_window",
    )(*args)


def _attn_b_kernel(*refs, latent, lam_init):
    if latent:
        lam_ref, gain_ref, qt_ref, k_ref, kx_ref, vt_ref, vxt_ref, o_ref, s_ref, p_ref = refs
    else:
        lam_ref, gain_ref, qt_ref, kx_ref, vxt_ref, o_ref, s_ref, p_ref = refs
    qt = qt_ref[0]
    hd = HEAD_DIM
    lv = lam_ref[...]
    lam = (jnp.exp(jnp.sum(lv[0:1] * lv[1:2], axis=-1, keepdims=True))
           - jnp.exp(jnp.sum(lv[2:3] * lv[3:4], axis=-1, keepdims=True)) + lam_init)
    gain = gain_ref[...] * (1.0 - lam_init)

    def scores(u):
        hh = u // 2
        rhs = _half_rhs(qt[u * hd:(u + 1) * hd], u % 2 == 1)
        cols = slice(hh * 2 * hd, (hh + 1) * 2 * hd)
        pieces = [jnp.dot(kx_ref[0, :, cols], rhs, preferred_element_type=F32)]
        if latent:
            pieces.append(jnp.dot(k_ref[0, :, cols], rhs, preferred_element_type=F32))
        return pieces

    def values(u):
        rows = slice((u // 2) * B_V_DIM, (u // 2 + 1) * B_V_DIM)
        return [vxt_ref[0, rows, :]] + ([vt_ref[0, rows, :]] if latent else [])

    outs = _attend_units(2 * B_HEADS, scores, values, s_ref, p_ref)
    heads = []
    for hh in range(B_HEADS):
        o = outs[2 * hh] - lam * outs[2 * hh + 1]
        heads.append(_rms_rows(o, gain))
    _store_outputs(o_ref, heads)


def _attn_b(lam_vecs, sub_gain, qt, k, vt, kx, vxt, *, latent, lam_init):
    nb, nq, t = qt.shape
    lx, nk = kx.shape[1], kx.shape[2]
    nv = vxt.shape[1]
    tq = min(ATTN_Q_TILE, t)
    const = lambda b, i: (0, 0)
    whole = lambda b, i: (b, 0, 0)
    in_specs = [pl.BlockSpec(lam_vecs.shape, const), pl.BlockSpec(sub_gain.shape, const),
                pl.BlockSpec((1, nq, tq), lambda b, i: (b, 0, i))]
    args = [lam_vecs, sub_gain, qt]
    n_keys = lx
    if latent:
        seq = k.shape[1]
        in_specs += [pl.BlockSpec((1, seq, nk), whole), pl.BlockSpec((1, lx, nk), whole),
                     pl.BlockSpec((1, nv, seq), whole), pl.BlockSpec((1, nv, lx), whole)]
        args += [k, kx, vt, vxt]
        n_keys += seq
    else:
        in_specs += [pl.BlockSpec((1, lx, nk), whole), pl.BlockSpec((1, nv, lx), whole)]
        args += [kx, vxt]
    return pl.pallas_call(
        functools.partial(_attn_b_kernel, latent=latent, lam_init=lam_init),
        out_shape=jax.ShapeDtypeStruct((nb, t, nv), BF16),
        grid=(nb, t // tq),
        in_specs=in_specs,
        out_specs=pl.BlockSpec((1, tq, nv), lambda b, i: (b, i, 0)),
        scratch_shapes=_attn_scratch(n_keys, tq),
        compiler_params=_params(),
        name="attn_diff",
    )(*args)


def _attn_c_kernel(qt_ref, k_ref, kx_ref, vt_ref, vxt_ref, o_ref, s_ref, p_ref):
    qt = qt_ref[0]

    def scores(u):
        rhs = qt[u * C_PAD:(u + 1) * C_PAD]
        cols = slice(u * C_PAD, (u + 1) * C_PAD)
        return [jnp.dot(kx_ref[0, :, cols], rhs, preferred_element_type=F32),
                jnp.dot(k_ref[0, :, cols], rhs, preferred_element_type=F32)]

    def values(u):
        rows = slice(u * C_V, (u + 1) * C_V)
        return [vxt_ref[0, rows, :], vt_ref[0, rows, :]]

    _store_outputs(o_ref, _attend_units(C_HEADS, scores, values, s_ref, p_ref))


def _attn_c(qt, k, vt, kx, vxt):
    nb, nq, t = qt.shape
    lx = kx.shape[1]
    nv = vt.shape[1]
    tq = ATTN_Q_TILE
    whole = lambda b, i: (b, 0, 0)
    return pl.pallas_call(
        _attn_c_kernel,
        out_shape=jax.ShapeDtypeStruct((nb, t, nv), BF16),
        grid=(nb, t // tq),
        in_specs=[pl.BlockSpec((1, nq, tq), lambda b, i: (b, 0, i)),
                  pl.BlockSpec((1, t, nq), whole),
                  pl.BlockSpec((1, lx, nq), whole),
                  pl.BlockSpec((1, nv, t), whole),
                  pl.BlockSpec((1, nv, lx), whole)],
        out_specs=pl.BlockSpec((1, tq, nv), lambda b, i: (b, i, 0)),
        scratch_shapes=_attn_scratch(t + lx, tq),
        compiler_params=_params(),
        name="attn_latent",
    )(qt, k, kx, vt, vxt)


def _rpb_table_kernel(rpb_ref, o_ref, *, n_dr, n_dc):
    hh = pl.program_id(0)
    shape = (GRID_W, LANES)
    kc = lax.broadcasted_iota(jnp.int32, shape, 0)
    lane = lax.broadcasted_iota(jnp.int32, shape, 1)
    right = lane >= GRID_W
    qc = jnp.where(right, lane - GRID_W, lane)
    dc = jnp.clip(kc - qc, -(D_WIN_COLS - 1), D_WIN_COLS - 1) + (D_WIN_COLS - 1)
    cs = jnp.clip(qc - D_WIN_COLS // 2, 0, GRID_W - D_WIN_COLS)
    in_cols = jnp.where(kc >= cs, jnp.where(kc < cs + D_WIN_COLS, 0.0, NEG), NEG)
    for dd in range(n_dr + 1):
        base_l = (hh * n_dr + min(dd, n_dr - 1)) * n_dc
        base_r = (hh * n_dr + max(dd - 1, 0)) * n_dc
        acc = jnp.zeros(shape, F32)
        for c in range(n_dc):
            val = jnp.where(right, rpb_ref[base_r + c], rpb_ref[base_l + c])
            acc = jnp.where(dc == c, val, acc)
        ok = jnp.where(right, 0.0 if dd >= 1 else NEG, 0.0 if dd <= n_dr - 1 else NEG)
        o_ref[0, dd] = acc * LOG2E + ok + in_cols


def _rpb_table(rpb):
    n_h, n_dr, n_dc = rpb.shape
    return pl.pallas_call(
        functools.partial(_rpb_table_kernel, n_dr=n_dr, n_dc=n_dc),
        out_shape=jax.ShapeDtypeStruct((n_h, n_dr + 1, GRID_W, LANES), F32),
        grid=(n_h,),
        in_specs=[pl.BlockSpec(memory_space=pltpu.SMEM)],
        out_specs=pl.BlockSpec((1, n_dr + 1, GRID_W, LANES), lambda h: (h, 0, 0, 0)),
        name="rpb_table",
    )(rpb.reshape(-1))


def _attn_d_kernel(tab_ref, qt_ref, k_ref, kx_ref, vt_ref, vxt_ref, o_ref, s_ref, p_ref, *, rows):
    qt = qt_ref[0]
    hd = HEAD_DIM
    kh = min(D_WIN_ROWS, rows)
    r_first = pl.program_id(1) * NAT_Q_ROWS
    ws = jnp.clip(r_first - kh // 2, 0, rows - kh)
    wl = jnp.minimum(ws, rows - NAT_WIN_ROWS)
    nwin = NAT_WIN_ROWS * GRID_W
    tok0 = pl.multiple_of(wl * GRID_W, LANES)
    kwin = k_ref[0, pl.ds(tok0, nwin), :]
    vwin = vt_ref[0, :, pl.ds(tok0, nwin)]
    lane = lax.broadcasted_iota(jnp.int32, (GRID_W, LANES), 1)
    right = lane >= GRID_W
    n_dd = tab_ref.shape[1]
    q_pairs = NAT_Q_ROWS // 2

    def in_window(kr, qr):
        rs = jnp.clip(qr - kh // 2, 0, rows - kh)
        return jnp.where((kr >= rs) & (kr < rs + kh), 0.0, NEG)

    row_add = jnp.concatenate([
        jnp.concatenate([jnp.where(right, in_window(wl + j, r_first + 2 * a + 1),
                                   in_window(wl + j, r_first + 2 * a))
                         for a in range(q_pairs)], axis=1)
        for j in range(NAT_WIN_ROWS)], axis=0)

    def scores(u):
        rhs = _half_rhs(qt[u * hd:(u + 1) * hd], u % 2 == 1)
        cols = slice((u // 2) * LANES, (u // 2 + 1) * LANES)
        sw = jnp.dot(kwin[:, cols], rhs, preferred_element_type=F32)
        bias = jnp.concatenate([
            jnp.concatenate([tab_ref[u, jnp.clip((wl + j) - (r_first + 2 * a) + (D_WIN_ROWS - 1), 0, n_dd - 1)]
                             for a in range(q_pairs)], axis=1)
            for j in range(NAT_WIN_ROWS)], axis=0)
        sx = jnp.dot(kx_ref[0, :, cols], rhs, preferred_element_type=F32)
        return [sw + (bias + row_add), sx]

    def values(u):
        rows_u = slice(u * hd, (u + 1) * hd)
        return [vwin[rows_u], vxt_ref[0, rows_u, :]]

    _store_outputs(o_ref, _attend_units(D_HEADS, scores, values, s_ref, p_ref))


def _attn_d(table, qt, k, vt, kx, vxt):
    nb, nq, t = qt.shape
    lx = kx.shape[1]
    tq = NAT_Q_ROWS * GRID_W
    rows = t // GRID_W
    assert rows >= NAT_WIN_ROWS and rows % NAT_Q_ROWS == 0 and NAT_Q_ROWS % 2 == 0
    whole = lambda b, i: (b, 0, 0)
    return pl.pallas_call(
        functools.partial(_attn_d_kernel, rows=rows),
        out_shape=jax.ShapeDtypeStruct((nb, t, nq), BF16),
        grid=(nb, t // tq),
        in_specs=[_resident(table.shape, lambda b, i: (0, 0, 0, 0)),
                  pl.BlockSpec((1, nq, tq), lambda b, i: (b, 0, i)),
                  pl.BlockSpec((1, t, nq), whole),
                  pl.BlockSpec((1, lx, nq), whole),
                  pl.BlockSpec((1, nq, t), whole),
                  pl.BlockSpec((1, nq, lx), whole)],
        out_specs=pl.BlockSpec((1, tq, nq), lambda b, i: (b, i, 0)),
        scratch_shapes=_attn_scratch(NAT_WIN_ROWS * GRID_W + lx, tq),
        compiler_params=_params(),
        name="attn_neighbourhood",
    )(table, qt, k, kx, vt, vxt)


def _rope_tables_t(n, rot_dim):
    nf = rot_dim // 4
    inv = ROPE_BASE ** (-jnp.arange(nf, dtype=F32) / nf)
    t = jnp.arange(n)
    row = (t // GRID_W).astype(F32)
    col = (t % GRID_W).astype(F32)
    ang = jnp.concatenate([row[:, None] * inv, col[:, None] * inv], axis=-1)
    cos, sin = jnp.cos(ang).T, jnp.sin(ang).T
    cos_e = jnp.concatenate([cos[:nf], cos[:nf], cos[nf:], cos[nf:]], axis=0)
    sin_e = jnp.concatenate([-sin[:nf], sin[:nf], -sin[nf:], sin[nf:]], axis=0)
    return cos_e, sin_e


def _col(v):
    return v.astype(F32).reshape(-1, 1)


def kernel(x, c, ctx, c_ctx, w_mod, b_mod,
           ffn1_w_gate, ffn1_w_up, ffn1_w_down, ffn2_w_gate, ffn2_w_up, ffn2_w_down,
           ab_w_in, ab_w_out, a_q_norm, a_k_norm, a_sink, b_q_norm, b_k_norm,
           b_lambda_q1, b_lambda_k1, b_lambda_q2, b_lambda_k2, b_sub_norm,
           cd_w_in, cd_w_out, c_q_a_norm, c_kv_a_norm, c_w_uq, c_w_ukv,
           c_q_nope_norm, c_q_rope_norm, c_k_nope_norm, c_k_rope_norm,
           d_q_norm, d_k_norm, d_rpb):
    nb, seq, d = x.shape
    lctx = ctx.shape[1]
    depth = w_mod.shape[0]
    hd = HEAD_DIM

    n_rows = -(-(nb + 1) // 8) * 8
    c_rows = jnp.concatenate([c, c_ctx[None, :], jnp.zeros((n_rows - nb - 1, d), F32)], axis=0)
    mods_all = _mod_vectors(c_rows, w_mod, b_mod).reshape(depth, n_rows, N_MOD, d)
    x_row = lambda b: b
    ctx_row = lambda b: nb

    rope_head = _rope_tables_t(seq, hd)
    rope_mla = _rope_tables_t(seq, C_ROPE)

    xc = ctx.reshape(1, nb * lctx, d)
    for l in range(depth):
        need_ctx = l < depth - 1
        mods = mods_all[l]
        w1 = (ffn1_w_gate[l].astype(BF16), ffn1_w_up[l].astype(BF16), ffn1_w_down[l].astype(BF16))
        w2 = (ffn2_w_gate[l].astype(BF16), ffn2_w_up[l].astype(BF16), ffn2_w_down[l].astype(BF16))
        x = _ffn(x, mods, x_row, *w1, j0=0)
        xc = _ffn(xc, mods, ctx_row, *w1, j0=0)
        xc_b = xc.reshape(nb, lctx, d)
        i = l // 2
        if l % 2 == 0:
            lam_init = 0.8 - 0.6 * math.exp(-0.3 * l)
            w_in = ab_w_in[i]
            n_q = A_HEADS * hd + B_HEADS * 2 * hd
            n_ak = A_KV_HEADS * hd
            n_bk = B_HEADS * 2 * hd
            o_av = n_q + n_ak
            o_bk = o_av + n_ak
            o_bv = o_bk + n_bk
            wt = jnp.concatenate([w_in[:, :o_av], w_in[:, o_bk:o_bv], w_in[:, o_av:o_bk], w_in[:, o_bv:]],
                                 axis=1).T.astype(BF16)
            gains = jnp.concatenate([_col(a_q_norm[i]), _col(b_q_norm[i]),
                                     _col(a_k_norm[i]), _col(b_k_norm[i])], axis=0)
            aqt, bqt, ak, bk, avt, bvt = _proj_ab(x, mods, x_row, wt, gains, rope_head, need_q=True)
            ctx_out = _proj_ab(xc_b, mods, ctx_row, wt if need_ctx else wt[n_q:], gains, None, need_q=need_ctx)
            akx, bkx, avxt, bvxt = ctx_out[-4:]
            sink = a_sink[i].astype(F32)
            lam_vecs = jnp.stack([b_lambda_q1[i], b_lambda_k1[i], b_lambda_q2[i], b_lambda_k2[i]]).astype(F32)
            sub_gain = _col(b_sub_norm[i])
            y1 = _attn_a(sink, aqt, ak, avt, akx, avxt, latent=True)
            y2 = _attn_b(lam_vecs, sub_gain, bqt, bk, bvt, bkx, bvxt, latent=True, lam_init=lam_init)
            w_out = ab_w_out[i].astype(BF16)
            if need_ctx:
                aqxt, bqxt = ctx_out[:2]
                y1x = _attn_a(sink, aqxt, None, None, akx, avxt, latent=False)
                y2x = _attn_b(lam_vecs, sub_gain, bqxt, None, None, bkx, bvxt, latent=False, lam_init=lam_init)
        else:
            assert not need_ctx, "context queries of an odd layer are only needed for depth > 2"
            w_in = cd_w_in[i]
            n_q = C_Q_RANK + D_HEADS * hd
            wt = w_in.T.astype(BF16)
            uq = c_w_uq[i].reshape(C_Q_RANK, C_HEADS, C_NOPE + C_ROPE)
            wuq = jnp.concatenate([uq[:, :, :C_NOPE].reshape(C_Q_RANK, -1),
                                   uq[:, :, C_NOPE:].reshape(C_Q_RANK, -1)], axis=1).T.astype(BF16)
            ukv = c_w_ukv[i].reshape(C_KV_RANK, C_HEADS, C_NOPE + C_V)
            wuk = ukv[:, :, :C_NOPE].reshape(C_KV_RANK, -1).T.astype(BF16)
            wuv = ukv[:, :, C_NOPE:].reshape(C_KV_RANK, -1).T.astype(BF16)
            g_qa = _col(c_q_a_norm[i])
            g_kva = _col(c_kv_a_norm[i])
            g_head = jnp.concatenate([_col(c_q_nope_norm[i]), _col(c_q_rope_norm[i]),
                                      _col(c_k_nope_norm[i]), _col(c_k_rope_norm[i]),
                                      _col(d_q_norm[i]), _col(d_k_norm[i])], axis=0)
            cqt, dqt, ck, cvt, dk, dvt = _proj_cd(x, mods, x_row, wt, wuq, wuk, wuv,
                                                  g_qa, g_kva, g_head, rope_mla, need_q=True)
            ckx, cvxt, dkx, dvxt = _proj_cd(xc_b, mods, ctx_row, wt[n_q:], wuq, wuk, wuv,
                                            g_qa, g_kva, g_head, None, need_q=False)
            y1 = _attn_c(cqt, ck, cvt, ckx, cvxt)
            y2 = _attn_d(_rpb_table(d_rpb[i].astype(F32)), dqt, dk, dvt, dkx, dvxt)
            w_out = cd_w_out[i].astype(BF16)
        x = _ffn(x, mods, x_row, *w2, j0=6, y=(y1, y2), w_out=w_out)
        if need_ctx:
            half = y1x.shape[-1]
            xc = _ffn(xc, mods, ctx_row, *w2, j0=6,
                      y=(y1x.reshape(1, nb * lctx, half), y2x.reshape(1, nb * lctx, half)), w_out=w_out)
    return x
```

```python
import functools
import math

import jax
import jax.numpy as jnp
from jax import lax
from jax.experimental import pallas as pl
from jax.experimental.pallas import tpu as pltpu

F32 = jnp.float32
BF16 = jnp.bfloat16

D_MODEL = 1024
GRID_W = 64
HEAD_DIM = 64
D_FF = 2816
N_MOD = 9
ROPE_BASE = 10000.0
EPS = 1e-6
NEG = -1e30
LOG2E = math.log2(math.e)
A_HEADS = 8
A_KV_HEADS = 2
A_WINDOW = 128
B_HEADS = 4
B_V_DIM = 2 * HEAD_DIM
C_HEADS = 8
C_Q_RANK = 768
C_KV_RANK = 256
C_NOPE = 64
C_ROPE = 32
C_V = 64
C_PAD = 128
D_HEADS = 8
D_WIN_ROWS = 8
D_WIN_COLS = 16

LANES = 128
VMEM_LIMIT_BYTES = 56 * 1024 * 1024

FFN_CHUNK = 256
TOKEN_TILE = 512
ATTN_Q_TILE = 256
FULL_Q_TILE = 512
KEY_CHUNK = 256
ONES_ROWS = 16
NAT_Q_ROWS = ATTN_Q_TILE // GRID_W
NAT_WIN_ROWS = 12


def _params(**flags):
    return pltpu.CompilerParams(vmem_limit_bytes=VMEM_LIMIT_BYTES, flags=flags or None)


def _resident(block_shape, index_map):
    return pl.BlockSpec(block_shape, index_map, pipeline_mode=pl.Buffered(1))


def _sigmoid(x):
    return 1.0 / (1.0 + jnp.exp(-x))


def _rms_rows(z, gain):
    ms = jnp.mean(z * z, axis=0, keepdims=True)
    return z * lax.rsqrt(ms + EPS) * gain


def _norm_modulate(x, shift, scale):
    r = lax.rsqrt(jnp.mean(x * x, axis=-1, keepdims=True) + EPS)
    return (x * r) * (1.0 + scale) + shift


def _rope_rows(y, cos, sin, nf):
    part = jnp.concatenate([y[nf:2 * nf], y[0:nf], y[3 * nf:4 * nf], y[2 * nf:3 * nf]], axis=0)
    return y * cos + part * sin


def _mod_kernel(c_ref, w_ref, b_ref, o_ref):
    c = c_ref[...]
    a = (c * _sigmoid(c)).astype(BF16)
    o_ref[0] = jnp.dot(a, w_ref[0].astype(BF16), preferred_element_type=F32) + b_ref[0]


def _mod_vectors(c_rows, w_mod, b_mod):
    depth, d, n = w_mod.shape
    rows = c_rows.shape[0]
    tn = 1152
    return pl.pallas_call(
        _mod_kernel,
        out_shape=jax.ShapeDtypeStruct((depth, rows, n), F32),
        grid=(depth, n // tn),
        in_specs=[
            pl.BlockSpec((rows, d), lambda l, j: (0, 0)),
            pl.BlockSpec((1, d, tn), lambda l, j: (l, 0, j)),
            pl.BlockSpec((1, 1, tn), lambda l, j: (l, 0, j)),
        ],
        out_specs=pl.BlockSpec((1, rows, tn), lambda l, j: (l, 0, j)),
        compiler_params=_params(),
        name="mod_vectors",
    )(c_rows, w_mod, b_mod.reshape(depth, 1, n))


def _ffn_kernel(*refs, j0, fuse_out):
    if fuse_out:
        x_ref, y1_ref, y2_ref, mod_ref, wo_ref, wg_ref, wu_ref, wd_ref, o_ref, acc_ref = refs
    else:
        x_ref, mod_ref, wg_ref, wu_ref, wd_ref, o_ref, acc_ref = refs
    x = x_ref[0]
    if fuse_out:
        half = y1_ref.shape[-1]
        y = (jnp.dot(y1_ref[0], wo_ref[0:half, :], preferred_element_type=F32)
             + jnp.dot(y2_ref[0], wo_ref[half:, :], preferred_element_type=F32))
        x = x + mod_ref[0, 5:6, :] * y
    shift = mod_ref[0, j0:j0 + 1, :]
    scale = mod_ref[0, j0 + 1:j0 + 2, :]
    gate = mod_ref[0, j0 + 2:j0 + 3, :]
    h = _norm_modulate(x, shift, scale).astype(BF16)
    d_ff = wg_ref.shape[1]
    for c in range(d_ff // FFN_CHUNK):
        lo, hi = c * FFN_CHUNK, (c + 1) * FFN_CHUNK
        g = jnp.dot(h, wg_ref[:, lo:hi], preferred_element_type=F32)
        u = jnp.dot(h, wu_ref[:, lo:hi], preferred_element_type=F32)
        a = (g * _sigmoid(g) * u).astype(BF16)
        part = jnp.dot(a, wd_ref[lo:hi, :], preferred_element_type=F32)
        if c == 0:
            acc_ref[...] = part
        else:
            acc_ref[...] += part
    o_ref[0] = x + (0.5 * gate) * acc_ref[...]


def _ffn(x, mods, mod_row, wg, wu, wd, *, j0, y=None, w_out=None):
    nb, t, d = x.shape
    tm = min(TOKEN_TILE, t)
    d_ff = wg.shape[1]
    fuse_out = y is not None
    tok = lambda b, i: (b, i, 0)
    const = lambda b, i: (0, 0)
    in_specs = [pl.BlockSpec((1, tm, d), tok)]
    args = [x]
    if fuse_out:
        half = y[0].shape[-1]
        in_specs += [pl.BlockSpec((1, tm, half), tok), pl.BlockSpec((1, tm, half), tok)]
        args += [y[0], y[1]]
    in_specs.append(pl.BlockSpec((1, N_MOD, d), lambda b, i: (mod_row(b), 0, 0)))
    args.append(mods)
    if fuse_out:
        in_specs.append(_resident(w_out.shape, const))
        args.append(w_out)
    in_specs += [_resident((d, d_ff), const), _resident((d, d_ff), const), _resident((d_ff, d), const)]
    args += [wg, wu, wd]
    return pl.pallas_call(
        functools.partial(_ffn_kernel, j0=j0, fuse_out=fuse_out),
        out_shape=jax.ShapeDtypeStruct(x.shape, F32),
        grid=(nb, t // tm),
        in_specs=in_specs,
        out_specs=pl.BlockSpec((1, tm, d), tok),
        scratch_shapes=[pltpu.VMEM((tm, d), F32)],
        compiler_params=_params(),
        name="ffn_out" if fuse_out else "ffn",
    )(*args)


def _store_token_major(ref, col0, blocks):
    width = sum(b.shape[0] for b in blocks)
    ref[0, :, col0:col0 + width] = jnp.concatenate(blocks, axis=0).T.astype(BF16)


def _store_values_t(ref, zrows, dv):
    ones = jnp.ones((ONES_ROWS, zrows.shape[1]), F32)
    stride = dv + ONES_ROWS
    for hh in range(zrows.shape[0] // dv):
        block = jnp.concatenate([zrows[hh * dv:(hh + 1) * dv], ones], axis=0)
        ref[0, hh * stride:(hh + 1) * stride, :] = block.astype(BF16)


def _grouped_projection(wt_ref, ht, groups):
    r0, pending = 0, None
    for n, epilogue in groups:
        z = jnp.dot(wt_ref[r0:r0 + n, :], ht, preferred_element_type=F32)
        if pending is not None:
            pending[0](pending[1])
        pending = (epilogue, z)
        r0 += n
    pending[0](pending[1])


def _proj_ab_kernel(*refs, rope, need_q):
    x_ref, mod_ref, wt_ref, gain_ref = refs[:4]
    refs = refs[4:]
    if rope:
        cos_ref, sin_ref = refs[:2]
        refs = refs[2:]
    if need_q:
        aqt_ref, bqt_ref = refs[:2]
        refs = refs[2:]
    ak_ref, bk_ref, avt_ref, bvt_ref = refs

    h = _norm_modulate(x_ref[0], mod_ref[0, 3:4, :], mod_ref[0, 4:5, :])
    ht = h.T.astype(BF16)
    hd = HEAD_DIM
    g_aq = gain_ref[0 * hd:1 * hd]
    g_bq = gain_ref[1 * hd:2 * hd]
    g_ak = gain_ref[2 * hd:3 * hd]
    g_bk = gain_ref[3 * hd:4 * hd]
    if rope:
        cos = cos_ref[...]
        sin = sin_ref[...]

    def head(z, row0, gain):
        y = _rms_rows(z[row0:row0 + hd], gain)
        if rope:
            y = _rope_rows(y, cos, sin, hd // 4)
        return y

    n_aq = A_HEADS * hd
    n_bq = B_HEADS * 2 * hd
    n_ak = A_KV_HEADS * hd
    qscale = hd ** -0.5 * LOG2E

    def queries(ref, gain):
        def epilogue(z):
            for j in range(z.shape[0] // hd):
                ref[0, j * hd:(j + 1) * hd, :] = (head(z, j * hd, gain) * qscale).astype(BF16)
        return epilogue

    def keys(z):
        _store_token_major(ak_ref, 0, [head(z, j * hd, g_ak) for j in range(A_KV_HEADS)])
        for j in range(n_bq // LANES):
            r0 = n_ak + j * LANES
            _store_token_major(bk_ref, j * LANES, [head(z, r0, g_bk), head(z, r0 + hd, g_bk)])

    def values(z):
        _store_values_t(avt_ref, z[:n_ak], hd)
        _store_values_t(bvt_ref, z[n_ak:], B_V_DIM)

    groups = [(n_aq, queries(aqt_ref, g_aq)), (n_bq, queries(bqt_ref, g_bq))] if need_q else []
    groups += [(n_ak + n_bq, keys), (wt_ref.shape[0] - sum(n for n, _ in groups) - n_ak - n_bq, values)]
    _grouped_projection(wt_ref, ht, groups)


def _proj_ab(x, mods, mod_row, wt, gains, rope_tabs, *, need_q):
    nb, t, d = x.shape
    tm = min(TOKEN_TILE, t)
    rope = rope_tabs is not None
    hd = HEAD_DIM
    n_aq, n_bq, n_ak = A_HEADS * hd, B_HEADS * 2 * hd, A_KV_HEADS * hd
    n_av, n_bv = A_KV_HEADS * (hd + ONES_ROWS), B_HEADS * (B_V_DIM + ONES_ROWS)
    tok = lambda b, i: (b, i, 0)
    ftok = lambda b, i: (b, 0, i)
    const = lambda b, i: (0, 0)
    in_specs = [
        pl.BlockSpec((1, tm, d), tok),
        pl.BlockSpec((1, N_MOD, d), lambda b, i: (mod_row(b), 0, 0)),
        _resident(wt.shape, const),
        _resident(gains.shape, const),
    ]
    args = [x, mods, wt, gains]
    if rope:
        in_specs += [pl.BlockSpec((hd, tm), lambda b, i: (0, i))] * 2
        args += list(rope_tabs)
    out_shape, out_specs = [], []
    if need_q:
        out_shape += [jax.ShapeDtypeStruct((nb, n_aq, t), BF16), jax.ShapeDtypeStruct((nb, n_bq, t), BF16)]
        out_specs += [pl.BlockSpec((1, n_aq, tm), ftok), pl.BlockSpec((1, n_bq, tm), ftok)]
    out_shape += [jax.ShapeDtypeStruct((nb, t, n_ak), BF16), jax.ShapeDtypeStruct((nb, t, n_bq), BF16),
                  jax.ShapeDtypeStruct((nb, n_av, t), BF16), jax.ShapeDtypeStruct((nb, n_bv, t), BF16)]
    out_specs += [pl.BlockSpec((1, tm, n_ak), tok), pl.BlockSpec((1, tm, n_bq), tok),
                  pl.BlockSpec((1, n_av, tm), ftok), pl.BlockSpec((1, n_bv, tm), ftok)]
    return pl.pallas_call(
        functools.partial(_proj_ab_kernel, rope=rope, need_q=need_q),
        out_shape=out_shape,
        grid=(nb, t // tm),
        in_specs=in_specs,
        out_specs=out_specs,
        compiler_params=_params(),
        name="proj_ab",
    )(*args)


def _proj_cd_kernel(*refs, rope, need_q):
    (x_ref, mod_ref, wt_ref, wuq_ref, wuk_ref, wuv_ref, g_qa_ref, g_kva_ref, g_head_ref) = refs[:9]
    refs = refs[9:]
    if rope:
        cos_ref, sin_ref = refs[:2]
        refs = refs[2:]
    if need_q:
        cqt_ref, dqt_ref = refs[:2]
        refs = refs[2:]
    ck_ref, cvt_ref, dk_ref, dvt_ref = refs

    h = _norm_modulate(x_ref[0], mod_ref[0, 3:4, :], mod_ref[0, 4:5, :])
    tm = h.shape[0]
    ht = h.T.astype(BF16)

    hd = HEAD_DIM
    n_cq = C_Q_RANK
    n_dq = D_HEADS * hd
    n_ckv = C_KV_RANK + C_ROPE
    g_qn = g_head_ref[0:64]
    g_qr = g_head_ref[64:96]
    g_kn = g_head_ref[96:160]
    g_kr = g_head_ref[160:192]
    g_dq = g_head_ref[192:256]
    g_dk = g_head_ref[256:320]
    if rope:
        cos = cos_ref[...]
        sin = sin_ref[...]
    zpad = jnp.zeros((C_PAD - C_NOPE - C_ROPE, tm), F32)

    def latent_queries(z):
        cqn = _rms_rows(z, g_qa_ref[...]).astype(BF16)
        qt = jnp.dot(wuq_ref[...], cqn, preferred_element_type=F32)
        cscale = (C_NOPE + C_ROPE) ** -0.5 * LOG2E
        for hh in range(C_HEADS):
            qn = _rms_rows(qt[hh * C_NOPE:(hh + 1) * C_NOPE], g_qn)
            rr = C_HEADS * C_NOPE + hh * C_ROPE
            qr = _rms_rows(qt[rr:rr + C_ROPE], g_qr)
            if rope:
                qr = _rope_rows(qr, cos, sin, C_ROPE // 4)
            qh = jnp.concatenate([qn, qr, zpad], axis=0) * cscale
            cqt_ref[0, hh * C_PAD:(hh + 1) * C_PAD, :] = qh.astype(BF16)

    def grid_queries(z):
        dscale = hd ** -0.5 * LOG2E
        for hh in range(D_HEADS):
            dqt_ref[0, hh * hd:(hh + 1) * hd, :] = (_rms_rows(z[hh * hd:(hh + 1) * hd], g_dq) * dscale).astype(BF16)

    def latent_keys_values(z):
        cn = _rms_rows(z[:C_KV_RANK], g_kva_ref[...]).astype(BF16)
        kr = _rms_rows(z[C_KV_RANK:], g_kr)
        if rope:
            kr = _rope_rows(kr, cos, sin, C_ROPE // 4)
        knt = jnp.dot(wuk_ref[...], cn, preferred_element_type=F32)
        for hh in range(C_HEADS):
            kn = _rms_rows(knt[hh * C_NOPE:(hh + 1) * C_NOPE], g_kn)
            _store_token_major(ck_ref, hh * C_PAD, [kn, kr, zpad])
        _store_values_t(cvt_ref, jnp.dot(wuv_ref[...], cn, preferred_element_type=F32), C_V)

    def grid_keys(z):
        for j in range(n_dq // LANES):
            rr = j * LANES
            _store_token_major(dk_ref, j * LANES, [_rms_rows(z[rr:rr + hd], g_dk),
                                                   _rms_rows(z[rr + hd:rr + 2 * hd], g_dk)])

    def grid_values(z):
        _store_values_t(dvt_ref, z, hd)

    groups = [(n_cq, latent_queries), (n_dq, grid_queries)] if need_q else []
    groups += [(n_ckv, latent_keys_values), (n_dq, grid_keys), (n_dq, grid_values)]
    _grouped_projection(wt_ref, ht, groups)


def _proj_cd(x, mods, mod_row, wt, wuq, wuk, wuv, g_qa, g_kva, g_head, rope_tabs, *, need_q):
    nb, t, d = x.shape
    tm = min(TOKEN_TILE, t)
    rope = rope_tabs is not None
    hd = HEAD_DIM
    n_c = C_HEADS * C_PAD
    n_d = D_HEADS * hd
    n_cv = C_HEADS * (C_V + ONES_ROWS)
    n_dv = D_HEADS * (hd + ONES_ROWS)
    tok = lambda b, i: (b, i, 0)
    ftok = lambda b, i: (b, 0, i)
    const = lambda b, i: (0, 0)
    consts = [wt, wuq, wuk, wuv, g_qa, g_kva, g_head]
    in_specs = [pl.BlockSpec((1, tm, d), tok),
                pl.BlockSpec((1, N_MOD, d), lambda b, i: (mod_row(b), 0, 0))]
    in_specs += [_resident(a.shape, const) for a in consts]
    args = [x, mods] + consts
    if rope:
        in_specs += [pl.BlockSpec((C_ROPE, tm), lambda b, i: (0, i))] * 2
        args += list(rope_tabs)
    out_shape, out_specs = [], []
    if need_q:
        out_shape += [jax.ShapeDtypeStruct((nb, n_c, t), BF16), jax.ShapeDtypeStruct((nb, n_d, t), BF16)]
        out_specs += [pl.BlockSpec((1, n_c, tm), ftok), pl.BlockSpec((1, n_d, tm), ftok)]
    out_shape += [jax.ShapeDtypeStruct((nb, t, n_c), BF16), jax.ShapeDtypeStruct((nb, n_cv, t), BF16),
                  jax.ShapeDtypeStruct((nb, t, n_d), BF16), jax.ShapeDtypeStruct((nb, n_dv, t), BF16)]
    out_specs += [pl.BlockSpec((1, tm, n_c), tok), pl.BlockSpec((1, n_cv, tm), ftok),
                  pl.BlockSpec((1, tm, n_d), tok), pl.BlockSpec((1, n_dv, tm), ftok)]
    return pl.pallas_call(
        functools.partial(_proj_cd_kernel, rope=rope, need_q=need_q),
        out_shape=out_shape,
        grid=(nb, t // tm),
        in_specs=in_specs,
        out_specs=out_specs,
        compiler_params=_params(),
        name="proj_cd",
    )(*args)


def _attend_units(n_units, n_chunks, rhs_of, score_chunk, value_chunk, s_ref, p_ref, extra=None):
    stats = {}
    outs = [None] * n_units
    tq = s_ref.shape[2]
    for t in range(n_units + 2):
        ua, ub, uc = t, t - 1, t - 2
        do_a, do_b, do_c = ua < n_units, 0 <= ub < n_units, 0 <= uc < n_units
        if do_a:
            rhs = rhs_of(ua)
            m8 = None
        acc = None
        for c in range(n_chunks):
            rows = slice(c * KEY_CHUNK, (c + 1) * KEY_CHUNK)
            if do_a:
                s = score_chunk(ua, c, rhs)
                s_ref[ua % 2, rows, :] = s
                pm = jnp.max(s.reshape(KEY_CHUNK // 8, 8, tq), axis=0)
                m8 = pm if m8 is None else jnp.maximum(m8, pm)
            if do_b:
                p_ref[ub % 2, rows, :] = jnp.exp2(s_ref[ub % 2, rows, :] - stats[ub]).astype(BF16)
            if do_c:
                part = jnp.dot(value_chunk(uc, c), p_ref[uc % 2, rows, :], preferred_element_type=F32)
                acc = part if acc is None else acc + part
        if do_a:
            m = jnp.max(m8, axis=0, keepdims=True)
            stats[ua] = m if extra is None else jnp.maximum(m, extra(ua))
        if do_c:
            dv = acc.shape[0] - ONES_ROWS
            total = acc[dv:dv + 1]
            if extra is not None:
                total = total + jnp.exp2(extra(uc) - stats[uc])
            outs[uc] = acc[:dv] * (1.0 / total)
    return outs


def _half_rhs(q_head, upper):
    zeros = jnp.zeros_like(q_head)
    return jnp.concatenate([zeros, q_head] if upper else [q_head, zeros], axis=0)


def _store_outputs(o_ref, outs):
    rows = jnp.concatenate(outs, axis=0)
    for j in range(rows.shape[0] // LANES):
        o_ref[0, :, j * LANES:(j + 1) * LANES] = rows[j * LANES:(j + 1) * LANES].T.astype(BF16)


def _attn_scratch(n_keys, tq):
    return [pltpu.VMEM((2, n_keys, tq), F32), pltpu.VMEM((2, n_keys, tq), BF16)]


def _attn_a_kernel(*refs, latent, seq):
    if latent:
        (sink_ref, qt_ref, kp_ref, kc_ref, kn_ref, kx_ref, vp_ref, vc_ref, vn_ref, vx_ref,
         o_ref, s_ref, p_ref) = refs
    else:
        sink_ref, qt_ref, kx_ref, vx_ref, o_ref, s_ref, p_ref = refs
    qt = qt_ref[0]
    tq = qt.shape[1]
    hd = HEAD_DIM
    if latent:
        kwin = jnp.concatenate([kp_ref[0], kc_ref[0], kn_ref[0], kx_ref[0]], axis=0)
        vwin = jnp.concatenate([vp_ref[0], vc_ref[0], vn_ref[0], vx_ref[0]], axis=1)
        n_lat = tq + 2 * A_WINDOW
        nk = kwin.shape[0]
        s0 = pl.program_id(1) * tq
        krow = lax.broadcasted_iota(jnp.int32, (nk, tq), 0)
        qcol = lax.broadcasted_iota(jnp.int32, (nk, tq), 1)
        kpos = krow + (s0 - A_WINDOW)
        inside = jnp.where(jnp.abs(krow - A_WINDOW - qcol) <= A_WINDOW,
                           jnp.where(kpos >= 0, jnp.where(kpos < seq, 0.0, NEG), NEG), NEG)
        bias = jnp.where(krow >= n_lat, 0.0, inside)
    else:
        kwin = kx_ref[0]
        vwin = vx_ref[0]
    group = A_HEADS // A_KV_HEADS

    def rhs_of(u):
        return _half_rhs(qt[u * hd:(u + 1) * hd], u // group == 1)

    def score_chunk(u, c, rhs):
        rows = slice(c * KEY_CHUNK, (c + 1) * KEY_CHUNK)
        s = jnp.dot(kwin[rows], rhs, preferred_element_type=F32)
        return s + bias[rows] if latent else s

    def value_chunk(u, c):
        g = u // group
        return vwin[g * (hd + ONES_ROWS):(g + 1) * (hd + ONES_ROWS), c * KEY_CHUNK:(c + 1) * KEY_CHUNK]

    outs = _attend_units(A_HEADS, kwin.shape[0] // KEY_CHUNK, rhs_of, score_chunk, value_chunk, s_ref, p_ref,
                         extra=lambda u: sink_ref[u] * LOG2E)
    _store_outputs(o_ref, outs)


def _attn_a(sink, qt, k, vt, kx, vxt, *, latent):
    nb, nq, t = qt.shape
    n_kv = A_KV_HEADS * HEAD_DIM
    n_v = vxt.shape[1]
    lx = kx.shape[1]
    smem = pl.BlockSpec(memory_space=pltpu.SMEM)
    if latent:
        tq = ATTN_Q_TILE
        seq = k.shape[1]
        w = A_WINDOW
        r = tq // w
        last = seq // w - 1
        prev = lambda i: jnp.maximum(i * r - 1, 0)
        nxt = lambda i: jnp.minimum((i + 1) * r, last)
        in_specs = [
            smem,
            pl.BlockSpec((1, nq, tq), lambda b, i: (b, 0, i)),
            pl.BlockSpec((1, w, n_kv), lambda b, i: (b, prev(i), 0)),
            pl.BlockSpec((1, tq, n_kv), lambda b, i: (b, i, 0)),
            pl.BlockSpec((1, w, n_kv), lambda b, i: (b, nxt(i), 0)),
            pl.BlockSpec((1, lx, n_kv), lambda b, i: (b, 0, 0)),
            pl.BlockSpec((1, n_v, w), lambda b, i: (b, 0, prev(i))),
            pl.BlockSpec((1, n_v, tq), lambda b, i: (b, 0, i)),
            pl.BlockSpec((1, n_v, w), lambda b, i: (b, 0, nxt(i))),
            pl.BlockSpec((1, n_v, lx), lambda b, i: (b, 0, 0)),
        ]
        args = [sink, qt, k, k, k, kx, vt, vt, vt, vxt]
        n_keys = tq + 2 * w + lx
    else:
        tq = t
        seq = 0
        in_specs = [
            smem,
            pl.BlockSpec((1, nq, tq), lambda b, i: (b, 0, i)),
            pl.BlockSpec((1, lx, n_kv), lambda b, i: (b, 0, 0)),
            pl.BlockSpec((1, n_v, lx), lambda b, i: (b, 0, 0)),
        ]
        args = [sink, qt, kx, vxt]
        n_keys = lx
    return pl.pallas_call(
        functools.partial(_attn_a_kernel, latent=latent, seq=seq),
        out_shape=jax.ShapeDtypeStruct((nb, t, nq), BF16),
        grid=(nb, t // tq),
        in_specs=in_specs,
        out_specs=pl.BlockSpec((1, tq, nq), lambda b, i: (b, i, 0)),
        scratch_shapes=_attn_scratch(n_keys, tq),
        compiler_params=_params(),
        name="attn_window",
    )(*args)


def _attn_b_kernel(*refs, latent, lam_init):
    if latent:
        lam_ref, gain_ref, qt_ref, k_ref, kx_ref, vt_ref, vxt_ref, o_ref, s_ref, p_ref = refs
    else:
        lam_ref, gain_ref, qt_ref, kx_ref, vxt_ref, o_ref, s_ref, p_ref = refs
    qt = qt_ref[0]
    hd = HEAD_DIM
    lv = lam_ref[...]
    lam = (jnp.exp(jnp.sum(lv[0:1] * lv[1:2], axis=-1, keepdims=True))
           - jnp.exp(jnp.sum(lv[2:3] * lv[3:4], axis=-1, keepdims=True)) + lam_init)
    gain = gain_ref[...] * (1.0 - lam_init)

    n_ctx = kx_ref.shape[1] // KEY_CHUNK

    def rhs_of(u):
        return _half_rhs(qt[u * hd:(u + 1) * hd], u % 2 == 1)

    def score_chunk(u, c, rhs):
        cols = slice((u // 2) * 2 * hd, (u // 2 + 1) * 2 * hd)
        if c < n_ctx:
            k = kx_ref[0, c * KEY_CHUNK:(c + 1) * KEY_CHUNK, cols]
        else:
            k = k_ref[0, (c - n_ctx) * KEY_CHUNK:(c - n_ctx + 1) * KEY_CHUNK, cols]
        return jnp.dot(k, rhs, preferred_element_type=F32)

    def value_chunk(u, c):
        stride = B_V_DIM + ONES_ROWS
        rows = slice((u // 2) * stride, (u // 2 + 1) * stride)
        if c < n_ctx:
            return vxt_ref[0, rows, c * KEY_CHUNK:(c + 1) * KEY_CHUNK]
        return vt_ref[0, rows, (c - n_ctx) * KEY_CHUNK:(c - n_ctx + 1) * KEY_CHUNK]

    outs = _attend_units(2 * B_HEADS, s_ref.shape[1] // KEY_CHUNK, rhs_of, score_chunk, value_chunk, s_ref, p_ref)
    heads = []
    for hh in range(B_HEADS):
        o = outs[2 * hh] - lam * outs[2 * hh + 1]
        heads.append(_rms_rows(o, gain))
    _store_outputs(o_ref, heads)


def _attn_b(lam_vecs, sub_gain, qt, k, vt, kx, vxt, *, latent, lam_init):
    nb, nq, t = qt.shape
    lx, nk = kx.shape[1], kx.shape[2]
    nv = vxt.shape[1]
    tq = min(FULL_Q_TILE, t)
    const = lambda b, i: (0, 0)
    whole = lambda b, i: (b, 0, 0)
    in_specs = [pl.BlockSpec(lam_vecs.shape, const), pl.BlockSpec(sub_gain.shape, const),
                pl.BlockSpec((1, nq, tq), lambda b, i: (b, 0, i))]
    args = [lam_vecs, sub_gain, qt]
    n_keys = lx
    if latent:
        seq = k.shape[1]
        in_specs += [pl.BlockSpec((1, seq, nk), whole), pl.BlockSpec((1, lx, nk), whole),
                     pl.BlockSpec((1, nv, seq), whole), pl.BlockSpec((1, nv, lx), whole)]
        args += [k, kx, vt, vxt]
        n_keys += seq
    else:
        in_specs += [pl.BlockSpec((1, lx, nk), whole), pl.BlockSpec((1, nv, lx), whole)]
        args += [kx, vxt]
    return pl.pallas_call(
        functools.partial(_attn_b_kernel, latent=latent, lam_init=lam_init),
        out_shape=jax.ShapeDtypeStruct((nb, t, B_HEADS * B_V_DIM), BF16),
        grid=(nb, t // tq),
        in_specs=in_specs,
        out_specs=pl.BlockSpec((1, tq, B_HEADS * B_V_DIM), lambda b, i: (b, i, 0)),
        scratch_shapes=_attn_scratch(n_keys, tq),
        compiler_params=_params(),
        name="attn_diff",
    )(*args)


def _attn_c_kernel(qt_ref, k_ref, kx_ref, vt_ref, vxt_ref, o_ref, s_ref, p_ref):
    qt = qt_ref[0]

    n_ctx = kx_ref.shape[1] // KEY_CHUNK

    def rhs_of(u):
        return qt[u * C_PAD:(u + 1) * C_PAD]

    def score_chunk(u, c, rhs):
        cols = slice(u * C_PAD, (u + 1) * C_PAD)
        if c < n_ctx:
            k = kx_ref[0, c * KEY_CHUNK:(c + 1) * KEY_CHUNK, cols]
        else:
            k = k_ref[0, (c - n_ctx) * KEY_CHUNK:(c - n_ctx + 1) * KEY_CHUNK, cols]
        return jnp.dot(k, rhs, preferred_element_type=F32)

    def value_chunk(u, c):
        rows = slice(u * (C_V + ONES_ROWS), (u + 1) * (C_V + ONES_ROWS))
        if c < n_ctx:
            return vxt_ref[0, rows, c * KEY_CHUNK:(c + 1) * KEY_CHUNK]
        return vt_ref[0, rows, (c - n_ctx) * KEY_CHUNK:(c - n_ctx + 1) * KEY_CHUNK]

    _store_outputs(o_ref, _attend_units(C_HEADS, s_ref.shape[1] // KEY_CHUNK, rhs_of, score_chunk, value_chunk,
                                        s_ref, p_ref))


def _attn_c(qt, k, vt, kx, vxt):
    nb, nq, t = qt.shape
    lx = kx.shape[1]
    nv = vt.shape[1]
    tq = FULL_Q_TILE
    whole = lambda b, i: (b, 0, 0)
    return pl.pallas_call(
        _attn_c_kernel,
        out_shape=jax.ShapeDtypeStruct((nb, t, C_HEADS * C_V), BF16),
        grid=(nb, t // tq),
        in_specs=[pl.BlockSpec((1, nq, tq), lambda b, i: (b, 0, i)),
                  pl.BlockSpec((1, t, nq), whole),
                  pl.BlockSpec((1, lx, nq), whole),
                  pl.BlockSpec((1, nv, t), whole),
                  pl.BlockSpec((1, nv, lx), whole)],
        out_specs=pl.BlockSpec((1, tq, C_HEADS * C_V), lambda b, i: (b, i, 0)),
        scratch_shapes=_attn_scratch(t + lx, tq),
        compiler_params=_params(),
        name="attn_latent",
    )(qt, k, kx, vt, vxt)


def _rpb_table_kernel(rpb_ref, o_ref, *, n_dr, n_dc):
    hh = pl.program_id(0)
    shape = (GRID_W, LANES)
    kc = lax.broadcasted_iota(jnp.int32, shape, 0)
    lane = lax.broadcasted_iota(jnp.int32, shape, 1)
    right = lane >= GRID_W
    qc = jnp.where(right, lane - GRID_W, lane)
    dc = jnp.clip(kc - qc, -(D_WIN_COLS - 1), D_WIN_COLS - 1) + (D_WIN_COLS - 1)
    cs = jnp.clip(qc - D_WIN_COLS // 2, 0, GRID_W - D_WIN_COLS)
    in_cols = jnp.where(kc >= cs, jnp.where(kc < cs + D_WIN_COLS, 0.0, NEG), NEG)
    for dd in range(n_dr + 1):
        base_l = (hh * n_dr + min(dd, n_dr - 1)) * n_dc
        base_r = (hh * n_dr + max(dd - 1, 0)) * n_dc
        acc = jnp.zeros(shape, F32)
        for c in range(n_dc):
            val = jnp.where(right, rpb_ref[base_r + c], rpb_ref[base_l + c])
            acc = jnp.where(dc == c, val, acc)
        ok = jnp.where(right, 0.0 if dd >= 1 else NEG, 0.0 if dd <= n_dr - 1 else NEG)
        o_ref[0, dd] = acc * LOG2E + ok + in_cols


def _rpb_table(rpb):
    n_h, n_dr, n_dc = rpb.shape
    return pl.pallas_call(
        functools.partial(_rpb_table_kernel, n_dr=n_dr, n_dc=n_dc),
        out_shape=jax.ShapeDtypeStruct((n_h, n_dr + 1, GRID_W, LANES), F32),
        grid=(n_h,),
        in_specs=[pl.BlockSpec(memory_space=pltpu.SMEM)],
        out_specs=pl.BlockSpec((1, n_dr + 1, GRID_W, LANES), lambda h: (h, 0, 0, 0)),
        name="rpb_table",
    )(rpb.reshape(-1))


def _attn_d_kernel(tab_ref, qt_ref, k_ref, kx_ref, vt_ref, vxt_ref, o_ref, s_ref, p_ref, *, rows):
    qt = qt_ref[0]
    hd = HEAD_DIM
    kh = min(D_WIN_ROWS, rows)
    r_first = pl.program_id(1) * NAT_Q_ROWS
    ws = jnp.clip(r_first - kh // 2, 0, rows - kh)
    wl = jnp.minimum(ws, rows - NAT_WIN_ROWS)
    nwin = NAT_WIN_ROWS * GRID_W
    tok0 = pl.multiple_of(wl * GRID_W, LANES)
    kwin = k_ref[0, pl.ds(tok0, nwin), :]
    vwin = vt_ref[0, :, pl.ds(tok0, nwin)]
    lane = lax.broadcasted_iota(jnp.int32, (GRID_W, LANES), 1)
    right = lane >= GRID_W
    n_dd = tab_ref.shape[1]
    q_pairs = NAT_Q_ROWS // 2

    def in_window(kr, qr):
        rs = jnp.clip(qr - kh // 2, 0, rows - kh)
        return jnp.where((kr >= rs) & (kr < rs + kh), 0.0, NEG)

    row_add = jnp.concatenate([
        jnp.concatenate([jnp.where(right, in_window(wl + j, r_first + 2 * a + 1),
                                   in_window(wl + j, r_first + 2 * a))
                         for a in range(q_pairs)], axis=1)
        for j in range(NAT_WIN_ROWS)], axis=0)

    rows_per_chunk = KEY_CHUNK // GRID_W
    n_win = nwin // KEY_CHUNK

    def rhs_of(u):
        return _half_rhs(qt[u * hd:(u + 1) * hd], u % 2 == 1)

    def score_chunk(u, c, rhs):
        cols = slice((u // 2) * LANES, (u // 2 + 1) * LANES)
        if c >= n_win:
            return jnp.dot(kx_ref[0, (c - n_win) * KEY_CHUNK:(c - n_win + 1) * KEY_CHUNK, cols], rhs,
                           preferred_element_type=F32)
        rows_c = slice(c * KEY_CHUNK, (c + 1) * KEY_CHUNK)
        bias = jnp.concatenate([
            jnp.concatenate([tab_ref[u, jnp.clip((wl + j) - (r_first + 2 * a) + (D_WIN_ROWS - 1), 0, n_dd - 1)]
                             for a in range(q_pairs)], axis=1)
            for j in range(c * rows_per_chunk, (c + 1) * rows_per_chunk)], axis=0)
        return jnp.dot(kwin[rows_c, cols], rhs, preferred_element_type=F32) + (bias + row_add[rows_c])

    def value_chunk(u, c):
        rows_u = slice(u * (hd + ONES_ROWS), (u + 1) * (hd + ONES_ROWS))
        if c >= n_win:
            return vxt_ref[0, rows_u, (c - n_win) * KEY_CHUNK:(c - n_win + 1) * KEY_CHUNK]
        return vwin[rows_u, c * KEY_CHUNK:(c + 1) * KEY_CHUNK]

    _store_outputs(o_ref, _attend_units(D_HEADS, s_ref.shape[1] // KEY_CHUNK, rhs_of, score_chunk, value_chunk,
                                        s_ref, p_ref))


def _attn_d(table, qt, k, vt, kx, vxt):
    nb, nq, t = qt.shape
    lx = kx.shape[1]
    tq = NAT_Q_ROWS * GRID_W
    rows = t // GRID_W
    assert rows >= NAT_WIN_ROWS and rows % NAT_Q_ROWS == 0 and NAT_Q_ROWS % 2 == 0
    whole = lambda b, i: (b, 0, 0)
    return pl.pallas_call(
        functools.partial(_attn_d_kernel, rows=rows),
        out_shape=jax.ShapeDtypeStruct((nb, t, nq), BF16),
        grid=(nb, t // tq),
        in_specs=[_resident(table.shape, lambda b, i: (0, 0, 0, 0)),
                  pl.BlockSpec((1, nq, tq), lambda b, i: (b, 0, i)),
                  pl.BlockSpec((1, t, nq), whole),
                  pl.BlockSpec((1, lx, nq), whole),
                  pl.BlockSpec((1, vt.shape[1], t), whole),
                  pl.BlockSpec((1, vt.shape[1], lx), whole)],
        out_specs=pl.BlockSpec((1, tq, nq), lambda b, i: (b, i, 0)),
        scratch_shapes=_attn_scratch(NAT_WIN_ROWS * GRID_W + lx, tq),
        compiler_params=_params(),
        name="attn_neighbourhood",
    )(table, qt, k, kx, vt, vxt)


def _rope_tables_t(n, rot_dim):
    nf = rot_dim // 4
    inv = ROPE_BASE ** (-jnp.arange(nf, dtype=F32) / nf)
    t = jnp.arange(n)
    row = (t // GRID_W).astype(F32)
    col = (t % GRID_W).astype(F32)
    ang = jnp.concatenate([row[:, None] * inv, col[:, None] * inv], axis=-1)
    cos, sin = jnp.cos(ang).T, jnp.sin(ang).T
    cos_e = jnp.concatenate([cos[:nf], cos[:nf], cos[nf:], cos[nf:]], axis=0)
    sin_e = jnp.concatenate([-sin[:nf], sin[:nf], -sin[nf:], sin[nf:]], axis=0)
    return cos_e, sin_e


def _col(v):
    return v.astype(F32).reshape(-1, 1)


def kernel(x, c, ctx, c_ctx, w_mod, b_mod,
           ffn1_w_gate, ffn1_w_up, ffn1_w_down, ffn2_w_gate, ffn2_w_up, ffn2_w_down,
           ab_w_in, ab_w_out, a_q_norm, a_k_norm, a_sink, b_q_norm, b_k_norm,
           b_lambda_q1, b_lambda_k1, b_lambda_q2, b_lambda_k2, b_sub_norm,
           cd_w_in, cd_w_out, c_q_a_norm, c_kv_a_norm, c_w_uq, c_w_ukv,
           c_q_nope_norm, c_q_rope_norm, c_k_nope_norm, c_k_rope_norm,
           d_q_norm, d_k_norm, d_rpb):
    nb, seq, d = x.shape
    lctx = ctx.shape[1]
    depth = w_mod.shape[0]
    hd = HEAD_DIM

    n_rows = -(-(nb + 1) // 8) * 8
    c_rows = jnp.concatenate([c, c_ctx[None, :], jnp.zeros((n_rows - nb - 1, d), F32)], axis=0)
    mods_all = _mod_vectors(c_rows, w_mod, b_mod).reshape(depth, n_rows, N_MOD, d)
    x_row = lambda b: b
    ctx_row = lambda b: nb

    rope_head = _rope_tables_t(seq, hd)
    rope_mla = _rope_tables_t(seq, C_ROPE)

    xc = ctx.reshape(1, nb * lctx, d)
    for l in range(depth):
        need_ctx = l < depth - 1
        mods = mods_all[l]
        w1 = (ffn1_w_gate[l].astype(BF16), ffn1_w_up[l].astype(BF16), ffn1_w_down[l].astype(BF16))
        w2 = (ffn2_w_gate[l].astype(BF16), ffn2_w_up[l].astype(BF16), ffn2_w_down[l].astype(BF16))
        x = _ffn(x, mods, x_row, *w1, j0=0)
        xc = _ffn(xc, mods, ctx_row, *w1, j0=0)
        xc_b = xc.reshape(nb, lctx, d)
        i = l // 2
        if l % 2 == 0:
            lam_init = 0.8 - 0.6 * math.exp(-0.3 * l)
            w_in = ab_w_in[i]
            n_q = A_HEADS * hd + B_HEADS * 2 * hd
            n_ak = A_KV_HEADS * hd
            n_bk = B_HEADS * 2 * hd
            o_av = n_q + n_ak
            o_bk = o_av + n_ak
            o_bv = o_bk + n_bk
            wt = jnp.concatenate([w_in[:, :o_av], w_in[:, o_bk:o_bv], w_in[:, o_av:o_bk], w_in[:, o_bv:]],
                                 axis=1).T.astype(BF16)
            gains = jnp.concatenate([_col(a_q_norm[i]), _col(b_q_norm[i]),
                                     _col(a_k_norm[i]), _col(b_k_norm[i])], axis=0)
            aqt, bqt, ak, bk, avt, bvt = _proj_ab(x, mods, x_row, wt, gains, rope_head, need_q=True)
            ctx_out = _proj_ab(xc_b, mods, ctx_row, wt if need_ctx else wt[n_q:], gains, None, need_q=need_ctx)
            akx, bkx, avxt, bvxt = ctx_out[-4:]
            sink = a_sink[i].astype(F32)
            lam_vecs = jnp.stack([b_lambda_q1[i], b_lambda_k1[i], b_lambda_q2[i], b_lambda_k2[i]]).astype(F32)
            sub_gain = _col(b_sub_norm[i])
            y1 = _attn_a(sink, aqt, ak, avt, akx, avxt, latent=True)
            y2 = _attn_b(lam_vecs, sub_gain, bqt, bk, bvt, bkx, bvxt, latent=True, lam_init=lam_init)
            w_out = ab_w_out[i].astype(BF16)
            if need_ctx:
                aqxt, bqxt = ctx_out[:2]
                y1x = _attn_a(sink, aqxt, None, None, akx, avxt, latent=False)
                y2x = _attn_b(lam_vecs, sub_gain, bqxt, None, None, bkx, bvxt, latent=False, lam_init=lam_init)
        else:
            assert not need_ctx, "context queries of an odd layer are only needed for depth > 2"
            w_in = cd_w_in[i]
            n_q = C_Q_RANK + D_HEADS * hd
            wt = w_in.T.astype(BF16)
            uq = c_w_uq[i].reshape(C_Q_RANK, C_HEADS, C_NOPE + C_ROPE)
            wuq = jnp.concatenate([uq[:, :, :C_NOPE].reshape(C_Q_RANK, -1),
                                   uq[:, :, C_NOPE:].reshape(C_Q_RANK, -1)], axis=1).T.astype(BF16)
            ukv = c_w_ukv[i].reshape(C_KV_RANK, C_HEADS, C_NOPE + C_V)
            wuk = ukv[:, :, :C_NOPE].reshape(C_KV_RANK, -1).T.astype(BF16)
            wuv = ukv[:, :, C_NOPE:].reshape(C_KV_RANK, -1).T.astype(BF16)
            g_qa = _col(c_q_a_norm[i])
            g_kva = _col(c_kv_a_norm[i])
            g_head = jnp.concatenate([_col(c_q_nope_norm[i]), _col(c_q_rope_norm[i]),
                                      _col(c_k_nope_norm[i]), _col(c_k_rope_norm[i]),
                                      _col(d_q_norm[i]), _col(d_k_norm[i])], axis=0)
            cqt, dqt, ck, cvt, dk, dvt = _proj_cd(x, mods, x_row, wt, wuq, wuk, wuv,
                                                  g_qa, g_kva, g_head, rope_mla, need_q=True)
            ckx, cvxt, dkx, dvxt = _proj_cd(xc_b, mods, ctx_row, wt[n_q:], wuq, wuk, wuv,
                                            g_qa, g_kva, g_head, None, need_q=False)
            y1 = _attn_c(cqt, ck, cvt, ckx, cvxt)
            y2 = _attn_d(_rpb_table(d_rpb[i].astype(F32)), dqt, dk, dvt, dkx, dvxt)
            w_out = cd_w_out[i].astype(BF16)
        x = _ffn(x, mods, x_row, *w2, j0=6, y=(y1, y2), w_out=w_out)
        if need_ctx:
            half = y1x.shape[-1]
            xc = _ffn(xc, mods, ctx_row, *w2, j0=6,
                      y=(y1x.reshape(1, nb * lctx, half), y2x.reshape(1, nb * lctx, half)), w_out=w_out)
    return x
```

```python
import functools
import math

import jax
import jax.numpy as jnp
from jax import lax
from jax.experimental import pallas as pl
from jax.experimental.pallas import tpu as pltpu

F32 = jnp.float32
BF16 = jnp.bfloat16

D_MODEL = 1024
GRID_W = 64
HEAD_DIM = 64
D_FF = 2816
N_MOD = 9
ROPE_BASE = 10000.0
EPS = 1e-6
NEG = -1e30
LOG2E = math.log2(math.e)
A_HEADS = 8
A_KV_HEADS = 2
A_WINDOW = 128
B_HEADS = 4
B_V_DIM = 2 * HEAD_DIM
C_HEADS = 8
C_Q_RANK = 768
C_KV_RANK = 256
C_NOPE = 64
C_ROPE = 32
C_V = 64
C_PAD = 128
D_HEADS = 8
D_WIN_ROWS = 8
D_WIN_COLS = 16

LANES = 128
VMEM_LIMIT_BYTES = 56 * 1024 * 1024

CAST_ROWS = 256
FFN_CHUNK = 256
TOKEN_TILE = 512
FFN_TOKEN_TILE = 1024
ATTN_Q_TILE = 256
LOCAL_TILES = 2
FULL_Q_TILE = 512
KEY_CHUNK = 256
ONES_ROWS = 16
NAT_Q_ROWS = ATTN_Q_TILE // GRID_W
NAT_WIN_ROWS = 12


def _params(**flags):
    return pltpu.CompilerParams(vmem_limit_bytes=VMEM_LIMIT_BYTES, flags=flags or None)


def _resident(block_shape, index_map):
    return pl.BlockSpec(block_shape, index_map, pipeline_mode=pl.Buffered(1))


def _sigmoid(x):
    return 1.0 / (1.0 + jnp.exp(-x))


def _rms_rows(z, gain):
    ms = jnp.mean(z * z, axis=0, keepdims=True)
    return z * lax.rsqrt(ms + EPS) * gain


def _norm_modulate(x, shift, scale):
    r = lax.rsqrt(jnp.mean(x * x, axis=-1, keepdims=True) + EPS)
    return (x * r) * (1.0 + scale) + shift


def _rope_rows(y, cos, sin, nf):
    part = jnp.concatenate([y[nf:2 * nf], y[0:nf], y[3 * nf:4 * nf], y[2 * nf:3 * nf]], axis=0)
    return y * cos + part * sin


def _mod_kernel(c_ref, w_ref, b_ref, o_ref):
    c = c_ref[...]
    a = (c * _sigmoid(c)).astype(BF16)
    o_ref[0] = jnp.dot(a, w_ref[0].astype(BF16), preferred_element_type=F32) + b_ref[0]


def _mod_vectors(c_rows, w_mod, b_mod):
    depth, d, n = w_mod.shape
    rows = c_rows.shape[0]
    tn = 1152
    return pl.pallas_call(
        _mod_kernel,
        out_shape=jax.ShapeDtypeStruct((depth, rows, n), F32),
        grid=(depth, n // tn),
        in_specs=[
            pl.BlockSpec((rows, d), lambda l, j: (0, 0)),
            pl.BlockSpec((1, d, tn), lambda l, j: (l, 0, j)),
            pl.BlockSpec((1, 1, tn), lambda l, j: (l, 0, j)),
        ],
        out_specs=pl.BlockSpec((1, rows, tn), lambda l, j: (l, 0, j)),
        compiler_params=_params(),
        name="mod_vectors",
    )(c_rows, w_mod, b_mod.reshape(depth, 1, n))


def _cast_kernel(w_ref, o_ref):
    o_ref[...] = w_ref[...].astype(BF16)


def _cast_bf16(w):
    depth, rows, cols = w.shape
    tr = min(CAST_ROWS, rows)
    spec = pl.BlockSpec((1, tr, cols), lambda l, i: (l, i, 0))
    return pl.pallas_call(
        _cast_kernel,
        out_shape=jax.ShapeDtypeStruct(w.shape, BF16),
        grid=(depth, rows // tr),
        in_specs=[spec],
        out_specs=spec,
        name="cast_bf16",
    )(w)


def _ffn_kernel(*refs, j0, fuse_out):
    if fuse_out:
        x_ref, y1_ref, y2_ref, mod_ref, wo_ref, wg_ref, wu_ref, wd_ref, o_ref, acc_ref = refs
    else:
        x_ref, mod_ref, wg_ref, wu_ref, wd_ref, o_ref, acc_ref = refs
    x = x_ref[0]
    if fuse_out:
        half = y1_ref.shape[-1]
        y = (jnp.dot(y1_ref[0], wo_ref[0, 0:half, :], preferred_element_type=F32)
             + jnp.dot(y2_ref[0], wo_ref[0, half:, :], preferred_element_type=F32))
        x = x + mod_ref[0, 5:6, :] * y
    shift = mod_ref[0, j0:j0 + 1, :]
    scale = mod_ref[0, j0 + 1:j0 + 2, :]
    gate = mod_ref[0, j0 + 2:j0 + 3, :]
    h = _norm_modulate(x, shift, scale).astype(BF16)
    d_ff = wg_ref.shape[2]
    for c in range(d_ff // FFN_CHUNK):
        lo, hi = c * FFN_CHUNK, (c + 1) * FFN_CHUNK
        g = jnp.dot(h, wg_ref[0, :, lo:hi], preferred_element_type=F32)
        u = jnp.dot(h, wu_ref[0, :, lo:hi], preferred_element_type=F32)
        a = (g * _sigmoid(g) * u).astype(BF16)
        part = jnp.dot(a, wd_ref[0, lo:hi, :], preferred_element_type=F32)
        if c == 0:
            acc_ref[...] = part
        else:
            acc_ref[...] += part
    o_ref[0] = x + (0.5 * gate) * acc_ref[...]


def _ffn(x, mods, mod_row, wg, wu, wd, layer, *, j0, y=None, w_out=None, out_layer=0):
    nb, t, d = x.shape
    tm = min(FFN_TOKEN_TILE, t)
    d_ff = wg.shape[2]
    fuse_out = y is not None
    tok = lambda b, i: (b, i, 0)
    this_layer = lambda b, i: (layer, 0, 0)
    in_specs = [pl.BlockSpec((1, tm, d), tok)]
    args = [x]
    if fuse_out:
        half = y[0].shape[-1]
        in_specs += [pl.BlockSpec((1, tm, half), tok), pl.BlockSpec((1, tm, half), tok)]
        args += [y[0], y[1]]
    in_specs.append(pl.BlockSpec((1, N_MOD, d), lambda b, i: (mod_row(b), 0, 0)))
    args.append(mods)
    if fuse_out:
        in_specs.append(_resident((1,) + w_out.shape[1:], lambda b, i: (out_layer, 0, 0)))
        args.append(w_out)
    in_specs += [_resident((1, d, d_ff), this_layer), _resident((1, d, d_ff), this_layer),
                 _resident((1, d_ff, d), this_layer)]
    args += [wg, wu, wd]
    return pl.pallas_call(
        functools.partial(_ffn_kernel, j0=j0, fuse_out=fuse_out),
        out_shape=jax.ShapeDtypeStruct(x.shape, F32),
        grid=(nb, t // tm),
        in_specs=in_specs,
        out_specs=pl.BlockSpec((1, tm, d), tok),
        scratch_shapes=[pltpu.VMEM((tm, d), F32)],
        compiler_params=_params(),
        name="ffn_out" if fuse_out else "ffn",
    )(*args)


def _store_token_major(ref, col0, blocks):
    width = sum(b.shape[0] for b in blocks)
    ref[0, :, col0:col0 + width] = jnp.concatenate(blocks, axis=0).T.astype(BF16)


def _store_values_t(ref, zrows, dv):
    ones = jnp.ones((ONES_ROWS, zrows.shape[1]), F32)
    stride = dv + ONES_ROWS
    for hh in range(zrows.shape[0] // dv):
        block = jnp.concatenate([zrows[hh * dv:(hh + 1) * dv], ones], axis=0)
        ref[0, hh * stride:(hh + 1) * stride, :] = block.astype(BF16)


def _grouped_projection(wt_ref, ht, groups):
    r0, pending = 0, None
    for n, epilogue in groups:
        z = jnp.dot(wt_ref[r0:r0 + n, :], ht, preferred_element_type=F32)
        if pending is not None:
            pending[0](pending[1])
        pending = (epilogue, z)
        r0 += n
    pending[0](pending[1])


def _proj_ab_kernel(*refs, rope, need_q):
    x_ref, mod_ref, wt_ref, gain_ref = refs[:4]
    refs = refs[4:]
    if rope:
        cos_ref, sin_ref = refs[:2]
        refs = refs[2:]
    if need_q:
        aqt_ref, bqt_ref = refs[:2]
        refs = refs[2:]
    ak_ref, bk_ref, avt_ref, bvt_ref = refs

    h = _norm_modulate(x_ref[0], mod_ref[0, 3:4, :], mod_ref[0, 4:5, :])
    ht = h.T.astype(BF16)
    hd = HEAD_DIM
    g_aq = gain_ref[0 * hd:1 * hd]
    g_bq = gain_ref[1 * hd:2 * hd]
    g_ak = gain_ref[2 * hd:3 * hd]
    g_bk = gain_ref[3 * hd:4 * hd]
    if rope:
        cos = cos_ref[...]
        sin = sin_ref[...]

    def head(z, row0, gain):
        y = _rms_rows(z[row0:row0 + hd], gain)
        if rope:
            y = _rope_rows(y, cos, sin, hd // 4)
        return y

    n_aq = A_HEADS * hd
    n_bq = B_HEADS * 2 * hd
    n_ak = A_KV_HEADS * hd
    qscale = hd ** -0.5 * LOG2E

    def queries(ref, gain):
        def epilogue(z):
            for j in range(z.shape[0] // hd):
                ref[0, j * hd:(j + 1) * hd, :] = (head(z, j * hd, gain) * qscale).astype(BF16)
        return epilogue

    def keys(z):
        _store_token_major(ak_ref, 0, [head(z, j * hd, g_ak) for j in range(A_KV_HEADS)])
        for j in range(n_bq // LANES):
            r0 = n_ak + j * LANES
            _store_token_major(bk_ref, j * LANES, [head(z, r0, g_bk), head(z, r0 + hd, g_bk)])

    def values(z):
        _store_values_t(avt_ref, z[:n_ak], hd)
        _store_values_t(bvt_ref, z[n_ak:], B_V_DIM)

    groups = [(n_aq, queries(aqt_ref, g_aq)), (n_bq, queries(bqt_ref, g_bq))] if need_q else []
    groups += [(n_ak + n_bq, keys), (wt_ref.shape[0] - sum(n for n, _ in groups) - n_ak - n_bq, values)]
    _grouped_projection(wt_ref, ht, groups)


def _proj_ab(x, mods, mod_row, wt, gains, rope_tabs, *, need_q):
    nb, t, d = x.shape
    tm = min(TOKEN_TILE, t)
    rope = rope_tabs is not None
    hd = HEAD_DIM
    n_aq, n_bq, n_ak = A_HEADS * hd, B_HEADS * 2 * hd, A_KV_HEADS * hd
    n_av, n_bv = A_KV_HEADS * (hd + ONES_ROWS), B_HEADS * (B_V_DIM + ONES_ROWS)
    tok = lambda b, i: (b, i, 0)
    ftok = lambda b, i: (b, 0, i)
    const = lambda b, i: (0, 0)
    in_specs = [
        pl.BlockSpec((1, tm, d), tok),
        pl.BlockSpec((1, N_MOD, d), lambda b, i: (mod_row(b), 0, 0)),
        _resident(wt.shape, const),
        _resident(gains.shape, const),
    ]
    args = [x, mods, wt, gains]
    if rope:
        in_specs += [pl.BlockSpec((hd, tm), lambda b, i: (0, i))] * 2
        args += list(rope_tabs)
    out_shape, out_specs = [], []
    if need_q:
        out_shape += [jax.ShapeDtypeStruct((nb, n_aq, t), BF16), jax.ShapeDtypeStruct((nb, n_bq, t), BF16)]
        out_specs += [pl.BlockSpec((1, n_aq, tm), ftok), pl.BlockSpec((1, n_bq, tm), ftok)]
    out_shape += [jax.ShapeDtypeStruct((nb, t, n_ak), BF16), jax.ShapeDtypeStruct((nb, t, n_bq), BF16),
                  jax.ShapeDtypeStruct((nb, n_av, t), BF16), jax.ShapeDtypeStruct((nb, n_bv, t), BF16)]
    out_specs += [pl.BlockSpec((1, tm, n_ak), tok), pl.BlockSpec((1, tm, n_bq), tok),
                  pl.BlockSpec((1, n_av, tm), ftok), pl.BlockSpec((1, n_bv, tm), ftok)]
    return pl.pallas_call(
        functools.partial(_proj_ab_kernel, rope=rope, need_q=need_q),
        out_shape=out_shape,
        grid=(nb, t // tm),
        in_specs=in_specs,
        out_specs=out_specs,
        compiler_params=_params(),
        name="proj_ab",
    )(*args)


def _proj_cd_kernel(*refs, rope, need_q):
    (x_ref, mod_ref, wt_ref, wuq_ref, wuk_ref, wuv_ref, g_qa_ref, g_kva_ref, g_head_ref) = refs[:9]
    refs = refs[9:]
    if rope:
        cos_ref, sin_ref = refs[:2]
        refs = refs[2:]
    if need_q:
        cqt_ref, dqt_ref = refs[:2]
        refs = refs[2:]
    ck_ref, cvt_ref, dk_ref, dvt_ref = refs

    h = _norm_modulate(x_ref[0], mod_ref[0, 3:4, :], mod_ref[0, 4:5, :])
    tm = h.shape[0]
    ht = h.T.astype(BF16)

    hd = HEAD_DIM
    n_cq = C_Q_RANK
    n_dq = D_HEADS * hd
    n_ckv = C_KV_RANK + C_ROPE
    g_qn = g_head_ref[0:64]
    g_qr = g_head_ref[64:96]
    g_kn = g_head_ref[96:160]
    g_kr = g_head_ref[160:192]
    g_dq = g_head_ref[192:256]
    g_dk = g_head_ref[256:320]
    if rope:
        cos = cos_ref[...]
        sin = sin_ref[...]
    zpad = jnp.zeros((C_PAD - C_NOPE - C_ROPE, tm), F32)

    def latent_queries(z):
        cqn = _rms_rows(z, g_qa_ref[...]).astype(BF16)
        qt = jnp.dot(wuq_ref[...], cqn, preferred_element_type=F32)
        cscale = (C_NOPE + C_ROPE) ** -0.5 * LOG2E
        for hh in range(C_HEADS):
            qn = _rms_rows(qt[hh * C_NOPE:(hh + 1) * C_NOPE], g_qn)
            rr = C_HEADS * C_NOPE + hh * C_ROPE
            qr = _rms_rows(qt[rr:rr + C_ROPE], g_qr)
            if rope:
                qr = _rope_rows(qr, cos, sin, C_ROPE // 4)
            qh = jnp.concatenate([qn, qr, zpad], axis=0) * cscale
            cqt_ref[0, hh * C_PAD:(hh + 1) * C_PAD, :] = qh.astype(BF16)

    def grid_queries(z):
        dscale = hd ** -0.5 * LOG2E
        for hh in range(D_HEADS):
            dqt_ref[0, hh * hd:(hh + 1) * hd, :] = (_rms_rows(z[hh * hd:(hh + 1) * hd], g_dq) * dscale).astype(BF16)

    def latent_keys_values(z):
        cn = _rms_rows(z[:C_KV_RANK], g_kva_ref[...]).astype(BF16)
        kr = _rms_rows(z[C_KV_RANK:], g_kr)
        if rope:
            kr = _rope_rows(kr, cos, sin, C_ROPE // 4)
        knt = jnp.dot(wuk_ref[...], cn, preferred_element_type=F32)
        for hh in range(C_HEADS):
            kn = _rms_rows(knt[hh * C_NOPE:(hh + 1) * C_NOPE], g_kn)
            _store_token_major(ck_ref, hh * C_PAD, [kn, kr, zpad])
        _store_values_t(cvt_ref, jnp.dot(wuv_ref[...], cn, preferred_element_type=F32), C_V)

    def grid_keys(z):
        for j in range(n_dq // LANES):
            rr = j * LANES
            _store_token_major(dk_ref, j * LANES, [_rms_rows(z[rr:rr + hd], g_dk),
                                                   _rms_rows(z[rr + hd:rr + 2 * hd], g_dk)])

    def grid_values(z):
        _store_values_t(dvt_ref, z, hd)

    groups = [(n_cq, latent_queries), (n_dq, grid_queries)] if need_q else []
    groups += [(n_ckv, latent_keys_values), (n_dq, grid_keys), (n_dq, grid_values)]
    _grouped_projection(wt_ref, ht, groups)


def _proj_cd(x, mods, mod_row, wt, wuq, wuk, wuv, g_qa, g_kva, g_head, rope_tabs, *, need_q):
    nb, t, d = x.shape
    tm = min(TOKEN_TILE, t)
    rope = rope_tabs is not None
    hd = HEAD_DIM
    n_c = C_HEADS * C_PAD
    n_d = D_HEADS * hd
    n_cv = C_HEADS * (C_V + ONES_ROWS)
    n_dv = D_HEADS * (hd + ONES_ROWS)
    tok = lambda b, i: (b, i, 0)
    ftok = lambda b, i: (b, 0, i)
    const = lambda b, i: (0, 0)
    consts = [wt, wuq, wuk, wuv, g_qa, g_kva, g_head]
    in_specs = [pl.BlockSpec((1, tm, d), tok),
                pl.BlockSpec((1, N_MOD, d), lambda b, i: (mod_row(b), 0, 0))]
    in_specs += [_resident(a.shape, const) for a in consts]
    args = [x, mods] + consts
    if rope:
        in_specs += [pl.BlockSpec((C_ROPE, tm), lambda b, i: (0, i))] * 2
        args += list(rope_tabs)
    out_shape, out_specs = [], []
    if need_q:
        out_shape += [jax.ShapeDtypeStruct((nb, n_c, t), BF16), jax.ShapeDtypeStruct((nb, n_d, t), BF16)]
        out_specs += [pl.BlockSpec((1, n_c, tm), ftok), pl.BlockSpec((1, n_d, tm), ftok)]
    out_shape += [jax.ShapeDtypeStruct((nb, t, n_c), BF16), jax.ShapeDtypeStruct((nb, n_cv, t), BF16),
                  jax.ShapeDtypeStruct((nb, t, n_d), BF16), jax.ShapeDtypeStruct((nb, n_dv, t), BF16)]
    out_specs += [pl.BlockSpec((1, tm, n_c), tok), pl.BlockSpec((1, n_cv, tm), ftok),
                  pl.BlockSpec((1, tm, n_d), tok), pl.BlockSpec((1, n_dv, tm), ftok)]
    return pl.pallas_call(
        functools.partial(_proj_cd_kernel, rope=rope, need_q=need_q),
        out_shape=out_shape,
        grid=(nb, t // tm),
        in_specs=in_specs,
        out_specs=out_specs,
        compiler_params=_params(),
        name="proj_cd",
    )(*args)


def _attend_units(n_units, n_chunks, rhs_of, score_chunk, value_chunk, s_ref, p_ref, extra=None):
    stats = {}
    outs = [None] * n_units
    tq = s_ref.shape[2]
    for t in range(n_units + 2):
        ua, ub, uc = t, t - 1, t - 2
        do_a, do_b, do_c = ua < n_units, 0 <= ub < n_units, 0 <= uc < n_units
        if do_a:
            rhs = rhs_of(ua)
            m8 = None
        acc = None
        for c in range(n_chunks):
            rows = slice(c * KEY_CHUNK, (c + 1) * KEY_CHUNK)
            if do_a:
                s = score_chunk(ua, c, rhs)
                s_ref[ua % 2, rows, :] = s
                pm = jnp.max(s.reshape(KEY_CHUNK // 8, 8, tq), axis=0)
                m8 = pm if m8 is None else jnp.maximum(m8, pm)
            if do_b:
                p_ref[ub % 2, rows, :] = jnp.exp2(s_ref[ub % 2, rows, :] - stats[ub]).astype(BF16)
            if do_c:
                part = jnp.dot(value_chunk(uc, c), p_ref[uc % 2, rows, :], preferred_element_type=F32)
                acc = part if acc is None else acc + part
        if do_a:
            m = jnp.max(m8, axis=0, keepdims=True)
            stats[ua] = m if extra is None else jnp.maximum(m, extra(ua))
        if do_c:
            dv = acc.shape[0] - ONES_ROWS
            total = acc[dv:dv + 1]
            if extra is not None:
                total = total + jnp.exp2(extra(uc) - stats[uc])
            outs[uc] = acc[:dv] * (1.0 / total)
    return outs


def _half_rhs(q_head, upper):
    zeros = jnp.zeros_like(q_head)
    return jnp.concatenate([zeros, q_head] if upper else [q_head, zeros], axis=0)


def _store_outputs(o_ref, outs, tok0=0):
    rows = jnp.concatenate(outs, axis=0)
    tq = rows.shape[1]
    for j in range(rows.shape[0] // LANES):
        o_ref[0, tok0:tok0 + tq, j * LANES:(j + 1) * LANES] = rows[j * LANES:(j + 1) * LANES].T.astype(BF16)


def _attn_scratch(n_keys, tq):
    return [pltpu.VMEM((2, n_keys, tq), F32), pltpu.VMEM((2, n_keys, tq), BF16)]


def _attn_a_kernel(*refs, latent, seq):
    if latent:
        (sink_ref, qt_ref, kp_ref, kc_ref, kn_ref, kx_ref, vp_ref, vc_ref, vn_ref, vx_ref,
         o_ref, s_ref, p_ref) = refs
    else:
        sink_ref, qt_ref, kx_ref, vx_ref, o_ref, s_ref, p_ref = refs
    qt = qt_ref[0]
    hd = HEAD_DIM
    tq = s_ref.shape[2]
    n_tiles = qt.shape[1] // tq
    group = A_HEADS // A_KV_HEADS
    kx = kx_ref[0]
    vx = vx_ref[0]
    lat_chunks = 0
    if latent:
        klat = jnp.concatenate([kp_ref[0], kc_ref[0], kn_ref[0]], axis=0)
        vlat = jnp.concatenate([vp_ref[0], vc_ref[0], vn_ref[0]], axis=1)
        n_lat = tq + 2 * A_WINDOW
        lat_chunks = n_lat // KEY_CHUNK
        krow = lax.broadcasted_iota(jnp.int32, (n_lat, tq), 0)
        qcol = lax.broadcasted_iota(jnp.int32, (n_lat, tq), 1)
        in_band = jnp.abs(krow - A_WINDOW - qcol) <= A_WINDOW
        biases = []
        for j in range(n_tiles):
            kpos = krow + (pl.program_id(1) * qt.shape[1] + j * tq - A_WINDOW)
            biases.append(jnp.where(in_band, jnp.where(kpos >= 0, jnp.where(kpos < seq, 0.0, NEG), NEG), NEG))
    n_chunks = lat_chunks + kx.shape[0] // KEY_CHUNK

    def rhs_of(u):
        j, hh = divmod(u, A_HEADS)
        return _half_rhs(qt[hh * hd:(hh + 1) * hd, j * tq:(j + 1) * tq], hh // group == 1)

    def score_chunk(u, c, rhs):
        j = u // A_HEADS
        if c < lat_chunks:
            r0 = j * tq + c * KEY_CHUNK
            return (jnp.dot(klat[r0:r0 + KEY_CHUNK], rhs, preferred_element_type=F32)
                    + biases[j][c * KEY_CHUNK:(c + 1) * KEY_CHUNK])
        r0 = (c - lat_chunks) * KEY_CHUNK
        return jnp.dot(kx[r0:r0 + KEY_CHUNK], rhs, preferred_element_type=F32)

    def value_chunk(u, c):
        j, hh = divmod(u, A_HEADS)
        g = hh // group
        rows = slice(g * (hd + ONES_ROWS), (g + 1) * (hd + ONES_ROWS))
        if c < lat_chunks:
            r0 = j * tq + c * KEY_CHUNK
            return vlat[rows, r0:r0 + KEY_CHUNK]
        r0 = (c - lat_chunks) * KEY_CHUNK
        return vx[rows, r0:r0 + KEY_CHUNK]

    outs = _attend_units(n_tiles * A_HEADS, n_chunks, rhs_of, score_chunk, value_chunk, s_ref, p_ref,
                         extra=lambda u: sink_ref[u % A_HEADS] * LOG2E)
    for j in range(n_tiles):
        _store_outputs(o_ref, outs[j * A_HEADS:(j + 1) * A_HEADS], j * tq)


def _attn_a(sink, qt, k, vt, kx, vxt, *, latent):
    nb, nq, t = qt.shape
    n_kv = A_KV_HEADS * HEAD_DIM
    n_v = vxt.shape[1]
    lx = kx.shape[1]
    smem = pl.BlockSpec(memory_space=pltpu.SMEM)
    if latent:
        tq = LOCAL_TILES * ATTN_Q_TILE
        seq = k.shape[1]
        w = A_WINDOW
        r = tq // w
        last = seq // w - 1
        prev = lambda i: jnp.maximum(i * r - 1, 0)
        nxt = lambda i: jnp.minimum((i + 1) * r, last)
        in_specs = [
            smem,
            pl.BlockSpec((1, nq, tq), lambda b, i: (b, 0, i)),
            pl.BlockSpec((1, w, n_kv), lambda b, i: (b, prev(i), 0)),
            pl.BlockSpec((1, tq, n_kv), lambda b, i: (b, i, 0)),
            pl.BlockSpec((1, w, n_kv), lambda b, i: (b, nxt(i), 0)),
            pl.BlockSpec((1, lx, n_kv), lambda b, i: (b, 0, 0)),
            pl.BlockSpec((1, n_v, w), lambda b, i: (b, 0, prev(i))),
            pl.BlockSpec((1, n_v, tq), lambda b, i: (b, 0, i)),
            pl.BlockSpec((1, n_v, w), lambda b, i: (b, 0, nxt(i))),
            pl.BlockSpec((1, n_v, lx), lambda b, i: (b, 0, 0)),
        ]
        args = [sink, qt, k, k, k, kx, vt, vt, vt, vxt]
        n_keys = ATTN_Q_TILE + 2 * w + lx
    else:
        tq = t
        seq = 0
        in_specs = [
            smem,
            pl.BlockSpec((1, nq, tq), lambda b, i: (b, 0, i)),
            pl.BlockSpec((1, lx, n_kv), lambda b, i: (b, 0, 0)),
            pl.BlockSpec((1, n_v, lx), lambda b, i: (b, 0, 0)),
        ]
        args = [sink, qt, kx, vxt]
        n_keys = lx
    return pl.pallas_call(
        functools.partial(_attn_a_kernel, latent=latent, seq=seq),
        out_shape=jax.ShapeDtypeStruct((nb, t, nq), BF16),
        grid=(nb, t // tq),
        in_specs=in_specs,
        out_specs=pl.BlockSpec((1, tq, nq), lambda b, i: (b, i, 0)),
        scratch_shapes=_attn_scratch(n_keys, min(ATTN_Q_TILE, tq)),
        compiler_params=_params(),
        name="attn_window",
    )(*args)


def _attn_b_kernel(*refs, latent, lam_init):
    if latent:
        lam_ref, gain_ref, qt_ref, k_ref, kx_ref, vt_ref, vxt_ref, o_ref, s_ref, p_ref = refs
    else:
        lam_ref, gain_ref, qt_ref, kx_ref, vxt_ref, o_ref, s_ref, p_ref = refs
    qt = qt_ref[0]
    hd = HEAD_DIM
    lv = lam_ref[...]
    lam = (jnp.exp(jnp.sum(lv[0:1] * lv[1:2], axis=-1, keepdims=True))
           - jnp.exp(jnp.sum(lv[2:3] * lv[3:4], axis=-1, keepdims=True)) + lam_init)
    gain = gain_ref[...] * (1.0 - lam_init)

    n_ctx = kx_ref.shape[1] // KEY_CHUNK

    def rhs_of(u):
        return _half_rhs(qt[u * hd:(u + 1) * hd], u % 2 == 1)

    def score_chunk(u, c, rhs):
        cols = slice((u // 2) * 2 * hd, (u // 2 + 1) * 2 * hd)
        if c < n_ctx:
            k = kx_ref[0, c * KEY_CHUNK:(c + 1) * KEY_CHUNK, cols]
        else:
            k = k_ref[0, (c - n_ctx) * KEY_CHUNK:(c - n_ctx + 1) * KEY_CHUNK, cols]
        return jnp.dot(k, rhs, preferred_element_type=F32)

    def value_chunk(u, c):
        stride = B_V_DIM + ONES_ROWS
        rows = slice((u // 2) * stride, (u // 2 + 1) * stride)
        if c < n_ctx:
            return vxt_ref[0, rows, c * KEY_CHUNK:(c + 1) * KEY_CHUNK]
        return vt_ref[0, rows, (c - n_ctx) * KEY_CHUNK:(c - n_ctx + 1) * KEY_CHUNK]

    outs = _attend_units(2 * B_HEADS, s_ref.shape[1] // KEY_CHUNK, rhs_of, score_chunk, value_chunk, s_ref, p_ref)
    heads = []
    for hh in range(B_HEADS):
        o = outs[2 * hh] - lam * outs[2 * hh + 1]
        heads.append(_rms_rows(o, gain))
    _store_outputs(o_ref, heads)


def _attn_b(lam_vecs, sub_gain, qt, k, vt, kx, vxt, *, latent, lam_init):
    nb, nq, t = qt.shape
    lx, nk = kx.shape[1], kx.shape[2]
    nv = vxt.shape[1]
    tq = min(FULL_Q_TILE, t)
    const = lambda b, i: (0, 0)
    whole = lambda b, i: (b, 0, 0)
    in_specs = [pl.BlockSpec(lam_vecs.shape, const), pl.BlockSpec(sub_gain.shape, const),
                pl.BlockSpec((1, nq, tq), lambda b, i: (b, 0, i))]
    args = [lam_vecs, sub_gain, qt]
    n_keys = lx
    if latent:
        seq = k.shape[1]
        in_specs += [pl.BlockSpec((1, seq, nk), whole), pl.BlockSpec((1, lx, nk), whole),
                     pl.BlockSpec((1, nv, seq), whole), pl.BlockSpec((1, nv, lx), whole)]
        args += [k, kx, vt, vxt]
        n_keys += seq
    else:
        in_specs += [pl.BlockSpec((1, lx, nk), whole), pl.BlockSpec((1, nv, lx), whole)]
        args += [kx, vxt]
    return pl.pallas_call(
        functools.partial(_attn_b_kernel, latent=latent, lam_init=lam_init),
        out_shape=jax.ShapeDtypeStruct((nb, t, B_HEADS * B_V_DIM), BF16),
        grid=(nb, t // tq),
        in_specs=in_specs,
        out_specs=pl.BlockSpec((1, tq, B_HEADS * B_V_DIM), lambda b, i: (b, i, 0)),
        scratch_shapes=_attn_scratch(n_keys, tq),
        compiler_params=_params(),
        name="attn_diff",
    )(*args)


def _attn_c_kernel(qt_ref, k_ref, kx_ref, vt_ref, vxt_ref, o_ref, s_ref, p_ref):
    qt = qt_ref[0]

    n_ctx = kx_ref.shape[1] // KEY_CHUNK

    def rhs_of(u):
        return qt[u * C_PAD:(u + 1) * C_PAD]

    def score_chunk(u, c, rhs):
        cols = slice(u * C_PAD, (u + 1) * C_PAD)
        if c < n_ctx:
            k = kx_ref[0, c * KEY_CHUNK:(c + 1) * KEY_CHUNK, cols]
        else:
            k = k_ref[0, (c - n_ctx) * KEY_CHUNK:(c - n_ctx + 1) * KEY_CHUNK, cols]
        return jnp.dot(k, rhs, preferred_element_type=F32)

    def value_chunk(u, c):
        rows = slice(u * (C_V + ONES_ROWS), (u + 1) * (C_V + ONES_ROWS))
        if c < n_ctx:
            return vxt_ref[0, rows, c * KEY_CHUNK:(c + 1) * KEY_CHUNK]
        return vt_ref[0, rows, (c - n_ctx) * KEY_CHUNK:(c - n_ctx + 1) * KEY_CHUNK]

    _store_outputs(o_ref, _attend_units(C_HEADS, s_ref.shape[1] // KEY_CHUNK, rhs_of, score_chunk, value_chunk,
                                        s_ref, p_ref))


def _attn_c(qt, k, vt, kx, vxt):
    nb, nq, t = qt.shape
    lx = kx.shape[1]
    nv = vt.shape[1]
    tq = FULL_Q_TILE
    whole = lambda b, i: (b, 0, 0)
    return pl.pallas_call(
        _attn_c_kernel,
        out_shape=jax.ShapeDtypeStruct((nb, t, C_HEADS * C_V), BF16),
        grid=(nb, t // tq),
        in_specs=[pl.BlockSpec((1, nq, tq), lambda b, i: (b, 0, i)),
                  pl.BlockSpec((1, t, nq), whole),
                  pl.BlockSpec((1, lx, nq), whole),
                  pl.BlockSpec((1, nv, t), whole),
                  pl.BlockSpec((1, nv, lx), whole)],
        out_specs=pl.BlockSpec((1, tq, C_HEADS * C_V), lambda b, i: (b, i, 0)),
        scratch_shapes=_attn_scratch(t + lx, tq),
        compiler_params=_params(),
        name="attn_latent",
    )(qt, k, kx, vt, vxt)


def _rpb_table_kernel(rpb_ref, o_ref, *, n_dr, n_dc):
    hh = pl.program_id(0)
    shape = (GRID_W, LANES)
    kc = lax.broadcasted_iota(jnp.int32, shape, 0)
    lane = lax.broadcasted_iota(jnp.int32, shape, 1)
    right = lane >= GRID_W
    qc = jnp.where(right, lane - GRID_W, lane)
    dc = jnp.clip(kc - qc, -(D_WIN_COLS - 1), D_WIN_COLS - 1) + (D_WIN_COLS - 1)
    cs = jnp.clip(qc - D_WIN_COLS // 2, 0, GRID_W - D_WIN_COLS)
    in_cols = jnp.where(kc >= cs, jnp.where(kc < cs + D_WIN_COLS, 0.0, NEG), NEG)
    for dd in range(n_dr + 1):
        base_l = (hh * n_dr + min(dd, n_dr - 1)) * n_dc
        base_r = (hh * n_dr + max(dd - 1, 0)) * n_dc
        acc = jnp.zeros(shape, F32)
        for c in range(n_dc):
            val = jnp.where(right, rpb_ref[base_r + c], rpb_ref[base_l + c])
            acc = jnp.where(dc == c, val, acc)
        ok = jnp.where(right, 0.0 if dd >= 1 else NEG, 0.0 if dd <= n_dr - 1 else NEG)
        o_ref[0, dd] = acc * LOG2E + ok + in_cols


def _rpb_table(rpb):
    n_h, n_dr, n_dc = rpb.shape
    return pl.pallas_call(
        functools.partial(_rpb_table_kernel, n_dr=n_dr, n_dc=n_dc),
        out_shape=jax.ShapeDtypeStruct((n_h, n_dr + 1, GRID_W, LANES), F32),
        grid=(n_h,),
        in_specs=[pl.BlockSpec(memory_space=pltpu.SMEM)],
        out_specs=pl.BlockSpec((1, n_dr + 1, GRID_W, LANES), lambda h: (h, 0, 0, 0)),
        name="rpb_table",
    )(rpb.reshape(-1))


def _attn_d_kernel(tab_ref, qt_ref, k_ref, kx_ref, vt_ref, vxt_ref, o_ref, s_ref, p_ref, *, rows):
    qt = qt_ref[0]
    hd = HEAD_DIM
    tq = s_ref.shape[2]
    n_tiles = qt.shape[1] // tq
    kh = min(D_WIN_ROWS, rows)
    nwin = NAT_WIN_ROWS * GRID_W
    lane = lax.broadcasted_iota(jnp.int32, (GRID_W, LANES), 1)
    right = lane >= GRID_W
    n_dd = tab_ref.shape[1]
    q_pairs = NAT_Q_ROWS // 2
    rows_per_chunk = KEY_CHUNK // GRID_W
    n_win = nwin // KEY_CHUNK

    def in_window(kr, qr):
        rs = jnp.clip(qr - kh // 2, 0, rows - kh)
        return jnp.where((kr >= rs) & (kr < rs + kh), 0.0, NEG)

    tiles = []
    for j in range(n_tiles):
        r_first = (pl.program_id(1) * n_tiles + j) * NAT_Q_ROWS
        ws = jnp.clip(r_first - kh // 2, 0, rows - kh)
        wl = jnp.minimum(ws, rows - NAT_WIN_ROWS)
        tok0 = pl.multiple_of(wl * GRID_W, LANES)
        row_add = jnp.concatenate([
            jnp.concatenate([jnp.where(right, in_window(wl + jr, r_first + 2 * a + 1),
                                       in_window(wl + jr, r_first + 2 * a))
                             for a in range(q_pairs)], axis=1)
            for jr in range(NAT_WIN_ROWS)], axis=0)
        tiles.append((r_first, wl, k_ref[0, pl.ds(tok0, nwin), :], vt_ref[0, :, pl.ds(tok0, nwin)], row_add))

    def rhs_of(u):
        j, hh = divmod(u, D_HEADS)
        return _half_rhs(qt[hh * hd:(hh + 1) * hd, j * tq:(j + 1) * tq], hh % 2 == 1)

    def score_chunk(u, c, rhs):
        j, hh = divmod(u, D_HEADS)
        cols = slice((hh // 2) * LANES, (hh // 2 + 1) * LANES)
        if c >= n_win:
            return jnp.dot(kx_ref[0, (c - n_win) * KEY_CHUNK:(c - n_win + 1) * KEY_CHUNK, cols], rhs,
                           preferred_element_type=F32)
        r_first, wl, kwin, _, row_add = tiles[j]
        rows_c = slice(c * KEY_CHUNK, (c + 1) * KEY_CHUNK)
        bias = jnp.concatenate([
            jnp.concatenate([tab_ref[hh, jnp.clip((wl + jr) - (r_first + 2 * a) + (D_WIN_ROWS - 1), 0, n_dd - 1)]
                             for a in range(q_pairs)], axis=1)
            for jr in range(c * rows_per_chunk, (c + 1) * rows_per_chunk)], axis=0)
        return jnp.dot(kwin[rows_c, cols], rhs, preferred_element_type=F32) + (bias + row_add[rows_c])

    def value_chunk(u, c):
        j, hh = divmod(u, D_HEADS)
        rows_u = slice(hh * (hd + ONES_ROWS), (hh + 1) * (hd + ONES_ROWS))
        if c >= n_win:
            return vxt_ref[0, rows_u, (c - n_win) * KEY_CHUNK:(c - n_win + 1) * KEY_CHUNK]
        return tiles[j][3][rows_u, c * KEY_CHUNK:(c + 1) * KEY_CHUNK]

    outs = _attend_units(n_tiles * D_HEADS, s_ref.shape[1] // KEY_CHUNK, rhs_of, score_chunk, value_chunk,
                         s_ref, p_ref)
    for j in range(n_tiles):
        _store_outputs(o_ref, outs[j * D_HEADS:(j + 1) * D_HEADS], j * tq)


def _attn_d(table, qt, k, vt, kx, vxt):
    nb, nq, t = qt.shape
    lx = kx.shape[1]
    tq = LOCAL_TILES * ATTN_Q_TILE
    rows = t // GRID_W
    assert rows >= NAT_WIN_ROWS and rows % NAT_Q_ROWS == 0 and NAT_Q_ROWS % 2 == 0
    whole = lambda b, i: (b, 0, 0)
    return pl.pallas_call(
        functools.partial(_attn_d_kernel, rows=rows),
        out_shape=jax.ShapeDtypeStruct((nb, t, nq), BF16),
        grid=(nb, t // tq),
        in_specs=[_resident(table.shape, lambda b, i: (0, 0, 0, 0)),
                  pl.BlockSpec((1, nq, tq), lambda b, i: (b, 0, i)),
                  pl.BlockSpec((1, t, nq), whole),
                  pl.BlockSpec((1, lx, nq), whole),
                  pl.BlockSpec((1, vt.shape[1], t), whole),
                  pl.BlockSpec((1, vt.shape[1], lx), whole)],
        out_specs=pl.BlockSpec((1, tq, nq), lambda b, i: (b, i, 0)),
        scratch_shapes=_attn_scratch(NAT_WIN_ROWS * GRID_W + lx, ATTN_Q_TILE),
        compiler_params=_params(),
        name="attn_neighbourhood",
    )(table, qt, k, kx, vt, vxt)


def _rope_tables_t(n, rot_dim):
    nf = rot_dim // 4
    inv = ROPE_BASE ** (-jnp.arange(nf, dtype=F32) / nf)
    t = jnp.arange(n)
    row = (t // GRID_W).astype(F32)
    col = (t % GRID_W).astype(F32)
    ang = jnp.concatenate([row[:, None] * inv, col[:, None] * inv], axis=-1)
    cos, sin = jnp.cos(ang).T, jnp.sin(ang).T
    cos_e = jnp.concatenate([cos[:nf], cos[:nf], cos[nf:], cos[nf:]], axis=0)
    sin_e = jnp.concatenate([-sin[:nf], sin[:nf], -sin[nf:], sin[nf:]], axis=0)
    return cos_e, sin_e


def _col(v):
    return v.astype(F32).reshape(-1, 1)


def kernel(x, c, ctx, c_ctx, w_mod, b_mod,
           ffn1_w_gate, ffn1_w_up, ffn1_w_down, ffn2_w_gate, ffn2_w_up, ffn2_w_down,
           ab_w_in, ab_w_out, a_q_norm, a_k_norm, a_sink, b_q_norm, b_k_norm,
           b_lambda_q1, b_lambda_k1, b_lambda_q2, b_lambda_k2, b_sub_norm,
           cd_w_in, cd_w_out, c_q_a_norm, c_kv_a_norm, c_w_uq, c_w_ukv,
           c_q_nope_norm, c_q_rope_norm, c_k_nope_norm, c_k_rope_norm,
           d_q_norm, d_k_norm, d_rpb):
    nb, seq, d = x.shape
    lctx = ctx.shape[1]
    depth = w_mod.shape[0]
    hd = HEAD_DIM

    n_rows = -(-(nb + 1) // 8) * 8
    c_rows = jnp.concatenate([c, c_ctx[None, :], jnp.zeros((n_rows - nb - 1, d), F32)], axis=0)
    mods_all = _mod_vectors(c_rows, w_mod, b_mod).reshape(depth, n_rows, N_MOD, d)
    x_row = lambda b: b
    ctx_row = lambda b: nb

    rope_head = _rope_tables_t(seq, hd)
    rope_mla = _rope_tables_t(seq, C_ROPE)

    w1 = tuple(_cast_bf16(w) for w in (ffn1_w_gate, ffn1_w_up, ffn1_w_down))
    w2 = tuple(_cast_bf16(w) for w in (ffn2_w_gate, ffn2_w_up, ffn2_w_down))
    ab_out = _cast_bf16(ab_w_out)
    cd_out = _cast_bf16(cd_w_out)

    xc = ctx.reshape(1, nb * lctx, d)
    for l in range(depth):
        need_ctx = l < depth - 1
        mods = mods_all[l]
        x = _ffn(x, mods, x_row, *w1, l, j0=0)
        xc = _ffn(xc, mods, ctx_row, *w1, l, j0=0)
        xc_b = xc.reshape(nb, lctx, d)
        i = l // 2
        if l % 2 == 0:
            lam_init = 0.8 - 0.6 * math.exp(-0.3 * l)
            w_in = ab_w_in[i]
            n_q = A_HEADS * hd + B_HEADS * 2 * hd
            n_ak = A_KV_HEADS * hd
            n_bk = B_HEADS * 2 * hd
            o_av = n_q + n_ak
            o_bk = o_av + n_ak
            o_bv = o_bk + n_bk
            wt = jnp.concatenate([w_in[:, :o_av], w_in[:, o_bk:o_bv], w_in[:, o_av:o_bk], w_in[:, o_bv:]],
                                 axis=1).T.astype(BF16)
            gains = jnp.concatenate([_col(a_q_norm[i]), _col(b_q_norm[i]),
                                     _col(a_k_norm[i]), _col(b_k_norm[i])], axis=0)
            aqt, bqt, ak, bk, avt, bvt = _proj_ab(x, mods, x_row, wt, gains, rope_head, need_q=True)
            ctx_out = _proj_ab(xc_b, mods, ctx_row, wt if need_ctx else wt[n_q:], gains, None, need_q=need_ctx)
            akx, bkx, avxt, bvxt = ctx_out[-4:]
            sink = a_sink[i].astype(F32)
            lam_vecs = jnp.stack([b_lambda_q1[i], b_lambda_k1[i], b_lambda_q2[i], b_lambda_k2[i]]).astype(F32)
            sub_gain = _col(b_sub_norm[i])
            y1 = _attn_a(sink, aqt, ak, avt, akx, avxt, latent=True)
            y2 = _attn_b(lam_vecs, sub_gain, bqt, bk, bvt, bkx, bvxt, latent=True, lam_init=lam_init)
            w_out = ab_out
            if need_ctx:
                aqxt, bqxt = ctx_out[:2]
                y1x = _attn_a(sink, aqxt, None, None, akx, avxt, latent=False)
                y2x = _attn_b(lam_vecs, sub_gain, bqxt, None, None, bkx, bvxt, latent=False, lam_init=lam_init)
        else:
            assert not need_ctx, "context queries of an odd layer are only needed for depth > 2"
            w_in = cd_w_in[i]
            n_q = C_Q_RANK + D_HEADS * hd
            wt = w_in.T.astype(BF16)
            uq = c_w_uq[i].reshape(C_Q_RANK, C_HEADS, C_NOPE + C_ROPE)
            wuq = jnp.concatenate([uq[:, :, :C_NOPE].reshape(C_Q_RANK, -1),
                                   uq[:, :, C_NOPE:].reshape(C_Q_RANK, -1)], axis=1).T.astype(BF16)
            ukv = c_w_ukv[i].reshape(C_KV_RANK, C_HEADS, C_NOPE + C_V)
            wuk = ukv[:, :, :C_NOPE].reshape(C_KV_RANK, -1).T.astype(BF16)
            wuv = ukv[:, :, C_NOPE:].reshape(C_KV_RANK, -1).T.astype(BF16)
            g_qa = _col(c_q_a_norm[i])
            g_kva = _col(c_kv_a_norm[i])
            g_head = jnp.concatenate([_col(c_q_nope_norm[i]), _col(c_q_rope_norm[i]),
                                      _col(c_k_nope_norm[i]), _col(c_k_rope_norm[i]),
                                      _col(d_q_norm[i]), _col(d_k_norm[i])], axis=0)
            cqt, dqt, ck, cvt, dk, dvt = _proj_cd(x, mods, x_row, wt, wuq, wuk, wuv,
                                                  g_qa, g_kva, g_head, rope_mla, need_q=True)
            ckx, cvxt, dkx, dvxt = _proj_cd(xc_b, mods, ctx_row, wt[n_q:], wuq, wuk, wuv,
                                            g_qa, g_kva, g_head, None, need_q=False)
            y1 = _attn_c(cqt, ck, cvt, ckx, cvxt)
            y2 = _attn_d(_rpb_table(d_rpb[i].astype(F32)), dqt, dk, dvt, dkx, dvxt)
            w_out = cd_out
        x = _ffn(x, mods, x_row, *w2, l, j0=6, y=(y1, y2), w_out=w_out, out_layer=i)
        if need_ctx:
            half = y1x.shape[-1]
            xc = _ffn(xc, mods, ctx_row, *w2, l, j0=6, w_out=w_out, out_layer=i,
                      y=(y1x.reshape(1, nb * lctx, half), y2x.reshape(1, nb * lctx, half)))
    return x
```

```python
import functools
import math

import jax
import jax.numpy as jnp
from jax import lax
from jax.experimental import pallas as pl
from jax.experimental.pallas import tpu as pltpu

F32 = jnp.float32
BF16 = jnp.bfloat16

D_MODEL = 1024
GRID_W = 64
HEAD_DIM = 64
D_FF = 2816
N_MOD = 9
ROPE_BASE = 10000.0
EPS = 1e-6
NEG = -1e30
LOG2E = math.log2(math.e)
A_HEADS = 8
A_KV_HEADS = 2
A_WINDOW = 128
B_HEADS = 4
B_V_DIM = 2 * HEAD_DIM
C_HEADS = 8
C_Q_RANK = 768
C_KV_RANK = 256
C_NOPE = 64
C_ROPE = 32
C_V = 64
C_PAD = 128
D_HEADS = 8
D_WIN_ROWS = 8
D_WIN_COLS = 16

LANES = 128
VMEM_LIMIT_BYTES = 56 * 1024 * 1024

CAST_ROWS = 256
FFN_CHUNK = 256
TOKEN_TILE = 512
FFN_TOKEN_TILE = 1024
ATTN_Q_TILE = 256
LOCAL_TILES = 4
FULL_Q_TILE = 512
FULL_TILES = 2
KEY_CHUNK = 256
ONES_ROWS = 16
NAT_Q_ROWS = ATTN_Q_TILE // GRID_W
NAT_WIN_ROWS = 12


def _params(**flags):
    return pltpu.CompilerParams(vmem_limit_bytes=VMEM_LIMIT_BYTES, flags=flags or None)


def _resident(block_shape, index_map):
    return pl.BlockSpec(block_shape, index_map, pipeline_mode=pl.Buffered(1))


def _sigmoid(x):
    return 1.0 / (1.0 + jnp.exp(-x))


def _rms_rows(z, gain):
    ms = jnp.mean(z * z, axis=0, keepdims=True)
    return z * lax.rsqrt(ms + EPS) * gain


def _norm_modulate(x, shift, scale):
    r = lax.rsqrt(jnp.mean(x * x, axis=-1, keepdims=True) + EPS)
    return (x * r) * (1.0 + scale) + shift


def _rope_rows(y, cos, sin, nf):
    part = jnp.concatenate([y[nf:2 * nf], y[0:nf], y[3 * nf:4 * nf], y[2 * nf:3 * nf]], axis=0)
    return y * cos + part * sin


def _mod_kernel(c_ref, w_ref, b_ref, o_ref):
    c = c_ref[...]
    a = (c * _sigmoid(c)).astype(BF16)
    o_ref[0] = jnp.dot(a, w_ref[0].astype(BF16), preferred_element_type=F32) + b_ref[0]


def _mod_vectors(c_rows, w_mod, b_mod):
    depth, d, n = w_mod.shape
    rows = c_rows.shape[0]
    tn = 1152
    return pl.pallas_call(
        _mod_kernel,
        out_shape=jax.ShapeDtypeStruct((depth, rows, n), F32),
        grid=(depth, n // tn),
        in_specs=[
            pl.BlockSpec((rows, d), lambda l, j: (0, 0)),
            pl.BlockSpec((1, d, tn), lambda l, j: (l, 0, j)),
            pl.BlockSpec((1, 1, tn), lambda l, j: (l, 0, j)),
        ],
        out_specs=pl.BlockSpec((1, rows, tn), lambda l, j: (l, 0, j)),
        compiler_params=_params(),
        name="mod_vectors",
    )(c_rows, w_mod, b_mod.reshape(depth, 1, n))


def _cast_kernel(w_ref, o_ref):
    o_ref[...] = w_ref[...].astype(BF16)


def _cast_bf16(w):
    depth, rows, cols = w.shape
    tr = min(CAST_ROWS, rows)
    spec = pl.BlockSpec((1, tr, cols), lambda l, i: (l, i, 0))
    return pl.pallas_call(
        _cast_kernel,
        out_shape=jax.ShapeDtypeStruct(w.shape, BF16),
        grid=(depth, rows // tr),
        in_specs=[spec],
        out_specs=spec,
        name="cast_bf16",
    )(w)


def _ffn_kernel(*refs, j0, fuse_out):
    if fuse_out:
        x_ref, y1_ref, y2_ref, mod_ref, wo_ref, wg_ref, wu_ref, wd_ref, o_ref, acc_ref = refs
    else:
        x_ref, mod_ref, wg_ref, wu_ref, wd_ref, o_ref, acc_ref = refs
    x = x_ref[0]
    if fuse_out:
        half = y1_ref.shape[-1]
        y = (jnp.dot(y1_ref[0], wo_ref[0, 0:half, :], preferred_element_type=F32)
             + jnp.dot(y2_ref[0], wo_ref[0, half:, :], preferred_element_type=F32))
        x = x + mod_ref[0, 5:6, :] * y
    shift = mod_ref[0, j0:j0 + 1, :]
    scale = mod_ref[0, j0 + 1:j0 + 2, :]
    gate = mod_ref[0, j0 + 2:j0 + 3, :]
    h = _norm_modulate(x, shift, scale).astype(BF16)
    d_ff = wg_ref.shape[2]
    for c in range(d_ff // FFN_CHUNK):
        lo, hi = c * FFN_CHUNK, (c + 1) * FFN_CHUNK
        g = jnp.dot(h, wg_ref[0, :, lo:hi], preferred_element_type=F32)
        u = jnp.dot(h, wu_ref[0, :, lo:hi], preferred_element_type=F32)
        a = (g * _sigmoid(g) * u).astype(BF16)
        part = jnp.dot(a, wd_ref[0, lo:hi, :], preferred_element_type=F32)
        if c == 0:
            acc_ref[...] = part
        else:
            acc_ref[...] += part
    o_ref[0] = x + (0.5 * gate) * acc_ref[...]


def _ffn(x, mods, mod_row, wg, wu, wd, layer, *, j0, y=None, w_out=None, out_layer=0):
    nb, t, d = x.shape
    tm = min(FFN_TOKEN_TILE, t)
    d_ff = wg.shape[2]
    fuse_out = y is not None
    tok = lambda b, i: (b, i, 0)
    this_layer = lambda b, i: (layer, 0, 0)
    in_specs = [pl.BlockSpec((1, tm, d), tok)]
    args = [x]
    if fuse_out:
        half = y[0].shape[-1]
        in_specs += [pl.BlockSpec((1, tm, half), tok), pl.BlockSpec((1, tm, half), tok)]
        args += [y[0], y[1]]
    in_specs.append(pl.BlockSpec((1, N_MOD, d), lambda b, i: (mod_row(b), 0, 0)))
    args.append(mods)
    if fuse_out:
        in_specs.append(_resident((1,) + w_out.shape[1:], lambda b, i: (out_layer, 0, 0)))
        args.append(w_out)
    in_specs += [_resident((1, d, d_ff), this_layer), _resident((1, d, d_ff), this_layer),
                 _resident((1, d_ff, d), this_layer)]
    args += [wg, wu, wd]
    return pl.pallas_call(
        functools.partial(_ffn_kernel, j0=j0, fuse_out=fuse_out),
        out_shape=jax.ShapeDtypeStruct(x.shape, F32),
        grid=(nb, t // tm),
        in_specs=in_specs,
        out_specs=pl.BlockSpec((1, tm, d), tok),
        scratch_shapes=[pltpu.VMEM((tm, d), F32)],
        compiler_params=_params(),
        name="ffn_out" if fuse_out else "ffn",
    )(*args)


def _store_token_major(ref, col0, blocks):
    width = sum(b.shape[0] for b in blocks)
    ref[0, :, col0:col0 + width] = jnp.concatenate(blocks, axis=0).T.astype(BF16)


def _store_values_t(ref, zrows, dv):
    ones = jnp.ones((ONES_ROWS, zrows.shape[1]), F32)
    stride = dv + ONES_ROWS
    for hh in range(zrows.shape[0] // dv):
        block = jnp.concatenate([zrows[hh * dv:(hh + 1) * dv], ones], axis=0)
        ref[0, hh * stride:(hh + 1) * stride, :] = block.astype(BF16)


def _copy_once(src_ref, dst_ref):
    first = (pl.program_id(0) == 0) & (pl.program_id(1) == 0)

    @pl.when(first)
    def _():
        dst_ref[...] = src_ref[...]


def _grouped_projection(wt_ref, ht, groups):
    r0, pending = 0, None
    for n, epilogue in groups:
        z = jnp.dot(wt_ref[r0:r0 + n, :], ht, preferred_element_type=F32)
        if pending is not None:
            pending[0](pending[1])
        pending = (epilogue, z)
        r0 += n
    pending[0](pending[1])


def _proj_ab_kernel(*refs, rope, need_q):
    x_ref, mod_ref, wt_ref, gain_ref = refs[:4]
    refs = refs[4:]
    if rope:
        cos_ref, sin_ref = refs[:2]
        refs = refs[2:]
    if need_q:
        aqt_ref, bqt_ref = refs[:2]
        refs = refs[2:]
    ak_ref, bk_ref, avt_ref, bvt_ref, wt_scr = refs
    _copy_once(wt_ref, wt_scr)

    h = _norm_modulate(x_ref[0], mod_ref[0, 3:4, :], mod_ref[0, 4:5, :])
    ht = h.T.astype(BF16)
    hd = HEAD_DIM
    g_aq = gain_ref[0 * hd:1 * hd]
    g_bq = gain_ref[1 * hd:2 * hd]
    g_ak = gain_ref[2 * hd:3 * hd]
    g_bk = gain_ref[3 * hd:4 * hd]
    if rope:
        cos = cos_ref[...]
        sin = sin_ref[...]

    def head(z, row0, gain):
        y = _rms_rows(z[row0:row0 + hd], gain)
        if rope:
            y = _rope_rows(y, cos, sin, hd // 4)
        return y

    n_aq = A_HEADS * hd
    n_bq = B_HEADS * 2 * hd
    n_ak = A_KV_HEADS * hd
    qscale = hd ** -0.5 * LOG2E

    def queries(ref, gain):
        def epilogue(z):
            for j in range(z.shape[0] // hd):
                ref[0, j * hd:(j + 1) * hd, :] = (head(z, j * hd, gain) * qscale).astype(BF16)
        return epilogue

    def keys(z):
        _store_token_major(ak_ref, 0, [head(z, j * hd, g_ak) for j in range(A_KV_HEADS)])
        for j in range(n_bq // LANES):
            r0 = n_ak + j * LANES
            _store_token_major(bk_ref, j * LANES, [head(z, r0, g_bk), head(z, r0 + hd, g_bk)])

    def values(z):
        _store_values_t(avt_ref, z[:n_ak], hd)
        _store_values_t(bvt_ref, z[n_ak:], B_V_DIM)

    groups = [(n_aq, queries(aqt_ref, g_aq)), (n_bq, queries(bqt_ref, g_bq))] if need_q else []
    groups += [(n_ak + n_bq, keys), (wt_ref.shape[0] - sum(n for n, _ in groups) - n_ak - n_bq, values)]
    _grouped_projection(wt_scr, ht, groups)


def _proj_ab(x, mods, mod_row, wt, gains, rope_tabs, *, need_q):
    nb, t, d = x.shape
    tm = min(TOKEN_TILE, t)
    rope = rope_tabs is not None
    hd = HEAD_DIM
    n_aq, n_bq, n_ak = A_HEADS * hd, B_HEADS * 2 * hd, A_KV_HEADS * hd
    n_av, n_bv = A_KV_HEADS * (hd + ONES_ROWS), B_HEADS * (B_V_DIM + ONES_ROWS)
    tok = lambda b, i: (b, i, 0)
    ftok = lambda b, i: (b, 0, i)
    const = lambda b, i: (0, 0)
    in_specs = [
        pl.BlockSpec((1, tm, d), tok),
        pl.BlockSpec((1, N_MOD, d), lambda b, i: (mod_row(b), 0, 0)),
        _resident(wt.shape, const),
        _resident(gains.shape, const),
    ]
    args = [x, mods, wt, gains]
    if rope:
        in_specs += [pl.BlockSpec((hd, tm), lambda b, i: (0, i))] * 2
        args += list(rope_tabs)
    out_shape, out_specs = [], []
    if need_q:
        out_shape += [jax.ShapeDtypeStruct((nb, n_aq, t), BF16), jax.ShapeDtypeStruct((nb, n_bq, t), BF16)]
        out_specs += [pl.BlockSpec((1, n_aq, tm), ftok), pl.BlockSpec((1, n_bq, tm), ftok)]
    out_shape += [jax.ShapeDtypeStruct((nb, t, n_ak), BF16), jax.ShapeDtypeStruct((nb, t, n_bq), BF16),
                  jax.ShapeDtypeStruct((nb, n_av, t), BF16), jax.ShapeDtypeStruct((nb, n_bv, t), BF16)]
    out_specs += [pl.BlockSpec((1, tm, n_ak), tok), pl.BlockSpec((1, tm, n_bq), tok),
                  pl.BlockSpec((1, n_av, tm), ftok), pl.BlockSpec((1, n_bv, tm), ftok)]
    return pl.pallas_call(
        functools.partial(_proj_ab_kernel, rope=rope, need_q=need_q),
        out_shape=out_shape,
        grid=(nb, t // tm),
        in_specs=in_specs,
        out_specs=out_specs,
        scratch_shapes=[pltpu.VMEM(wt.shape, BF16)],
        compiler_params=_params(),
        name="proj_ab",
    )(*args)


def _proj_cd_kernel(*refs, rope, need_q):
    (x_ref, mod_ref, wt_ref, wuq_ref, wuk_ref, wuv_ref, g_qa_ref, g_kva_ref, g_head_ref) = refs[:9]
    refs = refs[9:]
    if rope:
        cos_ref, sin_ref = refs[:2]
        refs = refs[2:]
    if need_q:
        cqt_ref, dqt_ref = refs[:2]
        refs = refs[2:]
    ck_ref, cvt_ref, dk_ref, dvt_ref, wt_scr = refs
    _copy_once(wt_ref, wt_scr)

    h = _norm_modulate(x_ref[0], mod_ref[0, 3:4, :], mod_ref[0, 4:5, :])
    tm = h.shape[0]
    ht = h.T.astype(BF16)

    hd = HEAD_DIM
    n_cq = C_Q_RANK
    n_dq = D_HEADS * hd
    n_ckv = C_KV_RANK + C_ROPE
    g_qn = g_head_ref[0:64]
    g_qr = g_head_ref[64:96]
    g_kn = g_head_ref[96:160]
    g_kr = g_head_ref[160:192]
    g_dq = g_head_ref[192:256]
    g_dk = g_head_ref[256:320]
    if rope:
        cos = cos_ref[...]
        sin = sin_ref[...]
    zpad = jnp.zeros((C_PAD - C_NOPE - C_ROPE, tm), F32)

    def latent_queries(z):
        cqn = _rms_rows(z, g_qa_ref[...]).astype(BF16)
        qt = jnp.dot(wuq_ref[...], cqn, preferred_element_type=F32)
        cscale = (C_NOPE + C_ROPE) ** -0.5 * LOG2E
        for hh in range(C_HEADS):
            qn = _rms_rows(qt[hh * C_NOPE:(hh + 1) * C_NOPE], g_qn)
            rr = C_HEADS * C_NOPE + hh * C_ROPE
            qr = _rms_rows(qt[rr:rr + C_ROPE], g_qr)
            if rope:
                qr = _rope_rows(qr, cos, sin, C_ROPE // 4)
            qh = jnp.concatenate([qn, qr, zpad], axis=0) * cscale
            cqt_ref[0, hh * C_PAD:(hh + 1) * C_PAD, :] = qh.astype(BF16)

    def grid_queries(z):
        dscale = hd ** -0.5 * LOG2E
        for hh in range(D_HEADS):
            dqt_ref[0, hh * hd:(hh + 1) * hd, :] = (_rms_rows(z[hh * hd:(hh + 1) * hd], g_dq) * dscale).astype(BF16)

    def latent_keys_values(z):
        cn = _rms_rows(z[:C_KV_RANK], g_kva_ref[...]).astype(BF16)
        kr = _rms_rows(z[C_KV_RANK:], g_kr)
        if rope:
            kr = _rope_rows(kr, cos, sin, C_ROPE // 4)
        knt = jnp.dot(wuk_ref[...], cn, preferred_element_type=F32)
        for hh in range(C_HEADS):
            kn = _rms_rows(knt[hh * C_NOPE:(hh + 1) * C_NOPE], g_kn)
            _store_token_major(ck_ref, hh * C_PAD, [kn, kr, zpad])
        _store_values_t(cvt_ref, jnp.dot(wuv_ref[...], cn, preferred_element_type=F32), C_V)

    def grid_keys(z):
        for j in range(n_dq // LANES):
            rr = j * LANES
            _store_token_major(dk_ref, j * LANES, [_rms_rows(z[rr:rr + hd], g_dk),
                                                   _rms_rows(z[rr + hd:rr + 2 * hd], g_dk)])

    def grid_values(z):
        _store_values_t(dvt_ref, z, hd)

    groups = [(n_cq, latent_queries), (n_dq, grid_queries)] if need_q else []
    groups += [(n_ckv, latent_keys_values), (n_dq, grid_keys), (n_dq, grid_values)]
    _grouped_projection(wt_scr, ht, groups)


def _proj_cd(x, mods, mod_row, wt, wuq, wuk, wuv, g_qa, g_kva, g_head, rope_tabs, *, need_q):
    nb, t, d = x.shape
    tm = min(TOKEN_TILE, t)
    rope = rope_tabs is not None
    hd = HEAD_DIM
    n_c = C_HEADS * C_PAD
    n_d = D_HEADS * hd
    n_cv = C_HEADS * (C_V + ONES_ROWS)
    n_dv = D_HEADS * (hd + ONES_ROWS)
    tok = lambda b, i: (b, i, 0)
    ftok = lambda b, i: (b, 0, i)
    const = lambda b, i: (0, 0)
    consts = [wt, wuq, wuk, wuv, g_qa, g_kva, g_head]
    in_specs = [pl.BlockSpec((1, tm, d), tok),
                pl.BlockSpec((1, N_MOD, d), lambda b, i: (mod_row(b), 0, 0))]
    in_specs += [_resident(a.shape, const) for a in consts]
    args = [x, mods] + consts
    if rope:
        in_specs += [pl.BlockSpec((C_ROPE, tm), lambda b, i: (0, i))] * 2
        args += list(rope_tabs)
    out_shape, out_specs = [], []
    if need_q:
        out_shape += [jax.ShapeDtypeStruct((nb, n_c, t), BF16), jax.ShapeDtypeStruct((nb, n_d, t), BF16)]
        out_specs += [pl.BlockSpec((1, n_c, tm), ftok), pl.BlockSpec((1, n_d, tm), ftok)]
    out_shape += [jax.ShapeDtypeStruct((nb, t, n_c), BF16), jax.ShapeDtypeStruct((nb, n_cv, t), BF16),
                  jax.ShapeDtypeStruct((nb, t, n_d), BF16), jax.ShapeDtypeStruct((nb, n_dv, t), BF16)]
    out_specs += [pl.BlockSpec((1, tm, n_c), tok), pl.BlockSpec((1, n_cv, tm), ftok),
                  pl.BlockSpec((1, tm, n_d), tok), pl.BlockSpec((1, n_dv, tm), ftok)]
    return pl.pallas_call(
        functools.partial(_proj_cd_kernel, rope=rope, need_q=need_q),
        out_shape=out_shape,
        grid=(nb, t // tm),
        in_specs=in_specs,
        out_specs=out_specs,
        scratch_shapes=[pltpu.VMEM(wt.shape, BF16)],
        compiler_params=_params(),
        name="proj_cd",
    )(*args)


def _attend_units(n_units, n_chunks, rhs_of, score_chunk, value_chunk, s_ref, p_ref, extra=None):
    stats = {}
    outs = [None] * n_units
    tq = s_ref.shape[2]
    for t in range(n_units + 2):
        ua, ub, uc = t, t - 1, t - 2
        do_a, do_b, do_c = ua < n_units, 0 <= ub < n_units, 0 <= uc < n_units
        if do_a:
            rhs = rhs_of(ua)
            m8 = None
        acc = None
        for c in range(n_chunks):
            rows = slice(c * KEY_CHUNK, (c + 1) * KEY_CHUNK)
            if do_a:
                s = score_chunk(ua, c, rhs)
                s_ref[ua % 2, rows, :] = s
                pm = jnp.max(s.reshape(KEY_CHUNK // 8, 8, tq), axis=0)
                m8 = pm if m8 is None else jnp.maximum(m8, pm)
            if do_b:
                p_ref[ub % 2, rows, :] = jnp.exp2(s_ref[ub % 2, rows, :] - stats[ub]).astype(BF16)
            if do_c:
                part = jnp.dot(value_chunk(uc, c), p_ref[uc % 2, rows, :], preferred_element_type=F32)
                acc = part if acc is None else acc + part
        if do_a:
            m = jnp.max(m8, axis=0, keepdims=True)
            stats[ua] = m if extra is None else jnp.maximum(m, extra(ua))
        if do_c:
            dv = acc.shape[0] - ONES_ROWS
            total = acc[dv:dv + 1]
            if extra is not None:
                total = total + jnp.exp2(extra(uc) - stats[uc])
            outs[uc] = acc[:dv] * (1.0 / total)
    return outs


def _half_rhs(q_head, upper):
    zeros = jnp.zeros_like(q_head)
    return jnp.concatenate([zeros, q_head] if upper else [q_head, zeros], axis=0)


def _store_outputs(o_ref, outs, tok0=0):
    rows = jnp.concatenate(outs, axis=0)
    tq = rows.shape[1]
    for j in range(rows.shape[0] // LANES):
        o_ref[0, tok0:tok0 + tq, j * LANES:(j + 1) * LANES] = rows[j * LANES:(j + 1) * LANES].T.astype(BF16)


def _attn_scratch(n_keys, tq):
    return [pltpu.VMEM((2, n_keys, tq), F32), pltpu.VMEM((2, n_keys, tq), BF16)]


def _attn_a_kernel(*refs, latent, seq):
    if latent:
        (sink_ref, qt_ref, kp_ref, kc_ref, kn_ref, kx_ref, vp_ref, vc_ref, vn_ref, vx_ref,
         o_ref, s_ref, p_ref) = refs
    else:
        sink_ref, qt_ref, kx_ref, vx_ref, o_ref, s_ref, p_ref = refs
    qt = qt_ref[0]
    hd = HEAD_DIM
    tq = s_ref.shape[2]
    n_tiles = qt.shape[1] // tq
    group = A_HEADS // A_KV_HEADS
    kx = kx_ref[0]
    vx = vx_ref[0]
    lat_chunks = 0
    if latent:
        klat = jnp.concatenate([kp_ref[0], kc_ref[0], kn_ref[0]], axis=0)
        vlat = jnp.concatenate([vp_ref[0], vc_ref[0], vn_ref[0]], axis=1)
        n_lat = tq + 2 * A_WINDOW
        lat_chunks = n_lat // KEY_CHUNK
        krow = lax.broadcasted_iota(jnp.int32, (n_lat, tq), 0)
        qcol = lax.broadcasted_iota(jnp.int32, (n_lat, tq), 1)
        in_band = jnp.abs(krow - A_WINDOW - qcol) <= A_WINDOW
        biases = []
        for j in range(n_tiles):
            kpos = krow + (pl.program_id(1) * qt.shape[1] + j * tq - A_WINDOW)
            biases.append(jnp.where(in_band, jnp.where(kpos >= 0, jnp.where(kpos < seq, 0.0, NEG), NEG), NEG))
    n_chunks = lat_chunks + kx.shape[0] // KEY_CHUNK

    def rhs_of(u):
        j, hh = divmod(u, A_HEADS)
        return _half_rhs(qt[hh * hd:(hh + 1) * hd, j * tq:(j + 1) * tq], hh // group == 1)

    def score_chunk(u, c, rhs):
        j = u // A_HEADS
        if c < lat_chunks:
            r0 = j * tq + c * KEY_CHUNK
            return (jnp.dot(klat[r0:r0 + KEY_CHUNK], rhs, preferred_element_type=F32)
                    + biases[j][c * KEY_CHUNK:(c + 1) * KEY_CHUNK])
        r0 = (c - lat_chunks) * KEY_CHUNK
        return jnp.dot(kx[r0:r0 + KEY_CHUNK], rhs, preferred_element_type=F32)

    def value_chunk(u, c):
        j, hh = divmod(u, A_HEADS)
        g = hh // group
        rows = slice(g * (hd + ONES_ROWS), (g + 1) * (hd + ONES_ROWS))
        if c < lat_chunks:
            r0 = j * tq + c * KEY_CHUNK
            return vlat[rows, r0:r0 + KEY_CHUNK]
        r0 = (c - lat_chunks) * KEY_CHUNK
        return vx[rows, r0:r0 + KEY_CHUNK]

    outs = _attend_units(n_tiles * A_HEADS, n_chunks, rhs_of, score_chunk, value_chunk, s_ref, p_ref,
                         extra=lambda u: sink_ref[u % A_HEADS] * LOG2E)
    for j in range(n_tiles):
        _store_outputs(o_ref, outs[j * A_HEADS:(j + 1) * A_HEADS], j * tq)


def _attn_a(sink, qt, k, vt, kx, vxt, *, latent):
    nb, nq, t = qt.shape
    n_kv = A_KV_HEADS * HEAD_DIM
    n_v = vxt.shape[1]
    lx = kx.shape[1]
    smem = pl.BlockSpec(memory_space=pltpu.SMEM)
    if latent:
        tq = LOCAL_TILES * ATTN_Q_TILE
        seq = k.shape[1]
        w = A_WINDOW
        r = tq // w
        last = seq // w - 1
        prev = lambda i: jnp.maximum(i * r - 1, 0)
        nxt = lambda i: jnp.minimum((i + 1) * r, last)
        in_specs = [
            smem,
            pl.BlockSpec((1, nq, tq), lambda b, i: (b, 0, i)),
            pl.BlockSpec((1, w, n_kv), lambda b, i: (b, prev(i), 0)),
            pl.BlockSpec((1, tq, n_kv), lambda b, i: (b, i, 0)),
            pl.BlockSpec((1, w, n_kv), lambda b, i: (b, nxt(i), 0)),
            pl.BlockSpec((1, lx, n_kv), lambda b, i: (b, 0, 0)),
            pl.BlockSpec((1, n_v, w), lambda b, i: (b, 0, prev(i))),
            pl.BlockSpec((1, n_v, tq), lambda b, i: (b, 0, i)),
            pl.BlockSpec((1, n_v, w), lambda b, i: (b, 0, nxt(i))),
            pl.BlockSpec((1, n_v, lx), lambda b, i: (b, 0, 0)),
        ]
        args = [sink, qt, k, k, k, kx, vt, vt, vt, vxt]
        n_keys = ATTN_Q_TILE + 2 * w + lx
    else:
        tq = t
        seq = 0
        in_specs = [
            smem,
            pl.BlockSpec((1, nq, tq), lambda b, i: (b, 0, i)),
            pl.BlockSpec((1, lx, n_kv), lambda b, i: (b, 0, 0)),
            pl.BlockSpec((1, n_v, lx), lambda b, i: (b, 0, 0)),
        ]
        args = [sink, qt, kx, vxt]
        n_keys = lx
    return pl.pallas_call(
        functools.partial(_attn_a_kernel, latent=latent, seq=seq),
        out_shape=jax.ShapeDtypeStruct((nb, t, nq), BF16),
        grid=(nb, t // tq),
        in_specs=in_specs,
        out_specs=pl.BlockSpec((1, tq, nq), lambda b, i: (b, i, 0)),
        scratch_shapes=_attn_scratch(n_keys, min(ATTN_Q_TILE, tq)),
        compiler_params=_params(),
        name="attn_window",
    )(*args)


def _attn_b_kernel(*refs, latent, lam_init):
    if latent:
        lam_ref, gain_ref, qt_ref, k_ref, kx_ref, vt_ref, vxt_ref, o_ref, s_ref, p_ref = refs
    else:
        lam_ref, gain_ref, qt_ref, kx_ref, vxt_ref, o_ref, s_ref, p_ref = refs
    qt = qt_ref[0]
    hd = HEAD_DIM
    lv = lam_ref[...]
    lam = (jnp.exp(jnp.sum(lv[0:1] * lv[1:2], axis=-1, keepdims=True))
           - jnp.exp(jnp.sum(lv[2:3] * lv[3:4], axis=-1, keepdims=True)) + lam_init)
    gain = gain_ref[...] * (1.0 - lam_init)

    n_ctx = kx_ref.shape[1] // KEY_CHUNK

    tq = s_ref.shape[2]
    n_tiles = qt.shape[1] // tq
    n_maps = 2 * B_HEADS

    def rhs_of(u):
        j, mm = divmod(u, n_maps)
        return _half_rhs(qt[mm * hd:(mm + 1) * hd, j * tq:(j + 1) * tq], mm % 2 == 1)

    def score_chunk(u, c, rhs):
        hh = (u % n_maps) // 2
        cols = slice(hh * 2 * hd, (hh + 1) * 2 * hd)
        if c < n_ctx:
            k = kx_ref[0, c * KEY_CHUNK:(c + 1) * KEY_CHUNK, cols]
        else:
            k = k_ref[0, (c - n_ctx) * KEY_CHUNK:(c - n_ctx + 1) * KEY_CHUNK, cols]
        return jnp.dot(k, rhs, preferred_element_type=F32)

    def value_chunk(u, c):
        hh = (u % n_maps) // 2
        stride = B_V_DIM + ONES_ROWS
        rows = slice(hh * stride, (hh + 1) * stride)
        if c < n_ctx:
            return vxt_ref[0, rows, c * KEY_CHUNK:(c + 1) * KEY_CHUNK]
        return vt_ref[0, rows, (c - n_ctx) * KEY_CHUNK:(c - n_ctx + 1) * KEY_CHUNK]

    outs = _attend_units(n_tiles * n_maps, s_ref.shape[1] // KEY_CHUNK, rhs_of, score_chunk, value_chunk,
                         s_ref, p_ref)
    for j in range(n_tiles):
        heads = []
        for hh in range(B_HEADS):
            o = outs[j * n_maps + 2 * hh] - lam * outs[j * n_maps + 2 * hh + 1]
            heads.append(_rms_rows(o, gain))
        _store_outputs(o_ref, heads, j * tq)


def _attn_b(lam_vecs, sub_gain, qt, k, vt, kx, vxt, *, latent, lam_init):
    nb, nq, t = qt.shape
    lx, nk = kx.shape[1], kx.shape[2]
    nv = vxt.shape[1]
    tu = min(FULL_Q_TILE, t)
    tq = min(FULL_TILES * tu, t)
    const = lambda b, i: (0, 0)
    whole = lambda b, i: (b, 0, 0)
    in_specs = [pl.BlockSpec(lam_vecs.shape, const), pl.BlockSpec(sub_gain.shape, const),
                pl.BlockSpec((1, nq, tq), lambda b, i: (b, 0, i))]
    args = [lam_vecs, sub_gain, qt]
    n_keys = lx
    if latent:
        seq = k.shape[1]
        in_specs += [pl.BlockSpec((1, seq, nk), whole), pl.BlockSpec((1, lx, nk), whole),
                     pl.BlockSpec((1, nv, seq), whole), pl.BlockSpec((1, nv, lx), whole)]
        args += [k, kx, vt, vxt]
        n_keys += seq
    else:
        in_specs += [pl.BlockSpec((1, lx, nk), whole), pl.BlockSpec((1, nv, lx), whole)]
        args += [kx, vxt]
    return pl.pallas_call(
        functools.partial(_attn_b_kernel, latent=latent, lam_init=lam_init),
        out_shape=jax.ShapeDtypeStruct((nb, t, B_HEADS * B_V_DIM), BF16),
        grid=(nb, t // tq),
        in_specs=in_specs,
        out_specs=pl.BlockSpec((1, tq, B_HEADS * B_V_DIM), lambda b, i: (b, i, 0)),
        scratch_shapes=_attn_scratch(n_keys, tu),
        compiler_params=_params(),
        name="attn_diff",
    )(*args)


def _attn_c_kernel(qt_ref, k_ref, kx_ref, vt_ref, vxt_ref, o_ref, s_ref, p_ref):
    qt = qt_ref[0]

    n_ctx = kx_ref.shape[1] // KEY_CHUNK

    tq = s_ref.shape[2]
    n_tiles = qt.shape[1] // tq

    def rhs_of(u):
        j, hh = divmod(u, C_HEADS)
        return qt[hh * C_PAD:(hh + 1) * C_PAD, j * tq:(j + 1) * tq]

    def score_chunk(u, c, rhs):
        hh = u % C_HEADS
        cols = slice(hh * C_PAD, (hh + 1) * C_PAD)
        if c < n_ctx:
            k = kx_ref[0, c * KEY_CHUNK:(c + 1) * KEY_CHUNK, cols]
        else:
            k = k_ref[0, (c - n_ctx) * KEY_CHUNK:(c - n_ctx + 1) * KEY_CHUNK, cols]
        return jnp.dot(k, rhs, preferred_element_type=F32)

    def value_chunk(u, c):
        hh = u % C_HEADS
        rows = slice(hh * (C_V + ONES_ROWS), (hh + 1) * (C_V + ONES_ROWS))
        if c < n_ctx:
            return vxt_ref[0, rows, c * KEY_CHUNK:(c + 1) * KEY_CHUNK]
        return vt_ref[0, rows, (c - n_ctx) * KEY_CHUNK:(c - n_ctx + 1) * KEY_CHUNK]

    outs = _attend_units(n_tiles * C_HEADS, s_ref.shape[1] // KEY_CHUNK, rhs_of, score_chunk, value_chunk,
                         s_ref, p_ref)
    for j in range(n_tiles):
        _store_outputs(o_ref, outs[j * C_HEADS:(j + 1) * C_HEADS], j * tq)


def _attn_c(qt, k, vt, kx, vxt):
    nb, nq, t = qt.shape
    lx = kx.shape[1]
    nv = vt.shape[1]
    tq = FULL_TILES * FULL_Q_TILE
    whole = lambda b, i: (b, 0, 0)
    return pl.pallas_call(
        _attn_c_kernel,
        out_shape=jax.ShapeDtypeStruct((nb, t, C_HEADS * C_V), BF16),
        grid=(nb, t // tq),
        in_specs=[pl.BlockSpec((1, nq, tq), lambda b, i: (b, 0, i)),
                  pl.BlockSpec((1, t, nq), whole),
                  pl.BlockSpec((1, lx, nq), whole),
                  pl.BlockSpec((1, nv, t), whole),
                  pl.BlockSpec((1, nv, lx), whole)],
        out_specs=pl.BlockSpec((1, tq, C_HEADS * C_V), lambda b, i: (b, i, 0)),
        scratch_shapes=_attn_scratch(t + lx, FULL_Q_TILE),
        compiler_params=_params(),
        name="attn_latent",
    )(qt, k, kx, vt, vxt)


def _rpb_table_kernel(rpb_ref, o_ref, *, n_dr, n_dc):
    hh = pl.program_id(0)
    shape = (GRID_W, LANES)
    kc = lax.broadcasted_iota(jnp.int32, shape, 0)
    lane = lax.broadcasted_iota(jnp.int32, shape, 1)
    right = lane >= GRID_W
    qc = jnp.where(right, lane - GRID_W, lane)
    dc = jnp.clip(kc - qc, -(D_WIN_COLS - 1), D_WIN_COLS - 1) + (D_WIN_COLS - 1)
    cs = jnp.clip(qc - D_WIN_COLS // 2, 0, GRID_W - D_WIN_COLS)
    in_cols = jnp.where(kc >= cs, jnp.where(kc < cs + D_WIN_COLS, 0.0, NEG), NEG)
    for dd in range(n_dr + 1):
        base_l = (hh * n_dr + min(dd, n_dr - 1)) * n_dc
        base_r = (hh * n_dr + max(dd - 1, 0)) * n_dc
        acc = jnp.zeros(shape, F32)
        for c in range(n_dc):
            val = jnp.where(right, rpb_ref[base_r + c], rpb_ref[base_l + c])
            acc = jnp.where(dc == c, val, acc)
        ok = jnp.where(right, 0.0 if dd >= 1 else NEG, 0.0 if dd <= n_dr - 1 else NEG)
        o_ref[0, dd] = acc * LOG2E + ok + in_cols


def _rpb_table(rpb):
    n_h, n_dr, n_dc = rpb.shape
    return pl.pallas_call(
        functools.partial(_rpb_table_kernel, n_dr=n_dr, n_dc=n_dc),
        out_shape=jax.ShapeDtypeStruct((n_h, n_dr + 1, GRID_W, LANES), F32),
        grid=(n_h,),
        in_specs=[pl.BlockSpec(memory_space=pltpu.SMEM)],
        out_specs=pl.BlockSpec((1, n_dr + 1, GRID_W, LANES), lambda h: (h, 0, 0, 0)),
        name="rpb_table",
    )(rpb.reshape(-1))


def _attn_d_kernel(tab_ref, qt_ref, k_ref, kx_ref, vt_ref, vxt_ref, o_ref, s_ref, p_ref, *, rows):
    qt = qt_ref[0]
    hd = HEAD_DIM
    tq = s_ref.shape[2]
    n_tiles = qt.shape[1] // tq
    kh = min(D_WIN_ROWS, rows)
    nwin = NAT_WIN_ROWS * GRID_W
    lane = lax.broadcasted_iota(jnp.int32, (GRID_W, LANES), 1)
    right = lane >= GRID_W
    n_dd = tab_ref.shape[1]
    q_pairs = NAT_Q_ROWS // 2
    rows_per_chunk = KEY_CHUNK // GRID_W
    n_win = nwin // KEY_CHUNK

    def in_window(kr, qr):
        rs = jnp.clip(qr - kh // 2, 0, rows - kh)
        return jnp.where((kr >= rs) & (kr < rs + kh), 0.0, NEG)

    tiles = []
    for j in range(n_tiles):
        r_first = (pl.program_id(1) * n_tiles + j) * NAT_Q_ROWS
        ws = jnp.clip(r_first - kh // 2, 0, rows - kh)
        wl = jnp.minimum(ws, rows - NAT_WIN_ROWS)
        tok0 = pl.multiple_of(wl * GRID_W, LANES)
        row_add = jnp.concatenate([
            jnp.concatenate([jnp.where(right, in_window(wl + jr, r_first + 2 * a + 1),
                                       in_window(wl + jr, r_first + 2 * a))
                             for a in range(q_pairs)], axis=1)
            for jr in range(NAT_WIN_ROWS)], axis=0)
        tiles.append((r_first, wl, k_ref[0, pl.ds(tok0, nwin), :], vt_ref[0, :, pl.ds(tok0, nwin)], row_add))

    def rhs_of(u):
        j, hh = divmod(u, D_HEADS)
        return _half_rhs(qt[hh * hd:(hh + 1) * hd, j * tq:(j + 1) * tq], hh % 2 == 1)

    def score_chunk(u, c, rhs):
        j, hh = divmod(u, D_HEADS)
        cols = slice((hh // 2) * LANES, (hh // 2 + 1) * LANES)
        if c >= n_win:
            return jnp.dot(kx_ref[0, (c - n_win) * KEY_CHUNK:(c - n_win + 1) * KEY_CHUNK, cols], rhs,
                           preferred_element_type=F32)
        r_first, wl, kwin, _, row_add = tiles[j]
        rows_c = slice(c * KEY_CHUNK, (c + 1) * KEY_CHUNK)
        bias = jnp.concatenate([
            jnp.concatenate([tab_ref[hh, jnp.clip((wl + jr) - (r_first + 2 * a) + (D_WIN_ROWS - 1), 0, n_dd - 1)]
                             for a in range(q_pairs)], axis=1)
            for jr in range(c * rows_per_chunk, (c + 1) * rows_per_chunk)], axis=0)
        return jnp.dot(kwin[rows_c, cols], rhs, preferred_element_type=F32) + (bias + row_add[rows_c])

    def value_chunk(u, c):
        j, hh = divmod(u, D_HEADS)
        rows_u = slice(hh * (hd + ONES_ROWS), (hh + 1) * (hd + ONES_ROWS))
        if c >= n_win:
            return vxt_ref[0, rows_u, (c - n_win) * KEY_CHUNK:(c - n_win + 1) * KEY_CHUNK]
        return tiles[j][3][rows_u, c * KEY_CHUNK:(c + 1) * KEY_CHUNK]

    outs = _attend_units(n_tiles * D_HEADS, s_ref.shape[1] // KEY_CHUNK, rhs_of, score_chunk, value_chunk,
                         s_ref, p_ref)
    for j in range(n_tiles):
        _store_outputs(o_ref, outs[j * D_HEADS:(j + 1) * D_HEADS], j * tq)


def _attn_d(table, qt, k, vt, kx, vxt):
    nb, nq, t = qt.shape
    lx = kx.shape[1]
    tq = LOCAL_TILES * ATTN_Q_TILE
    rows = t // GRID_W
    assert rows >= NAT_WIN_ROWS and rows % NAT_Q_ROWS == 0 and NAT_Q_ROWS % 2 == 0
    whole = lambda b, i: (b, 0, 0)
    return pl.pallas_call(
        functools.partial(_attn_d_kernel, rows=rows),
        out_shape=jax.ShapeDtypeStruct((nb, t, nq), BF16),
        grid=(nb, t // tq),
        in_specs=[_resident(table.shape, lambda b, i: (0, 0, 0, 0)),
                  pl.BlockSpec((1, nq, tq), lambda b, i: (b, 0, i)),
                  pl.BlockSpec((1, t, nq), whole),
                  pl.BlockSpec((1, lx, nq), whole),
                  pl.BlockSpec((1, vt.shape[1], t), whole),
                  pl.BlockSpec((1, vt.shape[1], lx), whole)],
        out_specs=pl.BlockSpec((1, tq, nq), lambda b, i: (b, i, 0)),
        scratch_shapes=_attn_scratch(NAT_WIN_ROWS * GRID_W + lx, ATTN_Q_TILE),
        compiler_params=_params(),
        name="attn_neighbourhood",
    )(table, qt, k, kx, vt, vxt)


def _rope_tables_t(n, rot_dim):
    nf = rot_dim // 4
    inv = ROPE_BASE ** (-jnp.arange(nf, dtype=F32) / nf)
    t = jnp.arange(n)
    row = (t // GRID_W).astype(F32)
    col = (t % GRID_W).astype(F32)
    ang = jnp.concatenate([row[:, None] * inv, col[:, None] * inv], axis=-1)
    cos, sin = jnp.cos(ang).T, jnp.sin(ang).T
    cos_e = jnp.concatenate([cos[:nf], cos[:nf], cos[nf:], cos[nf:]], axis=0)
    sin_e = jnp.concatenate([-sin[:nf], sin[:nf], -sin[nf:], sin[nf:]], axis=0)
    return cos_e, sin_e


def _col(v):
    return v.astype(F32).reshape(-1, 1)


def kernel(x, c, ctx, c_ctx, w_mod, b_mod,
           ffn1_w_gate, ffn1_w_up, ffn1_w_down, ffn2_w_gate, ffn2_w_up, ffn2_w_down,
           ab_w_in, ab_w_out, a_q_norm, a_k_norm, a_sink, b_q_norm, b_k_norm,
           b_lambda_q1, b_lambda_k1, b_lambda_q2, b_lambda_k2, b_sub_norm,
           cd_w_in, cd_w_out, c_q_a_norm, c_kv_a_norm, c_w_uq, c_w_ukv,
           c_q_nope_norm, c_q_rope_norm, c_k_nope_norm, c_k_rope_norm,
           d_q_norm, d_k_norm, d_rpb):
    nb, seq, d = x.shape
    lctx = ctx.shape[1]
    depth = w_mod.shape[0]
    hd = HEAD_DIM

    n_rows = -(-(nb + 1) // 8) * 8
    c_rows = jnp.concatenate([c, c_ctx[None, :], jnp.zeros((n_rows - nb - 1, d), F32)], axis=0)
    mods_all = _mod_vectors(c_rows, w_mod, b_mod).reshape(depth, n_rows, N_MOD, d)
    x_row = lambda b: b
    ctx_row = lambda b: nb

    rope_head = _rope_tables_t(seq, hd)
    rope_mla = _rope_tables_t(seq, C_ROPE)

    w1 = tuple(_cast_bf16(w) for w in (ffn1_w_gate, ffn1_w_up, ffn1_w_down))
    w2 = tuple(_cast_bf16(w) for w in (ffn2_w_gate, ffn2_w_up, ffn2_w_down))
    ab_out = _cast_bf16(ab_w_out)
    cd_out = _cast_bf16(cd_w_out)

    xc = ctx.reshape(1, nb * lctx, d)
    for l in range(depth):
        need_ctx = l < depth - 1
        mods = mods_all[l]
        x = _ffn(x, mods, x_row, *w1, l, j0=0)
        xc = _ffn(xc, mods, ctx_row, *w1, l, j0=0)
        xc_b = xc.reshape(nb, lctx, d)
        i = l // 2
        if l % 2 == 0:
            lam_init = 0.8 - 0.6 * math.exp(-0.3 * l)
            w_in = ab_w_in[i]
            n_q = A_HEADS * hd + B_HEADS * 2 * hd
            n_ak = A_KV_HEADS * hd
            n_bk = B_HEADS * 2 * hd
            o_av = n_q + n_ak
            o_bk = o_av + n_ak
            o_bv = o_bk + n_bk
            wt = jnp.concatenate([w_in[:, :o_av], w_in[:, o_bk:o_bv], w_in[:, o_av:o_bk], w_in[:, o_bv:]],
                                 axis=1).T.astype(BF16)
            gains = jnp.concatenate([_col(a_q_norm[i]), _col(b_q_norm[i]),
                                     _col(a_k_norm[i]), _col(b_k_norm[i])], axis=0)
            aqt, bqt, ak, bk, avt, bvt = _proj_ab(x, mods, x_row, wt, gains, rope_head, need_q=True)
            ctx_out = _proj_ab(xc_b, mods, ctx_row, wt if need_ctx else wt[n_q:], gains, None, need_q=need_ctx)
            akx, bkx, avxt, bvxt = ctx_out[-4:]
            sink = a_sink[i].astype(F32)
            lam_vecs = jnp.stack([b_lambda_q1[i], b_lambda_k1[i], b_lambda_q2[i], b_lambda_k2[i]]).astype(F32)
            sub_gain = _col(b_sub_norm[i])
            y1 = _attn_a(sink, aqt, ak, avt, akx, avxt, latent=True)
            y2 = _attn_b(lam_vecs, sub_gain, bqt, bk, bvt, bkx, bvxt, latent=True, lam_init=lam_init)
            w_out = ab_out
            if need_ctx:
                aqxt, bqxt = ctx_out[:2]
                y1x = _attn_a(sink, aqxt, None, None, akx, avxt, latent=False)
                y2x = _attn_b(lam_vecs, sub_gain, bqxt, None, None, bkx, bvxt, latent=False, lam_init=lam_init)
        else:
            assert not need_ctx, "context queries of an odd layer are only needed for depth > 2"
            w_in = cd_w_in[i]
            n_q = C_Q_RANK + D_HEADS * hd
            wt = w_in.T.astype(BF16)
            uq = c_w_uq[i].reshape(C_Q_RANK, C_HEADS, C_NOPE + C_ROPE)
            wuq = jnp.concatenate([uq[:, :, :C_NOPE].reshape(C_Q_RANK, -1),
                                   uq[:, :, C_NOPE:].reshape(C_Q_RANK, -1)], axis=1).T.astype(BF16)
            ukv = c_w_ukv[i].reshape(C_KV_RANK, C_HEADS, C_NOPE + C_V)
            wuk = ukv[:, :, :C_NOPE].reshape(C_KV_RANK, -1).T.astype(BF16)
            wuv = ukv[:, :, C_NOPE:].reshape(C_KV_RANK, -1).T.astype(BF16)
            g_qa = _col(c_q_a_norm[i])
            g_kva = _col(c_kv_a_norm[i])
            g_head = jnp.concatenate([_col(c_q_nope_norm[i]), _col(c_q_rope_norm[i]),
                                      _col(c_k_nope_norm[i]), _col(c_k_rope_norm[i]),
                                      _col(d_q_norm[i]), _col(d_k_norm[i])], axis=0)
            cqt, dqt, ck, cvt, dk, dvt = _proj_cd(x, mods, x_row, wt, wuq, wuk, wuv,
                                                  g_qa, g_kva, g_head, rope_mla, need_q=True)
            ckx, cvxt, dkx, dvxt = _proj_cd(xc_b, mods, ctx_row, wt[n_q:], wuq, wuk, wuv,
                                            g_qa, g_kva, g_head, None, need_q=False)
            y1 = _attn_c(cqt, ck, cvt, ckx, cvxt)
            y2 = _attn_d(_rpb_table(d_rpb[i].astype(F32)), dqt, dk, dvt, dkx, dvxt)
            w_out = cd_out
        x = _ffn(x, mods, x_row, *w2, l, j0=6, y=(y1, y2), w_out=w_out, out_layer=i)
        if need_ctx:
            half = y1x.shape[-1]
            xc = _ffn(xc, mods, ctx_row, *w2, l, j0=6, w_out=w_out, out_layer=i,
                      y=(y1x.reshape(1, nb * lctx, half), y2x.reshape(1, nb * lctx, half)))
    return x
```

```python
import functools
import math

import jax
import jax.numpy as jnp
from jax import lax
from jax.experimental import pallas as pl
from jax.experimental.pallas import tpu as pltpu

F32 = jnp.float32
BF16 = jnp.bfloat16

D_MODEL = 1024
GRID_W = 64
HEAD_DIM = 64
D_FF = 2816
N_MOD = 9
ROPE_BASE = 10000.0
EPS = 1e-6
NEG = -1e30
LOG2E = math.log2(math.e)
A_HEADS = 8
A_KV_HEADS = 2
A_WINDOW = 128
B_HEADS = 4
B_V_DIM = 2 * HEAD_DIM
C_HEADS = 8
C_Q_RANK = 768
C_KV_RANK = 256
C_NOPE = 64
C_ROPE = 32
C_V = 64
C_PAD = 128
D_HEADS = 8
D_WIN_ROWS = 8
D_WIN_COLS = 16

LANES = 128
VMEM_LIMIT_BYTES = 56 * 1024 * 1024

CAST_BLOCK_BYTES = 3 * 1024 * 1024
FFN_CHUNK = 256
TOKEN_TILE = 512
FFN_TOKEN_TILE = 1024
ATTN_Q_TILE = 256
LOCAL_TILES = 4
FULL_Q_TILE = 512
FULL_TILES = 2
KEY_CHUNK = 256
ONES_ROWS = 16
NAT_Q_ROWS = ATTN_Q_TILE // GRID_W
NAT_WIN_ROWS = 12


def _params(**flags):
    return pltpu.CompilerParams(vmem_limit_bytes=VMEM_LIMIT_BYTES, flags=flags or None)


def _resident(block_shape, index_map):
    return pl.BlockSpec(block_shape, index_map, pipeline_mode=pl.Buffered(1))


def _sigmoid(x):
    return 1.0 / (1.0 + jnp.exp(-x))


def _rms_rows(z, gain):
    ms = jnp.mean(z * z, axis=0, keepdims=True)
    return z * lax.rsqrt(ms + EPS) * gain


def _norm_modulate(x, shift, scale):
    r = lax.rsqrt(jnp.mean(x * x, axis=-1, keepdims=True) + EPS)
    return (x * r) * (1.0 + scale) + shift


def _rope_rows(y, cos, sin, nf):
    part = jnp.concatenate([y[nf:2 * nf], y[0:nf], y[3 * nf:4 * nf], y[2 * nf:3 * nf]], axis=0)
    return y * cos + part * sin


def _mod_kernel(c_ref, w_ref, b_ref, o_ref):
    c = c_ref[...]
    a = (c * _sigmoid(c)).astype(BF16)
    o_ref[0] = jnp.dot(a, w_ref[0].astype(BF16), preferred_element_type=F32) + b_ref[0]


def _mod_vectors(c_rows, w_mod, b_mod):
    depth, d, n = w_mod.shape
    rows = c_rows.shape[0]
    tn = 1152
    return pl.pallas_call(
        _mod_kernel,
        out_shape=jax.ShapeDtypeStruct((depth, rows, n), F32),
        grid=(depth, n // tn),
        in_specs=[
            pl.BlockSpec((rows, d), lambda l, j: (0, 0)),
            pl.BlockSpec((1, d, tn), lambda l, j: (l, 0, j)),
            pl.BlockSpec((1, 1, tn), lambda l, j: (l, 0, j)),
        ],
        out_specs=pl.BlockSpec((1, rows, tn), lambda l, j: (l, 0, j)),
        compiler_params=_params(),
        name="mod_vectors",
    )(c_rows, w_mod, b_mod.reshape(depth, 1, n))


def _cast_kernel(w_ref, o_ref):
    o_ref[...] = w_ref[...].astype(BF16)


def _cast_bf16(w):
    depth, rows, cols = w.shape
    tr = next(rows // k for k in range(1, rows + 1)
              if rows % k == 0 and (rows // k) % 8 == 0 and (rows // k) * cols * 4 <= CAST_BLOCK_BYTES)
    spec = pl.BlockSpec((1, tr, cols), lambda l, i: (l, i, 0))
    return pl.pallas_call(
        _cast_kernel,
        out_shape=jax.ShapeDtypeStruct(w.shape, BF16),
        grid=(depth, rows // tr),
        in_specs=[spec],
        out_specs=spec,
        name="cast_bf16",
    )(w)


def _ffn_kernel(*refs, j0, fuse_out):
    if fuse_out:
        x_ref, y1_ref, y2_ref, mod_ref, wo_ref, wg_ref, wu_ref, wd_ref, o_ref, acc_ref = refs
    else:
        x_ref, mod_ref, wg_ref, wu_ref, wd_ref, o_ref, acc_ref = refs
    x = x_ref[0]
    if fuse_out:
        half = y1_ref.shape[-1]
        y = (jnp.dot(y1_ref[0], wo_ref[0, 0:half, :], preferred_element_type=F32)
             + jnp.dot(y2_ref[0], wo_ref[0, half:, :], preferred_element_type=F32))
        x = x + mod_ref[0, 5:6, :] * y
    shift = mod_ref[0, j0:j0 + 1, :]
    scale = mod_ref[0, j0 + 1:j0 + 2, :]
    gate = mod_ref[0, j0 + 2:j0 + 3, :]
    h = _norm_modulate(x, shift, scale).astype(BF16)
    d_ff = wg_ref.shape[2]
    for c in range(d_ff // FFN_CHUNK):
        lo, hi = c * FFN_CHUNK, (c + 1) * FFN_CHUNK
        g = jnp.dot(h, wg_ref[0, :, lo:hi], preferred_element_type=F32)
        u = jnp.dot(h, wu_ref[0, :, lo:hi], preferred_element_type=F32)
        a = (g * _sigmoid(g) * u).astype(BF16)
        part = jnp.dot(a, wd_ref[0, lo:hi, :], preferred_element_type=F32)
        if c == 0:
            acc_ref[...] = part
        else:
            acc_ref[...] += part
    o_ref[0] = x + (0.5 * gate) * acc_ref[...]


def _ffn(x, mods, mod_row, wg, wu, wd, layer, *, j0, y=None, w_out=None, out_layer=0):
    nb, t, d = x.shape
    tm = min(FFN_TOKEN_TILE if nb > 1 else TOKEN_TILE, t)
    d_ff = wg.shape[2]
    fuse_out = y is not None
    tok = lambda b, i: (b, i, 0)
    this_layer = lambda b, i: (layer, 0, 0)
    in_specs = [pl.BlockSpec((1, tm, d), tok)]
    args = [x]
    if fuse_out:
        half = y[0].shape[-1]
        in_specs += [pl.BlockSpec((1, tm, half), tok), pl.BlockSpec((1, tm, half), tok)]
        args += [y[0], y[1]]
    in_specs.append(pl.BlockSpec((1, N_MOD, d), lambda b, i: (mod_row(b), 0, 0)))
    args.append(mods)
    if fuse_out:
        in_specs.append(_resident((1,) + w_out.shape[1:], lambda b, i: (out_layer, 0, 0)))
        args.append(w_out)
    in_specs += [_resident((1, d, d_ff), this_layer), _resident((1, d, d_ff), this_layer),
                 _resident((1, d_ff, d), this_layer)]
    args += [wg, wu, wd]
    return pl.pallas_call(
        functools.partial(_ffn_kernel, j0=j0, fuse_out=fuse_out),
        out_shape=jax.ShapeDtypeStruct(x.shape, F32),
        grid=(nb, t // tm),
        in_specs=in_specs,
        out_specs=pl.BlockSpec((1, tm, d), tok),
        scratch_shapes=[pltpu.VMEM((tm, d), F32)],
        compiler_params=_params(),
        name="ffn_out" if fuse_out else "ffn",
    )(*args)


def _store_token_major(ref, col0, blocks):
    width = sum(b.shape[0] for b in blocks)
    ref[0, :, col0:col0 + width] = jnp.concatenate(blocks, axis=0).T.astype(BF16)


def _store_values_t(ref, zrows, dv):
    ones = jnp.ones((ONES_ROWS, zrows.shape[1]), F32)
    stride = dv + ONES_ROWS
    for hh in range(zrows.shape[0] // dv):
        block = jnp.concatenate([zrows[hh * dv:(hh + 1) * dv], ones], axis=0)
        ref[0, hh * stride:(hh + 1) * stride, :] = block.astype(BF16)


def _pipelined_ht(x0_ref, mod0_ref, xn_ref, modn_ref, ht_ref):
    step = pl.program_id(0) * pl.num_programs(1) + pl.program_id(1)
    slot = lax.rem(step, 2)

    def modulated_t(x_ref, mod_ref):
        h = _norm_modulate(x_ref[0], mod_ref[0, 3:4, :], mod_ref[0, 4:5, :])
        return h.T.astype(BF16)

    @pl.when(step == 0)
    def _():
        ht_ref[0] = modulated_t(x0_ref, mod0_ref)

    def prepare_next():
        ht_ref[1 - slot] = modulated_t(xn_ref, modn_ref)

    return ht_ref[slot], prepare_next


def _grouped_projection(wt_ref, ht, groups, after_first):
    r0, pending = 0, None
    for n, epilogue in groups:
        z = jnp.dot(wt_ref[r0:r0 + n, :], ht, preferred_element_type=F32)
        if pending is None:
            after_first()
        else:
            pending[0](pending[1])
        pending = (epilogue, z)
        r0 += n
    pending[0](pending[1])


def _next_tile_specs(nb, nt, tm, d, mod_row):
    def nxt(b, i):
        lin = jnp.minimum(b * nt + i + 1, nb * nt - 1)
        return lin // nt, lin % nt
    return [pl.BlockSpec((1, tm, d), lambda b, i: (0, 0, 0)),
            pl.BlockSpec((1, N_MOD, d), lambda b, i: (mod_row(0), 0, 0)),
            pl.BlockSpec((1, tm, d), lambda b, i: nxt(b, i) + (0,)),
            pl.BlockSpec((1, N_MOD, d), lambda b, i: (mod_row(nxt(b, i)[0]), 0, 0))]


def _proj_ab_kernel(*refs, rope, need_q):
    x0_ref, mod0_ref, xn_ref, modn_ref, wt_ref, gain_ref = refs[:6]
    refs = refs[6:]
    if rope:
        cos_ref, sin_ref = refs[:2]
        refs = refs[2:]
    if need_q:
        aqt_ref, bqt_ref = refs[:2]
        refs = refs[2:]
    ak_ref, bk_ref, avt_ref, bvt_ref, ht_ref = refs

    ht, prepare_next = _pipelined_ht(x0_ref, mod0_ref, xn_ref, modn_ref, ht_ref)
    hd = HEAD_DIM
    g_aq = gain_ref[0 * hd:1 * hd]
    g_bq = gain_ref[1 * hd:2 * hd]
    g_ak = gain_ref[2 * hd:3 * hd]
    g_bk = gain_ref[3 * hd:4 * hd]
    if rope:
        cos = cos_ref[...]
        sin = sin_ref[...]

    def head(z, row0, gain):
        y = _rms_rows(z[row0:row0 + hd], gain)
        if rope:
            y = _rope_rows(y, cos, sin, hd // 4)
        return y

    n_aq = A_HEADS * hd
    n_bq = B_HEADS * 2 * hd
    n_ak = A_KV_HEADS * hd
    qscale = hd ** -0.5 * LOG2E

    def queries(ref, gain):
        def epilogue(z):
            for j in range(z.shape[0] // hd):
                ref[0, j * hd:(j + 1) * hd, :] = (head(z, j * hd, gain) * qscale).astype(BF16)
        return epilogue

    def keys(z):
        _store_token_major(ak_ref, 0, [head(z, j * hd, g_ak) for j in range(A_KV_HEADS)])
        for j in range(n_bq // LANES):
            r0 = n_ak + j * LANES
            _store_token_major(bk_ref, j * LANES, [head(z, r0, g_bk), head(z, r0 + hd, g_bk)])

    def values(z):
        _store_values_t(avt_ref, z[:n_ak], hd)
        _store_values_t(bvt_ref, z[n_ak:], B_V_DIM)

    groups = [(n_aq, queries(aqt_ref, g_aq)), (n_bq, queries(bqt_ref, g_bq))] if need_q else []
    groups += [(n_ak + n_bq, keys), (wt_ref.shape[0] - sum(n for n, _ in groups) - n_ak - n_bq, values)]
    _grouped_projection(wt_ref, ht, groups, prepare_next)


def _proj_ab(x, mods, mod_row, wt, gains, rope_tabs, *, need_q):
    nb, t, d = x.shape
    tm = min(TOKEN_TILE, t)
    rope = rope_tabs is not None
    hd = HEAD_DIM
    n_aq, n_bq, n_ak = A_HEADS * hd, B_HEADS * 2 * hd, A_KV_HEADS * hd
    n_av, n_bv = A_KV_HEADS * (hd + ONES_ROWS), B_HEADS * (B_V_DIM + ONES_ROWS)
    tok = lambda b, i: (b, i, 0)
    ftok = lambda b, i: (b, 0, i)
    const = lambda b, i: (0, 0)
    in_specs = _next_tile_specs(nb, t // tm, tm, d, mod_row) + [_resident(wt.shape, const),
                                                                 _resident(gains.shape, const)]
    args = [x, mods, x, mods, wt, gains]
    if rope:
        in_specs += [pl.BlockSpec((hd, tm), lambda b, i: (0, i))] * 2
        args += list(rope_tabs)
    out_shape, out_specs = [], []
    if need_q:
        out_shape += [jax.ShapeDtypeStruct((nb, n_aq, t), BF16), jax.ShapeDtypeStruct((nb, n_bq, t), BF16)]
        out_specs += [pl.BlockSpec((1, n_aq, tm), ftok), pl.BlockSpec((1, n_bq, tm), ftok)]
    out_shape += [jax.ShapeDtypeStruct((nb, t, n_ak), BF16), jax.ShapeDtypeStruct((nb, t, n_bq), BF16),
                  jax.ShapeDtypeStruct((nb, n_av, t), BF16), jax.ShapeDtypeStruct((nb, n_bv, t), BF16)]
    out_specs += [pl.BlockSpec((1, tm, n_ak), tok), pl.BlockSpec((1, tm, n_bq), tok),
                  pl.BlockSpec((1, n_av, tm), ftok), pl.BlockSpec((1, n_bv, tm), ftok)]
    return pl.pallas_call(
        functools.partial(_proj_ab_kernel, rope=rope, need_q=need_q),
        out_shape=out_shape,
        grid=(nb, t // tm),
        in_specs=in_specs,
        out_specs=out_specs,
        scratch_shapes=[pltpu.VMEM((2, d, tm), BF16)],
        compiler_params=_params(),
        name="proj_ab",
    )(*args)


def _proj_cd_kernel(*refs, rope, need_q):
    (x0_ref, mod0_ref, xn_ref, modn_ref, wt_ref, wuq_ref, wuk_ref, wuv_ref,
     g_qa_ref, g_kva_ref, g_head_ref) = refs[:11]
    refs = refs[11:]
    if rope:
        cos_ref, sin_ref = refs[:2]
        refs = refs[2:]
    if need_q:
        cqt_ref, dqt_ref = refs[:2]
        refs = refs[2:]
    ck_ref, cvt_ref, dk_ref, dvt_ref, ht_ref = refs

    ht, prepare_next = _pipelined_ht(x0_ref, mod0_ref, xn_ref, modn_ref, ht_ref)
    tm = ht.shape[1]

    hd = HEAD_DIM
    n_cq = C_Q_RANK
    n_dq = D_HEADS * hd
    n_ckv = C_KV_RANK + C_ROPE
    g_qn = g_head_ref[0:64]
    g_qr = g_head_ref[64:96]
    g_kn = g_head_ref[96:160]
    g_kr = g_head_ref[160:192]
    g_dq = g_head_ref[192:256]
    g_dk = g_head_ref[256:320]
    if rope:
        cos = cos_ref[...]
        sin = sin_ref[...]
    zpad = jnp.zeros((C_PAD - C_NOPE - C_ROPE, tm), F32)

    def latent_queries(z):
        cqn = _rms_rows(z, g_qa_ref[...]).astype(BF16)
        qt = jnp.dot(wuq_ref[...], cqn, preferred_element_type=F32)
        cscale = (C_NOPE + C_ROPE) ** -0.5 * LOG2E
        for hh in range(C_HEADS):
            qn = _rms_rows(qt[hh * C_NOPE:(hh + 1) * C_NOPE], g_qn)
            rr = C_HEADS * C_NOPE + hh * C_ROPE
            qr = _rms_rows(qt[rr:rr + C_ROPE], g_qr)
            if rope:
                qr = _rope_rows(qr, cos, sin, C_ROPE // 4)
            qh = jnp.concatenate([qn, qr, zpad], axis=0) * cscale
            cqt_ref[0, hh * C_PAD:(hh + 1) * C_PAD, :] = qh.astype(BF16)

    def grid_queries(z):
        dscale = hd ** -0.5 * LOG2E
        for hh in range(D_HEADS):
            dqt_ref[0, hh * hd:(hh + 1) * hd, :] = (_rms_rows(z[hh * hd:(hh + 1) * hd], g_dq) * dscale).astype(BF16)

    def latent_keys_values(z):
        cn = _rms_rows(z[:C_KV_RANK], g_kva_ref[...]).astype(BF16)
        kr = _rms_rows(z[C_KV_RANK:], g_kr)
        if rope:
            kr = _rope_rows(kr, cos, sin, C_ROPE // 4)
        knt = jnp.dot(wuk_ref[...], cn, preferred_element_type=F32)
        for hh in range(C_HEADS):
            kn = _rms_rows(knt[hh * C_NOPE:(hh + 1) * C_NOPE], g_kn)
            _store_token_major(ck_ref, hh * C_PAD, [kn, kr, zpad])
        _store_values_t(cvt_ref, jnp.dot(wuv_ref[...], cn, preferred_element_type=F32), C_V)

    def grid_keys(z):
        for j in range(n_dq // LANES):
            rr = j * LANES
            _store_token_major(dk_ref, j * LANES, [_rms_rows(z[rr:rr + hd], g_dk),
                                                   _rms_rows(z[rr + hd:rr + 2 * hd], g_dk)])

    def grid_values(z):
        _store_values_t(dvt_ref, z, hd)

    groups = [(n_cq, latent_queries), (n_dq, grid_queries)] if need_q else []
    groups += [(n_ckv, latent_keys_values), (n_dq, grid_keys), (n_dq, grid_values)]
    _grouped_projection(wt_ref, ht, groups, prepare_next)


def _proj_cd(x, mods, mod_row, wt, wuq, wuk, wuv, g_qa, g_kva, g_head, rope_tabs, *, need_q):
    nb, t, d = x.shape
    tm = min(TOKEN_TILE, t)
    rope = rope_tabs is not None
    hd = HEAD_DIM
    n_c = C_HEADS * C_PAD
    n_d = D_HEADS * hd
    n_cv = C_HEADS * (C_V + ONES_ROWS)
    n_dv = D_HEADS * (hd + ONES_ROWS)
    tok = lambda b, i: (b, i, 0)
    ftok = lambda b, i: (b, 0, i)
    const = lambda b, i: (0, 0)
    consts = [wt, wuq, wuk, wuv, g_qa, g_kva, g_head]
    in_specs = _next_tile_specs(nb, t // tm, tm, d, mod_row) + [_resident(a.shape, const) for a in consts]
    args = [x, mods, x, mods] + consts
    if rope:
        in_specs += [pl.BlockSpec((C_ROPE, tm), lambda b, i: (0, i))] * 2
        args += list(rope_tabs)
    out_shape, out_specs = [], []
    if need_q:
        out_shape += [jax.ShapeDtypeStruct((nb, n_c, t), BF16), jax.ShapeDtypeStruct((nb, n_d, t), BF16)]
        out_specs += [pl.BlockSpec((1, n_c, tm), ftok), pl.BlockSpec((1, n_d, tm), ftok)]
    out_shape += [jax.ShapeDtypeStruct((nb, t, n_c), BF16), jax.ShapeDtypeStruct((nb, n_cv, t), BF16),
                  jax.ShapeDtypeStruct((nb, t, n_d), BF16), jax.ShapeDtypeStruct((nb, n_dv, t), BF16)]
    out_specs += [pl.BlockSpec((1, tm, n_c), tok), pl.BlockSpec((1, n_cv, tm), ftok),
                  pl.BlockSpec((1, tm, n_d), tok), pl.BlockSpec((1, n_dv, tm), ftok)]
    return pl.pallas_call(
        functools.partial(_proj_cd_kernel, rope=rope, need_q=need_q),
        out_shape=out_shape,
        grid=(nb, t // tm),
        in_specs=in_specs,
        out_specs=out_specs,
        scratch_shapes=[pltpu.VMEM((2, d, tm), BF16)],
        compiler_params=_params(),
        name="proj_cd",
    )(*args)


def _attend_units(n_units, n_chunks, rhs_of, score_chunk, value_chunk, s_ref, p_ref, extra=None):
    stats = {}
    outs = [None] * n_units
    tq = s_ref.shape[2]
    for t in range(n_units + 2):
        ua, ub, uc = t, t - 1, t - 2
        do_a, do_b, do_c = ua < n_units, 0 <= ub < n_units, 0 <= uc < n_units
        if do_a:
            rhs = rhs_of(ua)
            m8 = None
        acc = None
        for c in range(n_chunks):
            rows = slice(c * KEY_CHUNK, (c + 1) * KEY_CHUNK)
            if do_a:
                s = score_chunk(ua, c, rhs)
                s_ref[ua % 2, rows, :] = s
                pm = jnp.max(s.reshape(KEY_CHUNK // 8, 8, tq), axis=0)
                m8 = pm if m8 is None else jnp.maximum(m8, pm)
            if do_b:
                p_ref[ub % 2, rows, :] = jnp.exp2(s_ref[ub % 2, rows, :] - stats[ub]).astype(BF16)
            if do_c:
                part = jnp.dot(value_chunk(uc, c), p_ref[uc % 2, rows, :], preferred_element_type=F32)
                acc = part if acc is None else acc + part
        if do_a:
            m = jnp.max(m8, axis=0, keepdims=True)
            stats[ua] = m if extra is None else jnp.maximum(m, extra(ua))
        if do_c:
            dv = acc.shape[0] - ONES_ROWS
            total = acc[dv:dv + 1]
            if extra is not None:
                total = total + jnp.exp2(extra(uc) - stats[uc])
            outs[uc] = acc[:dv] * (1.0 / total)
    return outs


def _half_rhs(q_head, upper):
    zeros = jnp.zeros_like(q_head)
    return jnp.concatenate([zeros, q_head] if upper else [q_head, zeros], axis=0)


def _store_outputs(o_ref, outs, tok0=0):
    rows = jnp.concatenate(outs, axis=0)
    tq = rows.shape[1]
    for j in range(rows.shape[0] // LANES):
        o_ref[0, tok0:tok0 + tq, j * LANES:(j + 1) * LANES] = rows[j * LANES:(j + 1) * LANES].T.astype(BF16)


def _attn_scratch(n_keys, tq):
    return [pltpu.VMEM((2, n_keys, tq), F32), pltpu.VMEM((2, n_keys, tq), BF16)]


def _attn_a_kernel(*refs, latent, seq):
    if latent:
        (sink_ref, qt_ref, kp_ref, kc_ref, kn_ref, kx_ref, vp_ref, vc_ref, vn_ref, vx_ref,
         o_ref, s_ref, p_ref) = refs
    else:
        sink_ref, qt_ref, kx_ref, vx_ref, o_ref, s_ref, p_ref = refs
    qt = qt_ref[0]
    hd = HEAD_DIM
    tq = s_ref.shape[2]
    n_tiles = qt.shape[1] // tq
    group = A_HEADS // A_KV_HEADS
    kx = kx_ref[0]
    vx = vx_ref[0]
    lat_chunks = 0
    if latent:
        klat = jnp.concatenate([kp_ref[0], kc_ref[0], kn_ref[0]], axis=0)
        vlat = jnp.concatenate([vp_ref[0], vc_ref[0], vn_ref[0]], axis=1)
        n_lat = tq + 2 * A_WINDOW
        lat_chunks = n_lat // KEY_CHUNK
        krow = lax.broadcasted_iota(jnp.int32, (n_lat, tq), 0)
        qcol = lax.broadcasted_iota(jnp.int32, (n_lat, tq), 1)
        in_band = jnp.abs(krow - A_WINDOW - qcol) <= A_WINDOW
        biases = []
        for j in range(n_tiles):
            kpos = krow + (pl.program_id(1) * qt.shape[1] + j * tq - A_WINDOW)
            biases.append(jnp.where(in_band, jnp.where(kpos >= 0, jnp.where(kpos < seq, 0.0, NEG), NEG), NEG))
    n_chunks = lat_chunks + kx.shape[0] // KEY_CHUNK

    def rhs_of(u):
        j, hh = divmod(u, A_HEADS)
        return _half_rhs(qt[hh * hd:(hh + 1) * hd, j * tq:(j + 1) * tq], hh // group == 1)

    def score_chunk(u, c, rhs):
        j = u // A_HEADS
        if c < lat_chunks:
            r0 = j * tq + c * KEY_CHUNK
            return (jnp.dot(klat[r0:r0 + KEY_CHUNK], rhs, preferred_element_type=F32)
                    + biases[j][c * KEY_CHUNK:(c + 1) * KEY_CHUNK])
        r0 = (c - lat_chunks) * KEY_CHUNK
        return jnp.dot(kx[r0:r0 + KEY_CHUNK], rhs, preferred_element_type=F32)

    def value_chunk(u, c):
        j, hh = divmod(u, A_HEADS)
        g = hh // group
        rows = slice(g * (hd + ONES_ROWS), (g + 1) * (hd + ONES_ROWS))
        if c < lat_chunks:
            r0 = j * tq + c * KEY_CHUNK
            return vlat[rows, r0:r0 + KEY_CHUNK]
        r0 = (c - lat_chunks) * KEY_CHUNK
        return vx[rows, r0:r0 + KEY_CHUNK]

    outs = _attend_units(n_tiles * A_HEADS, n_chunks, rhs_of, score_chunk, value_chunk, s_ref, p_ref,
                         extra=lambda u: sink_ref[u % A_HEADS] * LOG2E)
    for j in range(n_tiles):
        _store_outputs(o_ref, outs[j * A_HEADS:(j + 1) * A_HEADS], j * tq)


def _attn_a(sink, qt, k, vt, kx, vxt, *, latent):
    nb, nq, t = qt.shape
    n_kv = A_KV_HEADS * HEAD_DIM
    n_v = vxt.shape[1]
    lx = kx.shape[1]
    smem = pl.BlockSpec(memory_space=pltpu.SMEM)
    if latent:
        tq = LOCAL_TILES * ATTN_Q_TILE
        seq = k.shape[1]
        w = A_WINDOW
        r = tq // w
        last = seq // w - 1
        prev = lambda i: jnp.maximum(i * r - 1, 0)
        nxt = lambda i: jnp.minimum((i + 1) * r, last)
        in_specs = [
            smem,
            pl.BlockSpec((1, nq, tq), lambda b, i: (b, 0, i)),
            pl.BlockSpec((1, w, n_kv), lambda b, i: (b, prev(i), 0)),
            pl.BlockSpec((1, tq, n_kv), lambda b, i: (b, i, 0)),
            pl.BlockSpec((1, w, n_kv), lambda b, i: (b, nxt(i), 0)),
            pl.BlockSpec((1, lx, n_kv), lambda b, i: (b, 0, 0)),
            pl.BlockSpec((1, n_v, w), lambda b, i: (b, 0, prev(i))),
            pl.BlockSpec((1, n_v, tq), lambda b, i: (b, 0, i)),
            pl.BlockSpec((1, n_v, w), lambda b, i: (b, 0, nxt(i))),
            pl.BlockSpec((1, n_v, lx), lambda b, i: (b, 0, 0)),
        ]
        args = [sink, qt, k, k, k, kx, vt, vt, vt, vxt]
        n_keys = ATTN_Q_TILE + 2 * w + lx
    else:
        tq = t
        seq = 0
        in_specs = [
            smem,
            pl.BlockSpec((1, nq, tq), lambda b, i: (b, 0, i)),
            pl.BlockSpec((1, lx, n_kv), lambda b, i: (b, 0, 0)),
            pl.BlockSpec((1, n_v, lx), lambda b, i: (b, 0, 0)),
        ]
        args = [sink, qt, kx, vxt]
        n_keys = lx
    return pl.pallas_call(
        functools.partial(_attn_a_kernel, latent=latent, seq=seq),
        out_shape=jax.ShapeDtypeStruct((nb, t, nq), BF16),
        grid=(nb, t // tq),
        in_specs=in_specs,
        out_specs=pl.BlockSpec((1, tq, nq), lambda b, i: (b, i, 0)),
        scratch_shapes=_attn_scratch(n_keys, min(ATTN_Q_TILE, tq)),
        compiler_params=_params(),
        name="attn_window",
    )(*args)


def _attn_b_kernel(*refs, latent, lam_init):
    if latent:
        lam_ref, gain_ref, qt_ref, k_ref, kx_ref, vt_ref, vxt_ref, o_ref, s_ref, p_ref = refs
    else:
        lam_ref, gain_ref, qt_ref, kx_ref, vxt_ref, o_ref, s_ref, p_ref = refs
    qt = qt_ref[0]
    hd = HEAD_DIM
    lv = lam_ref[...]
    lam = (jnp.exp(jnp.sum(lv[0:1] * lv[1:2], axis=-1, keepdims=True))
           - jnp.exp(jnp.sum(lv[2:3] * lv[3:4], axis=-1, keepdims=True)) + lam_init)
    gain = gain_ref[...] * (1.0 - lam_init)

    n_ctx = kx_ref.shape[1] // KEY_CHUNK

    tq = s_ref.shape[2]
    n_tiles = qt.shape[1] // tq
    n_maps = 2 * B_HEADS

    def rhs_of(u):
        j, mm = divmod(u, n_maps)
        return _half_rhs(qt[mm * hd:(mm + 1) * hd, j * tq:(j + 1) * tq], mm % 2 == 1)

    def score_chunk(u, c, rhs):
        hh = (u % n_maps) // 2
        cols = slice(hh * 2 * hd, (hh + 1) * 2 * hd)
        if c < n_ctx:
            k = kx_ref[0, c * KEY_CHUNK:(c + 1) * KEY_CHUNK, cols]
        else:
            k = k_ref[0, (c - n_ctx) * KEY_CHUNK:(c - n_ctx + 1) * KEY_CHUNK, cols]
        return jnp.dot(k, rhs, preferred_element_type=F32)

    def value_chunk(u, c):
        hh = (u % n_maps) // 2
        stride = B_V_DIM + ONES_ROWS
        rows = slice(hh * stride, (hh + 1) * stride)
        if c < n_ctx:
            return vxt_ref[0, rows, c * KEY_CHUNK:(c + 1) * KEY_CHUNK]
        return vt_ref[0, rows, (c - n_ctx) * KEY_CHUNK:(c - n_ctx + 1) * KEY_CHUNK]

    outs = _attend_units(n_tiles * n_maps, s_ref.shape[1] // KEY_CHUNK, rhs_of, score_chunk, value_chunk,
                         s_ref, p_ref)
    for j in range(n_tiles):
        heads = []
        for hh in range(B_HEADS):
            o = outs[j * n_maps + 2 * hh] - lam * outs[j * n_maps + 2 * hh + 1]
            heads.append(_rms_rows(o, gain))
        _store_outputs(o_ref, heads, j * tq)


def _attn_b(lam_vecs, sub_gain, qt, k, vt, kx, vxt, *, latent, lam_init):
    nb, nq, t = qt.shape
    lx, nk = kx.shape[1], kx.shape[2]
    nv = vxt.shape[1]
    tu = min(FULL_Q_TILE, t)
    tq = min(FULL_TILES * tu, t)
    const = lambda b, i: (0, 0)
    whole = lambda b, i: (b, 0, 0)
    in_specs = [pl.BlockSpec(lam_vecs.shape, const), pl.BlockSpec(sub_gain.shape, const),
                pl.BlockSpec((1, nq, tq), lambda b, i: (b, 0, i))]
    args = [lam_vecs, sub_gain, qt]
    n_keys = lx
    if latent:
        seq = k.shape[1]
        in_specs += [pl.BlockSpec((1, seq, nk), whole), pl.BlockSpec((1, lx, nk), whole),
                     pl.BlockSpec((1, nv, seq), whole), pl.BlockSpec((1, nv, lx), whole)]
        args += [k, kx, vt, vxt]
        n_keys += seq
    else:
        in_specs += [pl.BlockSpec((1, lx, nk), whole), pl.BlockSpec((1, nv, lx), whole)]
        args += [kx, vxt]
    return pl.pallas_call(
        functools.partial(_attn_b_kernel, latent=latent, lam_init=lam_init),
        out_shape=jax.ShapeDtypeStruct((nb, t, B_HEADS * B_V_DIM), BF16),
        grid=(nb, t // tq),
        in_specs=in_specs,
        out_specs=pl.BlockSpec((1, tq, B_HEADS * B_V_DIM), lambda b, i: (b, i, 0)),
        scratch_shapes=_attn_scratch(n_keys, tu),
        compiler_params=_params(),
        name="attn_diff",
    )(*args)


def _attn_c_kernel(qt_ref, k_ref, kx_ref, vt_ref, vxt_ref, o_ref, s_ref, p_ref):
    qt = qt_ref[0]

    n_ctx = kx_ref.shape[1] // KEY_CHUNK

    tq = s_ref.shape[2]
    n_tiles = qt.shape[1] // tq

    def rhs_of(u):
        j, hh = divmod(u, C_HEADS)
        return qt[hh * C_PAD:(hh + 1) * C_PAD, j * tq:(j + 1) * tq]

    def score_chunk(u, c, rhs):
        hh = u % C_HEADS
        cols = slice(hh * C_PAD, (hh + 1) * C_PAD)
        if c < n_ctx:
            k = kx_ref[0, c * KEY_CHUNK:(c + 1) * KEY_CHUNK, cols]
        else:
            k = k_ref[0, (c - n_ctx) * KEY_CHUNK:(c - n_ctx + 1) * KEY_CHUNK, cols]
        return jnp.dot(k, rhs, preferred_element_type=F32)

    def value_chunk(u, c):
        hh = u % C_HEADS
        rows = slice(hh * (C_V + ONES_ROWS), (hh + 1) * (C_V + ONES_ROWS))
        if c < n_ctx:
            return vxt_ref[0, rows, c * KEY_CHUNK:(c + 1) * KEY_CHUNK]
        return vt_ref[0, rows, (c - n_ctx) * KEY_CHUNK:(c - n_ctx + 1) * KEY_CHUNK]

    outs = _attend_units(n_tiles * C_HEADS, s_ref.shape[1] // KEY_CHUNK, rhs_of, score_chunk, value_chunk,
                         s_ref, p_ref)
    for j in range(n_tiles):
        _store_outputs(o_ref, outs[j * C_HEADS:(j + 1) * C_HEADS], j * tq)


def _attn_c(qt, k, vt, kx, vxt):
    nb, nq, t = qt.shape
    lx = kx.shape[1]
    nv = vt.shape[1]
    tq = FULL_TILES * FULL_Q_TILE
    whole = lambda b, i: (b, 0, 0)
    return pl.pallas_call(
        _attn_c_kernel,
        out_shape=jax.ShapeDtypeStruct((nb, t, C_HEADS * C_V), BF16),
        grid=(nb, t // tq),
        in_specs=[pl.BlockSpec((1, nq, tq), lambda b, i: (b, 0, i)),
                  pl.BlockSpec((1, t, nq), whole),
                  pl.BlockSpec((1, lx, nq), whole),
                  pl.BlockSpec((1, nv, t), whole),
                  pl.BlockSpec((1, nv, lx), whole)],
        out_specs=pl.BlockSpec((1, tq, C_HEADS * C_V), lambda b, i: (b, i, 0)),
        scratch_shapes=_attn_scratch(t + lx, FULL_Q_TILE),
        compiler_params=_params(),
        name="attn_latent",
    )(qt, k, kx, vt, vxt)


def _rpb_table_kernel(rpb_ref, o_ref, *, n_dr, n_dc):
    hh = pl.program_id(0)
    shape = (GRID_W, LANES)
    kc = lax.broadcasted_iota(jnp.int32, shape, 0)
    lane = lax.broadcasted_iota(jnp.int32, shape, 1)
    right = lane >= GRID_W
    qc = jnp.where(right, lane - GRID_W, lane)
    dc = jnp.clip(kc - qc, -(D_WIN_COLS - 1), D_WIN_COLS - 1) + (D_WIN_COLS - 1)
    cs = jnp.clip(qc - D_WIN_COLS // 2, 0, GRID_W - D_WIN_COLS)
    in_cols = jnp.where(kc >= cs, jnp.where(kc < cs + D_WIN_COLS, 0.0, NEG), NEG)
    for dd in range(n_dr + 1):
        base_l = (hh * n_dr + min(dd, n_dr - 1)) * n_dc
        base_r = (hh * n_dr + max(dd - 1, 0)) * n_dc
        acc = jnp.zeros(shape, F32)
        for c in range(n_dc):
            val = jnp.where(right, rpb_ref[base_r + c], rpb_ref[base_l + c])
            acc = jnp.where(dc == c, val, acc)
        ok = jnp.where(right, 0.0 if dd >= 1 else NEG, 0.0 if dd <= n_dr - 1 else NEG)
        o_ref[0, dd] = acc * LOG2E + ok + in_cols


def _rpb_table(rpb):
    n_h, n_dr, n_dc = rpb.shape
    return pl.pallas_call(
        functools.partial(_rpb_table_kernel, n_dr=n_dr, n_dc=n_dc),
        out_shape=jax.ShapeDtypeStruct((n_h, n_dr + 1, GRID_W, LANES), F32),
        grid=(n_h,),
        in_specs=[pl.BlockSpec(memory_space=pltpu.SMEM)],
        out_specs=pl.BlockSpec((1, n_dr + 1, GRID_W, LANES), lambda h: (h, 0, 0, 0)),
        name="rpb_table",
    )(rpb.reshape(-1))


def _attn_d_kernel(tab_ref, qt_ref, k_ref, kx_ref, vt_ref, vxt_ref, o_ref, s_ref, p_ref, *, rows):
    qt = qt_ref[0]
    hd = HEAD_DIM
    tq = s_ref.shape[2]
    n_tiles = qt.shape[1] // tq
    kh = min(D_WIN_ROWS, rows)
    nwin = NAT_WIN_ROWS * GRID_W
    lane = lax.broadcasted_iota(jnp.int32, (GRID_W, LANES), 1)
    right = lane >= GRID_W
    n_dd = tab_ref.shape[1]
    q_pairs = NAT_Q_ROWS // 2
    rows_per_chunk = KEY_CHUNK // GRID_W
    n_win = nwin // KEY_CHUNK

    def in_window(kr, qr):
        rs = jnp.clip(qr - kh // 2, 0, rows - kh)
        return jnp.where((kr >= rs) & (kr < rs + kh), 0.0, NEG)

    tiles = []
    for j in range(n_tiles):
        r_first = (pl.program_id(1) * n_tiles + j) * NAT_Q_ROWS
        ws = jnp.clip(r_first - kh // 2, 0, rows - kh)
        wl = jnp.minimum(ws, rows - NAT_WIN_ROWS)
        tok0 = pl.multiple_of(wl * GRID_W, LANES)
        row_add = jnp.concatenate([
            jnp.concatenate([jnp.where(right, in_window(wl + jr, r_first + 2 * a + 1),
                                       in_window(wl + jr, r_first + 2 * a))
                             for a in range(q_pairs)], axis=1)
            for jr in range(NAT_WIN_ROWS)], axis=0)
        tiles.append((r_first, wl, k_ref[0, pl.ds(tok0, nwin), :], vt_ref[0, :, pl.ds(tok0, nwin)], row_add))

    def rhs_of(u):
        j, hh = divmod(u, D_HEADS)
        return _half_rhs(qt[hh * hd:(hh + 1) * hd, j * tq:(j + 1) * tq], hh % 2 == 1)

    def score_chunk(u, c, rhs):
        j, hh = divmod(u, D_HEADS)
        cols = slice((hh // 2) * LANES, (hh // 2 + 1) * LANES)
        if c >= n_win:
            return jnp.dot(kx_ref[0, (c - n_win) * KEY_CHUNK:(c - n_win + 1) * KEY_CHUNK, cols], rhs,
                           preferred_element_type=F32)
        r_first, wl, kwin, _, row_add = tiles[j]
        rows_c = slice(c * KEY_CHUNK, (c + 1) * KEY_CHUNK)
        bias = jnp.concatenate([
            jnp.concatenate([tab_ref[hh, jnp.clip((wl + jr) - (r_first + 2 * a) + (D_WIN_ROWS - 1), 0, n_dd - 1)]
                             for a in range(q_pairs)], axis=1)
            for jr in range(c * rows_per_chunk, (c + 1) * rows_per_chunk)], axis=0)
        return jnp.dot(kwin[rows_c, cols], rhs, preferred_element_type=F32) + (bias + row_add[rows_c])

    def value_chunk(u, c):
        j, hh = divmod(u, D_HEADS)
        rows_u = slice(hh * (hd + ONES_ROWS), (hh + 1) * (hd + ONES_ROWS))
        if c >= n_win:
            return vxt_ref[0, rows_u, (c - n_win) * KEY_CHUNK:(c - n_win + 1) * KEY_CHUNK]
        return tiles[j][3][rows_u, c * KEY_CHUNK:(c + 1) * KEY_CHUNK]

    outs = _attend_units(n_tiles * D_HEADS, s_ref.shape[1] // KEY_CHUNK, rhs_of, score_chunk, value_chunk,
                         s_ref, p_ref)
    for j in range(n_tiles):
        _store_outputs(o_ref, outs[j * D_HEADS:(j + 1) * D_HEADS], j * tq)


def _attn_d(table, qt, k, vt, kx, vxt):
    nb, nq, t = qt.shape
    lx = kx.shape[1]
    tq = LOCAL_TILES * ATTN_Q_TILE
    rows = t // GRID_W
    assert rows >= NAT_WIN_ROWS and rows % NAT_Q_ROWS == 0 and NAT_Q_ROWS % 2 == 0
    whole = lambda b, i: (b, 0, 0)
    return pl.pallas_call(
        functools.partial(_attn_d_kernel, rows=rows),
        out_shape=jax.ShapeDtypeStruct((nb, t, nq), BF16),
        grid=(nb, t // tq),
        in_specs=[_resident(table.shape, lambda b, i: (0, 0, 0, 0)),
                  pl.BlockSpec((1, nq, tq), lambda b, i: (b, 0, i)),
                  pl.BlockSpec((1, t, nq), whole),
                  pl.BlockSpec((1, lx, nq), whole),
                  pl.BlockSpec((1, vt.shape[1], t), whole),
                  pl.BlockSpec((1, vt.shape[1], lx), whole)],
        out_specs=pl.BlockSpec((1, tq, nq), lambda b, i: (b, i, 0)),
        scratch_shapes=_attn_scratch(NAT_WIN_ROWS * GRID_W + lx, ATTN_Q_TILE),
        compiler_params=_params(),
        name="attn_neighbourhood",
    )(table, qt, k, kx, vt, vxt)


def _rope_tables_t(n, rot_dim):
    nf = rot_dim // 4
    inv = ROPE_BASE ** (-jnp.arange(nf, dtype=F32) / nf)
    t = jnp.arange(n)
    row = (t // GRID_W).astype(F32)
    col = (t % GRID_W).astype(F32)
    ang = jnp.concatenate([row[:, None] * inv, col[:, None] * inv], axis=-1)
    cos, sin = jnp.cos(ang).T, jnp.sin(ang).T
    cos_e = jnp.concatenate([cos[:nf], cos[:nf], cos[nf:], cos[nf:]], axis=0)
    sin_e = jnp.concatenate([-sin[:nf], sin[:nf], -sin[nf:], sin[nf:]], axis=0)
    return cos_e, sin_e


def _col(v):
    return v.astype(F32).reshape(-1, 1)


def kernel(x, c, ctx, c_ctx, w_mod, b_mod,
           ffn1_w_gate, ffn1_w_up, ffn1_w_down, ffn2_w_gate, ffn2_w_up, ffn2_w_down,
           ab_w_in, ab_w_out, a_q_norm, a_k_norm, a_sink, b_q_norm, b_k_norm,
           b_lambda_q1, b_lambda_k1, b_lambda_q2, b_lambda_k2, b_sub_norm,
           cd_w_in, cd_w_out, c_q_a_norm, c_kv_a_norm, c_w_uq, c_w_ukv,
           c_q_nope_norm, c_q_rope_norm, c_k_nope_norm, c_k_rope_norm,
           d_q_norm, d_k_norm, d_rpb):
    nb, seq, d = x.shape
    lctx = ctx.shape[1]
    depth = w_mod.shape[0]
    hd = HEAD_DIM

    n_rows = -(-(nb + 1) // 8) * 8
    c_rows = jnp.concatenate([c, c_ctx[None, :], jnp.zeros((n_rows - nb - 1, d), F32)], axis=0)
    mods_all = _mod_vectors(c_rows, w_mod, b_mod).reshape(depth, n_rows, N_MOD, d)
    x_row = lambda b: b
    ctx_row = lambda b: nb

    rope_head = _rope_tables_t(seq, hd)
    rope_mla = _rope_tables_t(seq, C_ROPE)

    w1 = tuple(_cast_bf16(w) for w in (ffn1_w_gate, ffn1_w_up, ffn1_w_down))
    w2 = tuple(_cast_bf16(w) for w in (ffn2_w_gate, ffn2_w_up, ffn2_w_down))
    ab_out = _cast_bf16(ab_w_out)
    cd_out = _cast_bf16(cd_w_out)

    xc = ctx.reshape(1, nb * lctx, d)
    for l in range(depth):
        need_ctx = l < depth - 1
        mods = mods_all[l]
        x = _ffn(x, mods, x_row, *w1, l, j0=0)
        xc = _ffn(xc, mods, ctx_row, *w1, l, j0=0)
        xc_b = xc.reshape(nb, lctx, d)
        i = l // 2
        if l % 2 == 0:
            lam_init = 0.8 - 0.6 * math.exp(-0.3 * l)
            w_in = ab_w_in[i]
            n_q = A_HEADS * hd + B_HEADS * 2 * hd
            n_ak = A_KV_HEADS * hd
            n_bk = B_HEADS * 2 * hd
            o_av = n_q + n_ak
            o_bk = o_av + n_ak
            o_bv = o_bk + n_bk
            wt = jnp.concatenate([w_in[:, :o_av], w_in[:, o_bk:o_bv], w_in[:, o_av:o_bk], w_in[:, o_bv:]],
                                 axis=1).T.astype(BF16)
            gains = jnp.concatenate([_col(a_q_norm[i]), _col(b_q_norm[i]),
                                     _col(a_k_norm[i]), _col(b_k_norm[i])], axis=0)
            aqt, bqt, ak, bk, avt, bvt = _proj_ab(x, mods, x_row, wt, gains, rope_head, need_q=True)
            ctx_out = _proj_ab(xc_b, mods, ctx_row, wt if need_ctx else wt[n_q:], gains, None, need_q=need_ctx)
            akx, bkx, avxt, bvxt = ctx_out[-4:]
            sink = a_sink[i].astype(F32)
            lam_vecs = jnp.stack([b_lambda_q1[i], b_lambda_k1[i], b_lambda_q2[i], b_lambda_k2[i]]).astype(F32)
            sub_gain = _col(b_sub_norm[i])
            y1 = _attn_a(sink, aqt, ak, avt, akx, avxt, latent=True)
            y2 = _attn_b(lam_vecs, sub_gain, bqt, bk, bvt, bkx, bvxt, latent=True, lam_init=lam_init)
            w_out = ab_out
            if need_ctx:
                aqxt, bqxt = ctx_out[:2]
                y1x = _attn_a(sink, aqxt, None, None, akx, avxt, latent=False)
                y2x = _attn_b(lam_vecs, sub_gain, bqxt, None, None, bkx, bvxt, latent=False, lam_init=lam_init)
        else:
            assert not need_ctx, "context queries of an odd layer are only needed for depth > 2"
            w_in = cd_w_in[i]
            n_q = C_Q_RANK + D_HEADS * hd
            wt = w_in.T.astype(BF16)
            uq = c_w_uq[i].reshape(C_Q_RANK, C_HEADS, C_NOPE + C_ROPE)
            wuq = jnp.concatenate([uq[:, :, :C_NOPE].reshape(C_Q_RANK, -1),
                                   uq[:, :, C_NOPE:].reshape(C_Q_RANK, -1)], axis=1).T.astype(BF16)
            ukv = c_w_ukv[i].reshape(C_KV_RANK, C_HEADS, C_NOPE + C_V)
            wuk = ukv[:, :, :C_NOPE].reshape(C_KV_RANK, -1).T.astype(BF16)
            wuv = ukv[:, :, C_NOPE:].reshape(C_KV_RANK, -1).T.astype(BF16)
            g_qa = _col(c_q_a_norm[i])
            g_kva = _col(c_kv_a_norm[i])
            g_head = jnp.concatenate([_col(c_q_nope_norm[i]), _col(c_q_rope_norm[i]),
                                      _col(c_k_nope_norm[i]), _col(c_k_rope_norm[i]),
                                      _col(d_q_norm[i]), _col(d_k_norm[i])], axis=0)
            cqt, dqt, ck, cvt, dk, dvt = _proj_cd(x, mods, x_row, wt, wuq, wuk, wuv,
                                                  g_qa, g_kva, g_head, rope_mla, need_q=True)
            ckx, cvxt, dkx, dvxt = _proj_cd(xc_b, mods, ctx_row, wt[n_q:], wuq, wuk, wuv,
                                            g_qa, g_kva, g_head, None, need_q=False)
            y1 = _attn_c(cqt, ck, cvt, ckx, cvxt)
            y2 = _attn_d(_rpb_table(d_rpb[i].astype(F32)), dqt, dk, dvt, dkx, dvxt)
            w_out = cd_out
        x = _ffn(x, mods, x_row, *w2, l, j0=6, y=(y1, y2), w_out=w_out, out_layer=i)
        if need_ctx:
            half = y1x.shape[-1]
            xc = _ffn(xc, mods, ctx_row, *w2, l, j0=6, w_out=w_out, out_layer=i,
                      y=(y1x.reshape(1, nb * lctx, half), y2x.reshape(1, nb * lctx, half)))
    return x
```

```python
import functools
import math

import jax
import jax.numpy as jnp
from jax import lax
from jax.experimental import pallas as pl
from jax.experimental.pallas import tpu as pltpu

F32 = jnp.float32
BF16 = jnp.bfloat16

D_MODEL = 1024
GRID_W = 64
HEAD_DIM = 64
D_FF = 2816
N_MOD = 9
ROPE_BASE = 10000.0
EPS = 1e-6
NEG = -1e30
LOG2E = math.log2(math.e)
A_HEADS = 8
A_KV_HEADS = 2
A_WINDOW = 128
B_HEADS = 4
B_V_DIM = 2 * HEAD_DIM
C_HEADS = 8
C_Q_RANK = 768
C_KV_RANK = 256
C_NOPE = 64
C_ROPE = 32
C_V = 64
C_PAD = 128
D_HEADS = 8
D_WIN_ROWS = 8
D_WIN_COLS = 16

LANES = 128
VMEM_LIMIT_BYTES = 56 * 1024 * 1024

CAST_BLOCK_BYTES = 3 * 1024 * 1024
FFN_CHUNK = 256
TOKEN_TILE = 1024
CTX_FFN_TILE = 512
FFN_TOKEN_TILE = 1024
ATTN_Q_TILE = 256
LOCAL_TILES = 4
FULL_Q_TILE = 512
FULL_TILES = 2
KEY_CHUNK = 256
ONES_ROWS = 16
NAT_Q_ROWS = ATTN_Q_TILE // GRID_W
NAT_WIN_ROWS = 12


def _params(**flags):
    return pltpu.CompilerParams(vmem_limit_bytes=VMEM_LIMIT_BYTES, flags=flags or None)


def _resident(block_shape, index_map):
    return pl.BlockSpec(block_shape, index_map, pipeline_mode=pl.Buffered(1))


def _sigmoid(x):
    return 1.0 / (1.0 + jnp.exp(-x))


def _rms_rows(z, gain):
    ms = jnp.mean(z * z, axis=0, keepdims=True)
    return z * lax.rsqrt(ms + EPS) * gain


def _norm_modulate(x, shift, scale):
    r = lax.rsqrt(jnp.mean(x * x, axis=-1, keepdims=True) + EPS)
    return (x * r) * (1.0 + scale) + shift


def _rope_rows(y, cos, sin, nf):
    part = jnp.concatenate([y[nf:2 * nf], y[0:nf], y[3 * nf:4 * nf], y[2 * nf:3 * nf]], axis=0)
    return y * cos + part * sin


def _mod_kernel(c_ref, w_ref, b_ref, o_ref):
    c = c_ref[...]
    a = (c * _sigmoid(c)).astype(BF16)
    o_ref[0] = jnp.dot(a, w_ref[0].astype(BF16), preferred_element_type=F32) + b_ref[0]


def _mod_vectors(c_rows, w_mod, b_mod):
    depth, d, n = w_mod.shape
    rows = c_rows.shape[0]
    tn = 1152
    return pl.pallas_call(
        _mod_kernel,
        out_shape=jax.ShapeDtypeStruct((depth, rows, n), F32),
        grid=(depth, n // tn),
        in_specs=[
            pl.BlockSpec((rows, d), lambda l, j: (0, 0)),
            pl.BlockSpec((1, d, tn), lambda l, j: (l, 0, j)),
            pl.BlockSpec((1, 1, tn), lambda l, j: (l, 0, j)),
        ],
        out_specs=pl.BlockSpec((1, rows, tn), lambda l, j: (l, 0, j)),
        compiler_params=_params(),
        name="mod_vectors",
    )(c_rows, w_mod, b_mod.reshape(depth, 1, n))


def _cast_kernel(w_ref, o_ref):
    o_ref[...] = w_ref[...].astype(BF16)


def _cast_bf16(w):
    depth, rows, cols = w.shape
    tr = next(rows // k for k in range(1, rows + 1)
              if rows % k == 0 and (rows // k) % 8 == 0 and (rows // k) * cols * 4 <= CAST_BLOCK_BYTES)
    spec = pl.BlockSpec((1, tr, cols), lambda l, i: (l, i, 0))
    return pl.pallas_call(
        _cast_kernel,
        out_shape=jax.ShapeDtypeStruct(w.shape, BF16),
        grid=(depth, rows // tr),
        in_specs=[spec],
        out_specs=spec,
        name="cast_bf16",
    )(w)


def _ffn_kernel(*refs, j0, fuse_out):
    if fuse_out:
        x_ref, y1_ref, y2_ref, mod_ref, wo_ref, wg_ref, wu_ref, wd_ref, o_ref, acc_ref = refs
    else:
        x_ref, mod_ref, wg_ref, wu_ref, wd_ref, o_ref, acc_ref = refs
    x = x_ref[0]
    if fuse_out:
        half = y1_ref.shape[-1]
        y = (jnp.dot(y1_ref[0], wo_ref[0, 0:half, :], preferred_element_type=F32)
             + jnp.dot(y2_ref[0], wo_ref[0, half:, :], preferred_element_type=F32))
        x = x + mod_ref[0, 5:6, :] * y
    shift = mod_ref[0, j0:j0 + 1, :]
    scale = mod_ref[0, j0 + 1:j0 + 2, :]
    gate = mod_ref[0, j0 + 2:j0 + 3, :]
    h = _norm_modulate(x, shift, scale).astype(BF16)
    d_ff = wg_ref.shape[2]
    for c in range(d_ff // FFN_CHUNK):
        lo, hi = c * FFN_CHUNK, (c + 1) * FFN_CHUNK
        g = jnp.dot(h, wg_ref[0, :, lo:hi], preferred_element_type=F32)
        u = jnp.dot(h, wu_ref[0, :, lo:hi], preferred_element_type=F32)
        a = (g * _sigmoid(g) * u).astype(BF16)
        part = jnp.dot(a, wd_ref[0, lo:hi, :], preferred_element_type=F32)
        if c == 0:
            acc_ref[...] = part
        else:
            acc_ref[...] += part
    o_ref[0] = x + (0.5 * gate) * acc_ref[...]


def _ffn(x, mods, mod_row, wg, wu, wd, layer, *, j0, y=None, w_out=None, out_layer=0):
    nb, t, d = x.shape
    tm = min(FFN_TOKEN_TILE if nb > 1 else CTX_FFN_TILE, t)
    d_ff = wg.shape[2]
    fuse_out = y is not None
    tok = lambda b, i: (b, i, 0)
    this_layer = lambda b, i: (layer, 0, 0)
    in_specs = [pl.BlockSpec((1, tm, d), tok)]
    args = [x]
    if fuse_out:
        half = y[0].shape[-1]
        in_specs += [pl.BlockSpec((1, tm, half), tok), pl.BlockSpec((1, tm, half), tok)]
        args += [y[0], y[1]]
    in_specs.append(pl.BlockSpec((1, N_MOD, d), lambda b, i: (mod_row(b), 0, 0)))
    args.append(mods)
    if fuse_out:
        in_specs.append(_resident((1,) + w_out.shape[1:], lambda b, i: (out_layer, 0, 0)))
        args.append(w_out)
    in_specs += [_resident((1, d, d_ff), this_layer), _resident((1, d, d_ff), this_layer),
                 _resident((1, d_ff, d), this_layer)]
    args += [wg, wu, wd]
    return pl.pallas_call(
        functools.partial(_ffn_kernel, j0=j0, fuse_out=fuse_out),
        out_shape=jax.ShapeDtypeStruct(x.shape, F32),
        grid=(nb, t // tm),
        in_specs=in_specs,
        out_specs=pl.BlockSpec((1, tm, d), tok),
        scratch_shapes=[pltpu.VMEM((tm, d), F32)],
        compiler_params=_params(),
        name="ffn_out" if fuse_out else "ffn",
    )(*args)


def _store_token_major(ref, col0, blocks):
    width = sum(b.shape[0] for b in blocks)
    ref[0, :, col0:col0 + width] = jnp.concatenate(blocks, axis=0).T.astype(BF16)


def _store_values_t(ref, zrows, dv):
    ones = jnp.ones((ONES_ROWS, zrows.shape[1]), F32)
    stride = dv + ONES_ROWS
    for hh in range(zrows.shape[0] // dv):
        block = jnp.concatenate([zrows[hh * dv:(hh + 1) * dv], ones], axis=0)
        ref[0, hh * stride:(hh + 1) * stride, :] = block.astype(BF16)


def _modulated_t(x_ref, mod_ref):
    h = _norm_modulate(x_ref[0], mod_ref[0, 3:4, :], mod_ref[0, 4:5, :])
    return h.T.astype(BF16)


def _grouped_projection(wt_ref, ht, groups):
    r0, pending = 0, None
    for n, epilogue in groups:
        z = jnp.dot(wt_ref[r0:r0 + n, :], ht, preferred_element_type=F32)
        if pending is not None:
            pending[0](pending[1])
        pending = (epilogue, z)
        r0 += n
    pending[0](pending[1])


def _proj_ab_kernel(*refs, rope, need_q):
    x_ref, mod_ref, wt_ref, gain_ref = refs[:4]
    refs = refs[4:]
    if rope:
        cos_ref, sin_ref = refs[:2]
        refs = refs[2:]
    if need_q:
        aqt_ref, bqt_ref = refs[:2]
        refs = refs[2:]
    ak_ref, bk_ref, avt_ref, bvt_ref = refs

    ht = _modulated_t(x_ref, mod_ref)
    hd = HEAD_DIM
    g_aq = gain_ref[0 * hd:1 * hd]
    g_bq = gain_ref[1 * hd:2 * hd]
    g_ak = gain_ref[2 * hd:3 * hd]
    g_bk = gain_ref[3 * hd:4 * hd]
    if rope:
        cos = cos_ref[...]
        sin = sin_ref[...]

    def head(z, row0, gain):
        y = _rms_rows(z[row0:row0 + hd], gain)
        if rope:
            y = _rope_rows(y, cos, sin, hd // 4)
        return y

    n_aq = A_HEADS * hd
    n_bq = B_HEADS * 2 * hd
    n_ak = A_KV_HEADS * hd
    qscale = hd ** -0.5 * LOG2E

    def queries(ref, gain):
        def epilogue(z):
            for j in range(z.shape[0] // hd):
                ref[0, j * hd:(j + 1) * hd, :] = (head(z, j * hd, gain) * qscale).astype(BF16)
        return epilogue

    def keys(z):
        _store_token_major(ak_ref, 0, [head(z, j * hd, g_ak) for j in range(A_KV_HEADS)])
        for j in range(n_bq // LANES):
            r0 = n_ak + j * LANES
            _store_token_major(bk_ref, j * LANES, [head(z, r0, g_bk), head(z, r0 + hd, g_bk)])

    def values(z):
        _store_values_t(avt_ref, z[:n_ak], hd)
        _store_values_t(bvt_ref, z[n_ak:], B_V_DIM)

    groups = [(n_aq, queries(aqt_ref, g_aq)), (n_bq, queries(bqt_ref, g_bq))] if need_q else []
    groups += [(n_ak + n_bq, keys), (wt_ref.shape[0] - sum(n for n, _ in groups) - n_ak - n_bq, values)]
    _grouped_projection(wt_ref, ht, groups)


def _proj_ab(x, mods, mod_row, wt, gains, rope_tabs, *, need_q):
    nb, t, d = x.shape
    tm = min(TOKEN_TILE, t)
    rope = rope_tabs is not None
    hd = HEAD_DIM
    n_aq, n_bq, n_ak = A_HEADS * hd, B_HEADS * 2 * hd, A_KV_HEADS * hd
    n_av, n_bv = A_KV_HEADS * (hd + ONES_ROWS), B_HEADS * (B_V_DIM + ONES_ROWS)
    tok = lambda b, i: (b, i, 0)
    ftok = lambda b, i: (b, 0, i)
    const = lambda b, i: (0, 0)
    in_specs = [pl.BlockSpec((1, tm, d), tok), pl.BlockSpec((1, N_MOD, d), lambda b, i: (mod_row(b), 0, 0)),
                _resident(wt.shape, const), _resident(gains.shape, const)]
    args = [x, mods, wt, gains]
    if rope:
        in_specs += [pl.BlockSpec((hd, tm), lambda b, i: (0, i))] * 2
        args += list(rope_tabs)
    out_shape, out_specs = [], []
    if need_q:
        out_shape += [jax.ShapeDtypeStruct((nb, n_aq, t), BF16), jax.ShapeDtypeStruct((nb, n_bq, t), BF16)]
        out_specs += [pl.BlockSpec((1, n_aq, tm), ftok), pl.BlockSpec((1, n_bq, tm), ftok)]
    out_shape += [jax.ShapeDtypeStruct((nb, t, n_ak), BF16), jax.ShapeDtypeStruct((nb, t, n_bq), BF16),
                  jax.ShapeDtypeStruct((nb, n_av, t), BF16), jax.ShapeDtypeStruct((nb, n_bv, t), BF16)]
    out_specs += [pl.BlockSpec((1, tm, n_ak), tok), pl.BlockSpec((1, tm, n_bq), tok),
                  pl.BlockSpec((1, n_av, tm), ftok), pl.BlockSpec((1, n_bv, tm), ftok)]
    return pl.pallas_call(
        functools.partial(_proj_ab_kernel, rope=rope, need_q=need_q),
        out_shape=out_shape,
        grid=(nb, t // tm),
        in_specs=in_specs,
        out_specs=out_specs,
        compiler_params=_params(),
        name="proj_ab",
    )(*args)


def _proj_cd_kernel(*refs, rope, need_q):
    (x_ref, mod_ref, wt_ref, wuq_ref, wuk_ref, wuv_ref, g_qa_ref, g_kva_ref, g_head_ref) = refs[:9]
    refs = refs[9:]
    if rope:
        cos_ref, sin_ref = refs[:2]
        refs = refs[2:]
    if need_q:
        cqt_ref, dqt_ref = refs[:2]
        refs = refs[2:]
    ck_ref, cvt_ref, dk_ref, dvt_ref = refs

    ht = _modulated_t(x_ref, mod_ref)
    tm = ht.shape[1]

    hd = HEAD_DIM
    n_cq = C_Q_RANK
    n_dq = D_HEADS * hd
    n_ckv = C_KV_RANK + C_ROPE
    g_qn = g_head_ref[0:64]
    g_qr = g_head_ref[64:96]
    g_kn = g_head_ref[96:160]
    g_kr = g_head_ref[160:192]
    g_dq = g_head_ref[192:256]
    g_dk = g_head_ref[256:320]
    if rope:
        cos = cos_ref[...]
        sin = sin_ref[...]
    zpad = jnp.zeros((C_PAD - C_NOPE - C_ROPE, tm), F32)

    def latent_queries(z):
        cqn = _rms_rows(z, g_qa_ref[...]).astype(BF16)
        qt = jnp.dot(wuq_ref[...], cqn, preferred_element_type=F32)
        cscale = (C_NOPE + C_ROPE) ** -0.5 * LOG2E
        for hh in range(C_HEADS):
            qn = _rms_rows(qt[hh * C_NOPE:(hh + 1) * C_NOPE], g_qn)
            rr = C_HEADS * C_NOPE + hh * C_ROPE
            qr = _rms_rows(qt[rr:rr + C_ROPE], g_qr)
            if rope:
                qr = _rope_rows(qr, cos, sin, C_ROPE // 4)
            qh = jnp.concatenate([qn, qr, zpad], axis=0) * cscale
            cqt_ref[0, hh * C_PAD:(hh + 1) * C_PAD, :] = qh.astype(BF16)

    def grid_queries(z):
        dscale = hd ** -0.5 * LOG2E
        for hh in range(D_HEADS):
            dqt_ref[0, hh * hd:(hh + 1) * hd, :] = (_rms_rows(z[hh * hd:(hh + 1) * hd], g_dq) * dscale).astype(BF16)

    def latent_keys_values(z):
        cn = _rms_rows(z[:C_KV_RANK], g_kva_ref[...]).astype(BF16)
        kr = _rms_rows(z[C_KV_RANK:], g_kr)
        if rope:
            kr = _rope_rows(kr, cos, sin, C_ROPE // 4)
        knt = jnp.dot(wuk_ref[...], cn, preferred_element_type=F32)
        for hh in range(C_HEADS):
            kn = _rms_rows(knt[hh * C_NOPE:(hh + 1) * C_NOPE], g_kn)
            _store_token_major(ck_ref, hh * C_PAD, [kn, kr, zpad])
        _store_values_t(cvt_ref, jnp.dot(wuv_ref[...], cn, preferred_element_type=F32), C_V)

    def grid_keys(z):
        for j in range(n_dq // LANES):
            rr = j * LANES
            _store_token_major(dk_ref, j * LANES, [_rms_rows(z[rr:rr + hd], g_dk),
                                                   _rms_rows(z[rr + hd:rr + 2 * hd], g_dk)])

    def grid_values(z):
        _store_values_t(dvt_ref, z, hd)

    groups = [(n_cq, latent_queries), (n_dq, grid_queries)] if need_q else []
    groups += [(n_ckv, latent_keys_values), (n_dq, grid_keys), (n_dq, grid_values)]
    _grouped_projection(wt_ref, ht, groups)


def _proj_cd(x, mods, mod_row, wt, wuq, wuk, wuv, g_qa, g_kva, g_head, rope_tabs, *, need_q):
    nb, t, d = x.shape
    tm = min(TOKEN_TILE, t)
    rope = rope_tabs is not None
    hd = HEAD_DIM
    n_c = C_HEADS * C_PAD
    n_d = D_HEADS * hd
    n_cv = C_HEADS * (C_V + ONES_ROWS)
    n_dv = D_HEADS * (hd + ONES_ROWS)
    tok = lambda b, i: (b, i, 0)
    ftok = lambda b, i: (b, 0, i)
    const = lambda b, i: (0, 0)
    consts = [wt, wuq, wuk, wuv, g_qa, g_kva, g_head]
    in_specs = [pl.BlockSpec((1, tm, d), tok), pl.BlockSpec((1, N_MOD, d), lambda b, i: (mod_row(b), 0, 0))]
    in_specs += [_resident(a.shape, const) for a in consts]
    args = [x, mods] + consts
    if rope:
        in_specs += [pl.BlockSpec((C_ROPE, tm), lambda b, i: (0, i))] * 2
        args += list(rope_tabs)
    out_shape, out_specs = [], []
    if need_q:
        out_shape += [jax.ShapeDtypeStruct((nb, n_c, t), BF16), jax.ShapeDtypeStruct((nb, n_d, t), BF16)]
        out_specs += [pl.BlockSpec((1, n_c, tm), ftok), pl.BlockSpec((1, n_d, tm), ftok)]
    out_shape += [jax.ShapeDtypeStruct((nb, t, n_c), BF16), jax.ShapeDtypeStruct((nb, n_cv, t), BF16),
                  jax.ShapeDtypeStruct((nb, t, n_d), BF16), jax.ShapeDtypeStruct((nb, n_dv, t), BF16)]
    out_specs += [pl.BlockSpec((1, tm, n_c), tok), pl.BlockSpec((1, n_cv, tm), ftok),
                  pl.BlockSpec((1, tm, n_d), tok), pl.BlockSpec((1, n_dv, tm), ftok)]
    return pl.pallas_call(
        functools.partial(_proj_cd_kernel, rope=rope, need_q=need_q),
        out_shape=out_shape,
        grid=(nb, t // tm),
        in_specs=in_specs,
        out_specs=out_specs,
        compiler_params=_params(),
        name="proj_cd",
    )(*args)


def _attend_units(n_units, n_chunks, rhs_of, score_chunk, value_chunk, s_ref, p_ref, extra=None):
    stats = {}
    outs = [None] * n_units
    tq = s_ref.shape[2]
    for t in range(n_units + 2):
        ua, ub, uc = t, t - 1, t - 2
        do_a, do_b, do_c = ua < n_units, 0 <= ub < n_units, 0 <= uc < n_units
        if do_a:
            rhs = rhs_of(ua)
            m8 = None
        acc = None
        for c in range(n_chunks):
            rows = slice(c * KEY_CHUNK, (c + 1) * KEY_CHUNK)
            if do_a:
                s = score_chunk(ua, c, rhs)
                s_ref[ua % 2, rows, :] = s
                pm = jnp.max(s.reshape(KEY_CHUNK // 8, 8, tq), axis=0)
                m8 = pm if m8 is None else jnp.maximum(m8, pm)
            if do_b:
                p_ref[ub % 2, rows, :] = jnp.exp2(s_ref[ub % 2, rows, :] - stats[ub]).astype(BF16)
            if do_c:
                part = jnp.dot(value_chunk(uc, c), p_ref[uc % 2, rows, :], preferred_element_type=F32)
                acc = part if acc is None else acc + part
        if do_a:
            m = jnp.max(m8, axis=0, keepdims=True)
            stats[ua] = m if extra is None else jnp.maximum(m, extra(ua))
        if do_c:
            dv = acc.shape[0] - ONES_ROWS
            total = acc[dv:dv + 1]
            if extra is not None:
                total = total + jnp.exp2(extra(uc) - stats[uc])
            outs[uc] = acc[:dv] * (1.0 / total)
    return outs


def _half_rhs(q_head, upper):
    zeros = jnp.zeros_like(q_head)
    return jnp.concatenate([zeros, q_head] if upper else [q_head, zeros], axis=0)


def _store_outputs(o_ref, outs, tok0=0):
    rows = jnp.concatenate(outs, axis=0)
    tq = rows.shape[1]
    for j in range(rows.shape[0] // LANES):
        o_ref[0, tok0:tok0 + tq, j * LANES:(j + 1) * LANES] = rows[j * LANES:(j + 1) * LANES].T.astype(BF16)


def _attn_scratch(n_keys, tq):
    return [pltpu.VMEM((2, n_keys, tq), F32), pltpu.VMEM((2, n_keys, tq), BF16)]


def _attn_a_kernel(*refs, latent, seq):
    if latent:
        (sink_ref, qt_ref, kp_ref, kc_ref, kn_ref, kx_ref, vp_ref, vc_ref, vn_ref, vx_ref,
         o_ref, s_ref, p_ref) = refs
    else:
        sink_ref, qt_ref, kx_ref, vx_ref, o_ref, s_ref, p_ref = refs
    qt = qt_ref[0]
    hd = HEAD_DIM
    tq = s_ref.shape[2]
    n_tiles = qt.shape[1] // tq
    group = A_HEADS // A_KV_HEADS
    kx = kx_ref[0]
    vx = vx_ref[0]
    lat_chunks = 0
    if latent:
        klat = jnp.concatenate([kp_ref[0], kc_ref[0], kn_ref[0]], axis=0)
        vlat = jnp.concatenate([vp_ref[0], vc_ref[0], vn_ref[0]], axis=1)
        n_lat = tq + 2 * A_WINDOW
        lat_chunks = n_lat // KEY_CHUNK
        krow = lax.broadcasted_iota(jnp.int32, (n_lat, tq), 0)
        qcol = lax.broadcasted_iota(jnp.int32, (n_lat, tq), 1)
        in_band = jnp.abs(krow - A_WINDOW - qcol) <= A_WINDOW
        biases = []
        for j in range(n_tiles):
            kpos = krow + (pl.program_id(1) * qt.shape[1] + j * tq - A_WINDOW)
            biases.append(jnp.where(in_band, jnp.where(kpos >= 0, jnp.where(kpos < seq, 0.0, NEG), NEG), NEG))
    n_chunks = lat_chunks + kx.shape[0] // KEY_CHUNK

    def rhs_of(u):
        j, hh = divmod(u, A_HEADS)
        return _half_rhs(qt[hh * hd:(hh + 1) * hd, j * tq:(j + 1) * tq], hh // group == 1)

    def score_chunk(u, c, rhs):
        j = u // A_HEADS
        if c < lat_chunks:
            r0 = j * tq + c * KEY_CHUNK
            return (jnp.dot(klat[r0:r0 + KEY_CHUNK], rhs, preferred_element_type=F32)
                    + biases[j][c * KEY_CHUNK:(c + 1) * KEY_CHUNK])
        r0 = (c - lat_chunks) * KEY_CHUNK
        return jnp.dot(kx[r0:r0 + KEY_CHUNK], rhs, preferred_element_type=F32)

    def value_chunk(u, c):
        j, hh = divmod(u, A_HEADS)
        g = hh // group
        rows = slice(g * (hd + ONES_ROWS), (g + 1) * (hd + ONES_ROWS))
        if c < lat_chunks:
            r0 = j * tq + c * KEY_CHUNK
            return vlat[rows, r0:r0 + KEY_CHUNK]
        r0 = (c - lat_chunks) * KEY_CHUNK
        return vx[rows, r0:r0 + KEY_CHUNK]

    outs = _attend_units(n_tiles * A_HEADS, n_chunks, rhs_of, score_chunk, value_chunk, s_ref, p_ref,
                         extra=lambda u: sink_ref[u % A_HEADS] * LOG2E)
    for j in range(n_tiles):
        _store_outputs(o_ref, outs[j * A_HEADS:(j + 1) * A_HEADS], j * tq)


def _attn_a(sink, qt, k, vt, kx, vxt, *, latent):
    nb, nq, t = qt.shape
    n_kv = A_KV_HEADS * HEAD_DIM
    n_v = vxt.shape[1]
    lx = kx.shape[1]
    smem = pl.BlockSpec(memory_space=pltpu.SMEM)
    if latent:
        tq = LOCAL_TILES * ATTN_Q_TILE
        seq = k.shape[1]
        w = A_WINDOW
        r = tq // w
        last = seq // w - 1
        prev = lambda i: jnp.maximum(i * r - 1, 0)
        nxt = lambda i: jnp.minimum((i + 1) * r, last)
        in_specs = [
            smem,
            pl.BlockSpec((1, nq, tq), lambda b, i: (b, 0, i)),
            pl.BlockSpec((1, w, n_kv), lambda b, i: (b, prev(i), 0)),
            pl.BlockSpec((1, tq, n_kv), lambda b, i: (b, i, 0)),
            pl.BlockSpec((1, w, n_kv), lambda b, i: (b, nxt(i), 0)),
            pl.BlockSpec((1, lx, n_kv), lambda b, i: (b, 0, 0)),
            pl.BlockSpec((1, n_v, w), lambda b, i: (b, 0, prev(i))),
            pl.BlockSpec((1, n_v, tq), lambda b, i: (b, 0, i)),
            pl.BlockSpec((1, n_v, w), lambda b, i: (b, 0, nxt(i))),
            pl.BlockSpec((1, n_v, lx), lambda b, i: (b, 0, 0)),
        ]
        args = [sink, qt, k, k, k, kx, vt, vt, vt, vxt]
        n_keys = ATTN_Q_TILE + 2 * w + lx
    else:
        tq = t
        seq = 0
        in_specs = [
            smem,
            pl.BlockSpec((1, nq, tq), lambda b, i: (b, 0, i)),
            pl.BlockSpec((1, lx, n_kv), lambda b, i: (b, 0, 0)),
            pl.BlockSpec((1, n_v, lx), lambda b, i: (b, 0, 0)),
        ]
        args = [sink, qt, kx, vxt]
        n_keys = lx
    return pl.pallas_call(
        functools.partial(_attn_a_kernel, latent=latent, seq=seq),
        out_shape=jax.ShapeDtypeStruct((nb, t, nq), BF16),
        grid=(nb, t // tq),
        in_specs=in_specs,
        out_specs=pl.BlockSpec((1, tq, nq), lambda b, i: (b, i, 0)),
        scratch_shapes=_attn_scratch(n_keys, min(ATTN_Q_TILE, tq)),
        compiler_params=_params(),
        name="attn_window",
    )(*args)


def _attn_b_kernel(*refs, latent, lam_init):
    if latent:
        lam_ref, gain_ref, qt_ref, k_ref, kx_ref, vt_ref, vxt_ref, o_ref, s_ref, p_ref = refs
    else:
        lam_ref, gain_ref, qt_ref, kx_ref, vxt_ref, o_ref, s_ref, p_ref = refs
    qt = qt_ref[0]
    hd = HEAD_DIM
    lv = lam_ref[...]
    lam = (jnp.exp(jnp.sum(lv[0:1] * lv[1:2], axis=-1, keepdims=True))
           - jnp.exp(jnp.sum(lv[2:3] * lv[3:4], axis=-1, keepdims=True)) + lam_init)
    gain = gain_ref[...] * (1.0 - lam_init)

    n_ctx = kx_ref.shape[1] // KEY_CHUNK

    tq = s_ref.shape[2]
    n_tiles = qt.shape[1] // tq
    n_maps = 2 * B_HEADS

    def rhs_of(u):
        j, mm = divmod(u, n_maps)
        return _half_rhs(qt[mm * hd:(mm + 1) * hd, j * tq:(j + 1) * tq], mm % 2 == 1)

    def score_chunk(u, c, rhs):
        hh = (u % n_maps) // 2
        cols = slice(hh * 2 * hd, (hh + 1) * 2 * hd)
        if c < n_ctx:
            k = kx_ref[0, c * KEY_CHUNK:(c + 1) * KEY_CHUNK, cols]
        else:
            k = k_ref[0, (c - n_ctx) * KEY_CHUNK:(c - n_ctx + 1) * KEY_CHUNK, cols]
        return jnp.dot(k, rhs, preferred_element_type=F32)

    def value_chunk(u, c):
        hh = (u % n_maps) // 2
        stride = B_V_DIM + ONES_ROWS
        rows = slice(hh * stride, (hh + 1) * stride)
        if c < n_ctx:
            return vxt_ref[0, rows, c * KEY_CHUNK:(c + 1) * KEY_CHUNK]
        return vt_ref[0, rows, (c - n_ctx) * KEY_CHUNK:(c - n_ctx + 1) * KEY_CHUNK]

    outs = _attend_units(n_tiles * n_maps, s_ref.shape[1] // KEY_CHUNK, rhs_of, score_chunk, value_chunk,
                         s_ref, p_ref)
    for j in range(n_tiles):
        heads = []
        for hh in range(B_HEADS):
            o = outs[j * n_maps + 2 * hh] - lam * outs[j * n_maps + 2 * hh + 1]
            heads.append(_rms_rows(o, gain))
        _store_outputs(o_ref, heads, j * tq)


def _attn_b(lam_vecs, sub_gain, qt, k, vt, kx, vxt, *, latent, lam_init):
    nb, nq, t = qt.shape
    lx, nk = kx.shape[1], kx.shape[2]
    nv = vxt.shape[1]
    tu = min(FULL_Q_TILE, t)
    tq = min(FULL_TILES * tu, t)
    const = lambda b, i: (0, 0)
    whole = lambda b, i: (b, 0, 0)
    in_specs = [pl.BlockSpec(lam_vecs.shape, const), pl.BlockSpec(sub_gain.shape, const),
                pl.BlockSpec((1, nq, tq), lambda b, i: (b, 0, i))]
    args = [lam_vecs, sub_gain, qt]
    n_keys = lx
    if latent:
        seq = k.shape[1]
        in_specs += [pl.BlockSpec((1, seq, nk), whole), pl.BlockSpec((1, lx, nk), whole),
                     pl.BlockSpec((1, nv, seq), whole), pl.BlockSpec((1, nv, lx), whole)]
        args += [k, kx, vt, vxt]
        n_keys += seq
    else:
        in_specs += [pl.BlockSpec((1, lx, nk), whole), pl.BlockSpec((1, nv, lx), whole)]
        args += [kx, vxt]
    return pl.pallas_call(
        functools.partial(_attn_b_kernel, latent=latent, lam_init=lam_init),
        out_shape=jax.ShapeDtypeStruct((nb, t, B_HEADS * B_V_DIM), BF16),
        grid=(nb, t // tq),
        in_specs=in_specs,
        out_specs=pl.BlockSpec((1, tq, B_HEADS * B_V_DIM), lambda b, i: (b, i, 0)),
        scratch_shapes=_attn_scratch(n_keys, tu),
        compiler_params=_params(),
        name="attn_diff",
    )(*args)


def _attn_c_kernel(qt_ref, k_ref, kx_ref, vt_ref, vxt_ref, o_ref, s_ref, p_ref):
    qt = qt_ref[0]

    n_ctx = kx_ref.shape[1] // KEY_CHUNK

    tq = s_ref.shape[2]
    n_tiles = qt.shape[1] // tq

    def rhs_of(u):
        j, hh = divmod(u, C_HEADS)
        return qt[hh * C_PAD:(hh + 1) * C_PAD, j * tq:(j + 1) * tq]

    def score_chunk(u, c, rhs):
        hh = u % C_HEADS
        cols = slice(hh * C_PAD, (hh + 1) * C_PAD)
        if c < n_ctx:
            k = kx_ref[0, c * KEY_CHUNK:(c + 1) * KEY_CHUNK, cols]
        else:
            k = k_ref[0, (c - n_ctx) * KEY_CHUNK:(c - n_ctx + 1) * KEY_CHUNK, cols]
        return jnp.dot(k, rhs, preferred_element_type=F32)

    def value_chunk(u, c):
        hh = u % C_HEADS
        rows = slice(hh * (C_V + ONES_ROWS), (hh + 1) * (C_V + ONES_ROWS))
        if c < n_ctx:
            return vxt_ref[0, rows, c * KEY_CHUNK:(c + 1) * KEY_CHUNK]
        return vt_ref[0, rows, (c - n_ctx) * KEY_CHUNK:(c - n_ctx + 1) * KEY_CHUNK]

    outs = _attend_units(n_tiles * C_HEADS, s_ref.shape[1] // KEY_CHUNK, rhs_of, score_chunk, value_chunk,
                         s_ref, p_ref)
    for j in range(n_tiles):
        _store_outputs(o_ref, outs[j * C_HEADS:(j + 1) * C_HEADS], j * tq)


def _attn_c(qt, k, vt, kx, vxt):
    nb, nq, t = qt.shape
    lx = kx.shape[1]
    nv = vt.shape[1]
    tq = FULL_TILES * FULL_Q_TILE
    whole = lambda b, i: (b, 0, 0)
    return pl.pallas_call(
        _attn_c_kernel,
        out_shape=jax.ShapeDtypeStruct((nb, t, C_HEADS * C_V), BF16),
        grid=(nb, t // tq),
        in_specs=[pl.BlockSpec((1, nq, tq), lambda b, i: (b, 0, i)),
                  pl.BlockSpec((1, t, nq), whole),
                  pl.BlockSpec((1, lx, nq), whole),
                  pl.BlockSpec((1, nv, t), whole),
                  pl.BlockSpec((1, nv, lx), whole)],
        out_specs=pl.BlockSpec((1, tq, C_HEADS * C_V), lambda b, i: (b, i, 0)),
        scratch_shapes=_attn_scratch(t + lx, FULL_Q_TILE),
        compiler_params=_params(),
        name="attn_latent",
    )(qt, k, kx, vt, vxt)


def _rpb_table_kernel(rpb_ref, o_ref, *, n_dr, n_dc):
    hh = pl.program_id(0)
    shape = (GRID_W, LANES)
    kc = lax.broadcasted_iota(jnp.int32, shape, 0)
    lane = lax.broadcasted_iota(jnp.int32, shape, 1)
    right = lane >= GRID_W
    qc = jnp.where(right, lane - GRID_W, lane)
    dc = jnp.clip(kc - qc, -(D_WIN_COLS - 1), D_WIN_COLS - 1) + (D_WIN_COLS - 1)
    cs = jnp.clip(qc - D_WIN_COLS // 2, 0, GRID_W - D_WIN_COLS)
    in_cols = jnp.where(kc >= cs, jnp.where(kc < cs + D_WIN_COLS, 0.0, NEG), NEG)
    for dd in range(n_dr + 1):
        base_l = (hh * n_dr + min(dd, n_dr - 1)) * n_dc
        base_r = (hh * n_dr + max(dd - 1, 0)) * n_dc
        acc = jnp.zeros(shape, F32)
        for c in range(n_dc):
            val = jnp.where(right, rpb_ref[base_r + c], rpb_ref[base_l + c])
            acc = jnp.where(dc == c, val, acc)
        ok = jnp.where(right, 0.0 if dd >= 1 else NEG, 0.0 if dd <= n_dr - 1 else NEG)
        o_ref[0, dd] = acc * LOG2E + ok + in_cols


def _rpb_table(rpb):
    n_h, n_dr, n_dc = rpb.shape
    return pl.pallas_call(
        functools.partial(_rpb_table_kernel, n_dr=n_dr, n_dc=n_dc),
        out_shape=jax.ShapeDtypeStruct((n_h, n_dr + 1, GRID_W, LANES), F32),
        grid=(n_h,),
        in_specs=[pl.BlockSpec(memory_space=pltpu.SMEM)],
        out_specs=pl.BlockSpec((1, n_dr + 1, GRID_W, LANES), lambda h: (h, 0, 0, 0)),
        name="rpb_table",
    )(rpb.reshape(-1))


def _attn_d_kernel(tab_ref, qt_ref, k_ref, kx_ref, vt_ref, vxt_ref, o_ref, s_ref, p_ref, *, rows):
    qt = qt_ref[0]
    hd = HEAD_DIM
    tq = s_ref.shape[2]
    n_tiles = qt.shape[1] // tq
    kh = min(D_WIN_ROWS, rows)
    nwin = NAT_WIN_ROWS * GRID_W
    lane = lax.broadcasted_iota(jnp.int32, (GRID_W, LANES), 1)
    right = lane >= GRID_W
    n_dd = tab_ref.shape[1]
    q_pairs = NAT_Q_ROWS // 2
    rows_per_chunk = KEY_CHUNK // GRID_W
    n_win = nwin // KEY_CHUNK

    def in_window(kr, qr):
        rs = jnp.clip(qr - kh // 2, 0, rows - kh)
        return jnp.where((kr >= rs) & (kr < rs + kh), 0.0, NEG)

    tiles = []
    for j in range(n_tiles):
        r_first = (pl.program_id(1) * n_tiles + j) * NAT_Q_ROWS
        ws = jnp.clip(r_first - kh // 2, 0, rows - kh)
        wl = jnp.minimum(ws, rows - NAT_WIN_ROWS)
        tok0 = pl.multiple_of(wl * GRID_W, LANES)
        row_add = jnp.concatenate([
            jnp.concatenate([jnp.where(right, in_window(wl + jr, r_first + 2 * a + 1),
                                       in_window(wl + jr, r_first + 2 * a))
                             for a in range(q_pairs)], axis=1)
            for jr in range(NAT_WIN_ROWS)], axis=0)
        tiles.append((r_first, wl, k_ref[0, pl.ds(tok0, nwin), :], vt_ref[0, :, pl.ds(tok0, nwin)], row_add))

    def rhs_of(u):
        j, hh = divmod(u, D_HEADS)
        return _half_rhs(qt[hh * hd:(hh + 1) * hd, j * tq:(j + 1) * tq], hh % 2 == 1)

    def score_chunk(u, c, rhs):
        j, hh = divmod(u, D_HEADS)
        cols = slice((hh // 2) * LANES, (hh // 2 + 1) * LANES)
        if c >= n_win:
            return jnp.dot(kx_ref[0, (c - n_win) * KEY_CHUNK:(c - n_win + 1) * KEY_CHUNK, cols], rhs,
                           preferred_element_type=F32)
        r_first, wl, kwin, _, row_add = tiles[j]
        rows_c = slice(c * KEY_CHUNK, (c + 1) * KEY_CHUNK)
        bias = jnp.concatenate([
            jnp.concatenate([tab_ref[hh, jnp.clip((wl + jr) - (r_first + 2 * a) + (D_WIN_ROWS - 1), 0, n_dd - 1)]
                             for a in range(q_pairs)], axis=1)
            for jr in range(c * rows_per_chunk, (c + 1) * rows_per_chunk)], axis=0)
        return jnp.dot(kwin[rows_c, cols], rhs, preferred_element_type=F32) + (bias + row_add[rows_c])

    def value_chunk(u, c):
        j, hh = divmod(u, D_HEADS)
        rows_u = slice(hh * (hd + ONES_ROWS), (hh + 1) * (hd + ONES_ROWS))
        if c >= n_win:
            return vxt_ref[0, rows_u, (c - n_win) * KEY_CHUNK:(c - n_win + 1) * KEY_CHUNK]
        return tiles[j][3][rows_u, c * KEY_CHUNK:(c + 1) * KEY_CHUNK]

    outs = _attend_units(n_tiles * D_HEADS, s_ref.shape[1] // KEY_CHUNK, rhs_of, score_chunk, value_chunk,
                         s_ref, p_ref)
    for j in range(n_tiles):
        _store_outputs(o_ref, outs[j * D_HEADS:(j + 1) * D_HEADS], j * tq)


def _attn_d(table, qt, k, vt, kx, vxt):
    nb, nq, t = qt.shape
    lx = kx.shape[1]
    tq = LOCAL_TILES * ATTN_Q_TILE
    rows = t // GRID_W
    assert rows >= NAT_WIN_ROWS and rows % NAT_Q_ROWS == 0 and NAT_Q_ROWS % 2 == 0
    whole = lambda b, i: (b, 0, 0)
    return pl.pallas_call(
        functools.partial(_attn_d_kernel, rows=rows),
        out_shape=jax.ShapeDtypeStruct((nb, t, nq), BF16),
        grid=(nb, t // tq),
        in_specs=[_resident(table.shape, lambda b, i: (0, 0, 0, 0)),
                  pl.BlockSpec((1, nq, tq), lambda b, i: (b, 0, i)),
                  pl.BlockSpec((1, t, nq), whole),
                  pl.BlockSpec((1, lx, nq), whole),
                  pl.BlockSpec((1, vt.shape[1], t), whole),
                  pl.BlockSpec((1, vt.shape[1], lx), whole)],
        out_specs=pl.BlockSpec((1, tq, nq), lambda b, i: (b, i, 0)),
        scratch_shapes=_attn_scratch(NAT_WIN_ROWS * GRID_W + lx, ATTN_Q_TILE),
        compiler_params=_params(),
        name="attn_neighbourhood",
    )(table, qt, k, kx, vt, vxt)


def _rope_tables_t(n, rot_dim):
    nf = rot_dim // 4
    inv = ROPE_BASE ** (-jnp.arange(nf, dtype=F32) / nf)
    t = jnp.arange(n)
    row = (t // GRID_W).astype(F32)
    col = (t % GRID_W).astype(F32)
    ang = jnp.concatenate([row[:, None] * inv, col[:, None] * inv], axis=-1)
    cos, sin = jnp.cos(ang).T, jnp.sin(ang).T
    cos_e = jnp.concatenate([cos[:nf], cos[:nf], cos[nf:], cos[nf:]], axis=0)
    sin_e = jnp.concatenate([-sin[:nf], sin[:nf], -sin[nf:], sin[nf:]], axis=0)
    return cos_e, sin_e


def _col(v):
    return v.astype(F32).reshape(-1, 1)


def kernel(x, c, ctx, c_ctx, w_mod, b_mod,
           ffn1_w_gate, ffn1_w_up, ffn1_w_down, ffn2_w_gate, ffn2_w_up, ffn2_w_down,
           ab_w_in, ab_w_out, a_q_norm, a_k_norm, a_sink, b_q_norm, b_k_norm,
           b_lambda_q1, b_lambda_k1, b_lambda_q2, b_lambda_k2, b_sub_norm,
           cd_w_in, cd_w_out, c_q_a_norm, c_kv_a_norm, c_w_uq, c_w_ukv,
           c_q_nope_norm, c_q_rope_norm, c_k_nope_norm, c_k_rope_norm,
           d_q_norm, d_k_norm, d_rpb):
    nb, seq, d = x.shape
    lctx = ctx.shape[1]
    depth = w_mod.shape[0]
    hd = HEAD_DIM

    n_rows = -(-(nb + 1) // 8) * 8
    c_rows = jnp.concatenate([c, c_ctx[None, :], jnp.zeros((n_rows - nb - 1, d), F32)], axis=0)
    mods_all = _mod_vectors(c_rows, w_mod, b_mod).reshape(depth, n_rows, N_MOD, d)
    x_row = lambda b: b
    ctx_row = lambda b: nb

    rope_head = _rope_tables_t(seq, hd)
    rope_mla = _rope_tables_t(seq, C_ROPE)

    w1 = tuple(_cast_bf16(w) for w in (ffn1_w_gate, ffn1_w_up, ffn1_w_down))
    w2 = tuple(_cast_bf16(w) for w in (ffn2_w_gate, ffn2_w_up, ffn2_w_down))
    ab_out = _cast_bf16(ab_w_out)
    cd_out = _cast_bf16(cd_w_out)

    xc = ctx.reshape(1, nb * lctx, d)
    for l in range(depth):
        need_ctx = l < depth - 1
        mods = mods_all[l]
        x = _ffn(x, mods, x_row, *w1, l, j0=0)
        xc = _ffn(xc, mods, ctx_row, *w1, l, j0=0)
        xc_b = xc.reshape(nb, lctx, d)
        i = l // 2
        if l % 2 == 0:
            lam_init = 0.8 - 0.6 * math.exp(-0.3 * l)
            w_in = ab_w_in[i]
            n_q = A_HEADS * hd + B_HEADS * 2 * hd
            n_ak = A_KV_HEADS * hd
            n_bk = B_HEADS * 2 * hd
            o_av = n_q + n_ak
            o_bk = o_av + n_ak
            o_bv = o_bk + n_bk
            wt = jnp.concatenate([w_in[:, :o_av], w_in[:, o_bk:o_bv], w_in[:, o_av:o_bk], w_in[:, o_bv:]],
                                 axis=1).T.astype(BF16)
            gains = jnp.concatenate([_col(a_q_norm[i]), _col(b_q_norm[i]),
                                     _col(a_k_norm[i]), _col(b_k_norm[i])], axis=0)
            aqt, bqt, ak, bk, avt, bvt = _proj_ab(x, mods, x_row, wt, gains, rope_head, need_q=True)
            ctx_out = _proj_ab(xc_b, mods, ctx_row, wt if need_ctx else wt[n_q:], gains, None, need_q=need_ctx)
            akx, bkx, avxt, bvxt = ctx_out[-4:]
            sink = a_sink[i].astype(F32)
            lam_vecs = jnp.stack([b_lambda_q1[i], b_lambda_k1[i], b_lambda_q2[i], b_lambda_k2[i]]).astype(F32)
            sub_gain = _col(b_sub_norm[i])
            y1 = _attn_a(sink, aqt, ak, avt, akx, avxt, latent=True)
            y2 = _attn_b(lam_vecs, sub_gain, bqt, bk, bvt, bkx, bvxt, latent=True, lam_init=lam_init)
            w_out = ab_out
            if need_ctx:
                aqxt, bqxt = ctx_out[:2]
                y1x = _attn_a(sink, aqxt, None, None, akx, avxt, latent=False)
                y2x = _attn_b(lam_vecs, sub_gain, bqxt, None, None, bkx, bvxt, latent=False, lam_init=lam_init)
        else:
            assert not need_ctx, "context queries of an odd layer are only needed for depth > 2"
            w_in = cd_w_in[i]
            n_q = C_Q_RANK + D_HEADS * hd
            wt = w_in.T.astype(BF16)
            uq = c_w_uq[i].reshape(C_Q_RANK, C_HEADS, C_NOPE + C_ROPE)
            wuq = jnp.concatenate([uq[:, :, :C_NOPE].reshape(C_Q_RANK, -1),
                                   uq[:, :, C_NOPE:].reshape(C_Q_RANK, -1)], axis=1).T.astype(BF16)
            ukv = c_w_ukv[i].reshape(C_KV_RANK, C_HEADS, C_NOPE + C_V)
            wuk = ukv[:, :, :C_NOPE].reshape(C_KV_RANK, -1).T.astype(BF16)
            wuv = ukv[:, :, C_NOPE:].reshape(C_KV_RANK, -1).T.astype(BF16)
            g_qa = _col(c_q_a_norm[i])
            g_kva = _col(c_kv_a_norm[i])
            g_head = jnp.concatenate([_col(c_q_nope_norm[i]), _col(c_q_rope_norm[i]),
                                      _col(c_k_nope_norm[i]), _col(c_k_rope_norm[i]),
                                      _col(d_q_norm[i]), _col(d_k_norm[i])], axis=0)
            cqt, dqt, ck, cvt, dk, dvt = _proj_cd(x, mods, x_row, wt, wuq, wuk, wuv,
                                                  g_qa, g_kva, g_head, rope_mla, need_q=True)
            ckx, cvxt, dkx, dvxt = _proj_cd(xc_b, mods, ctx_row, wt[n_q:], wuq, wuk, wuv,
                                            g_qa, g_kva, g_head, None, need_q=False)
            y1 = _attn_c(cqt, ck, cvt, ckx, cvxt)
            y2 = _attn_d(_rpb_table(d_rpb[i].astype(F32)), dqt, dk, dvt, dkx, dvxt)
            w_out = cd_out
        x = _ffn(x, mods, x_row, *w2, l, j0=6, y=(y1, y2), w_out=w_out, out_layer=i)
        if need_ctx:
            half = y1x.shape[-1]
            xc = _ffn(xc, mods, ctx_row, *w2, l, j0=6, w_out=w_out, out_layer=i,
                      y=(y1x.reshape(1, nb * lctx, half), y2x.reshape(1, nb * lctx, half)))
    return x
```

```python
import functools
import math

import jax
import jax.numpy as jnp
from jax import lax
from jax.experimental import pallas as pl
from jax.experimental.pallas import tpu as pltpu

F32 = jnp.float32
BF16 = jnp.bfloat16

D_MODEL = 1024
GRID_W = 64
HEAD_DIM = 64
D_FF = 2816
N_MOD = 9
ROPE_BASE = 10000.0
EPS = 1e-6
NEG = -1e30
LOG2E = math.log2(math.e)
A_HEADS = 8
A_KV_HEADS = 2
A_WINDOW = 128
B_HEADS = 4
B_V_DIM = 2 * HEAD_DIM
C_HEADS = 8
C_Q_RANK = 768
C_KV_RANK = 256
C_NOPE = 64
C_ROPE = 32
C_V = 64
C_PAD = 128
D_HEADS = 8
D_WIN_ROWS = 8
D_WIN_COLS = 16

LANES = 128
VMEM_LIMIT_BYTES = 56 * 1024 * 1024

CAST_BLOCK_BYTES = 3 * 1024 * 1024
FFN_CHUNK = 256
TOKEN_TILE = 1024
CTX_FFN_TILE = 512
FFN_TOKEN_TILE = 1024
ATTN_Q_TILE = 256
LOCAL_TILES = 4
FULL_Q_TILE = 512
FULL_TILES = 4
KEY_CHUNK = 256
ONES_ROWS = 16
NAT_Q_ROWS = ATTN_Q_TILE // GRID_W
NAT_WIN_ROWS = 12


def _params(**flags):
    return pltpu.CompilerParams(vmem_limit_bytes=VMEM_LIMIT_BYTES, flags=flags or None)


def _resident(block_shape, index_map):
    return pl.BlockSpec(block_shape, index_map, pipeline_mode=pl.Buffered(1))


def _sigmoid(x):
    return 1.0 / (1.0 + jnp.exp(-x))


def _rms_rows(z, gain):
    ms = jnp.mean(z * z, axis=0, keepdims=True)
    return z * lax.rsqrt(ms + EPS) * gain


def _norm_modulate(x, shift, scale):
    r = lax.rsqrt(jnp.mean(x * x, axis=-1, keepdims=True) + EPS)
    return (x * r) * (1.0 + scale) + shift


def _rope_rows(y, cos, sin, nf):
    part = jnp.concatenate([y[nf:2 * nf], y[0:nf], y[3 * nf:4 * nf], y[2 * nf:3 * nf]], axis=0)
    return y * cos + part * sin


def _mod_kernel(c_ref, w_ref, b_ref, o_ref):
    c = c_ref[...]
    a = (c * _sigmoid(c)).astype(BF16)
    o_ref[0] = jnp.dot(a, w_ref[0].astype(BF16), preferred_element_type=F32) + b_ref[0]


def _mod_vectors(c_rows, w_mod, b_mod):
    depth, d, n = w_mod.shape
    rows = c_rows.shape[0]
    tn = 1152
    return pl.pallas_call(
        _mod_kernel,
        out_shape=jax.ShapeDtypeStruct((depth, rows, n), F32),
        grid=(depth, n // tn),
        in_specs=[
            pl.BlockSpec((rows, d), lambda l, j: (0, 0)),
            pl.BlockSpec((1, d, tn), lambda l, j: (l, 0, j)),
            pl.BlockSpec((1, 1, tn), lambda l, j: (l, 0, j)),
        ],
        out_specs=pl.BlockSpec((1, rows, tn), lambda l, j: (l, 0, j)),
        compiler_params=_params(),
        name="mod_vectors",
    )(c_rows, w_mod, b_mod.reshape(depth, 1, n))


def _cast_kernel(w_ref, o_ref):
    o_ref[...] = w_ref[...].astype(BF16)


def _cast_bf16(w):
    depth, rows, cols = w.shape
    tr = next(rows // k for k in range(1, rows + 1)
              if rows % k == 0 and (rows // k) % 8 == 0 and (rows // k) * cols * 4 <= CAST_BLOCK_BYTES)
    spec = pl.BlockSpec((1, tr, cols), lambda l, i: (l, i, 0))
    return pl.pallas_call(
        _cast_kernel,
        out_shape=jax.ShapeDtypeStruct(w.shape, BF16),
        grid=(depth, rows // tr),
        in_specs=[spec],
        out_specs=spec,
        name="cast_bf16",
    )(w)


def _ffn_kernel(*refs, j0, fuse_out):
    if fuse_out:
        x_ref, y1_ref, y2_ref, mod_ref, wo_ref, wg_ref, wu_ref, wd_ref, o_ref, acc_ref = refs
    else:
        x_ref, mod_ref, wg_ref, wu_ref, wd_ref, o_ref, acc_ref = refs
    x = x_ref[0]
    if fuse_out:
        half = y1_ref.shape[-1]
        y = (jnp.dot(y1_ref[0], wo_ref[0, 0:half, :], preferred_element_type=F32)
             + jnp.dot(y2_ref[0], wo_ref[0, half:, :], preferred_element_type=F32))
        x = x + mod_ref[0, 5:6, :] * y
    shift = mod_ref[0, j0:j0 + 1, :]
    scale = mod_ref[0, j0 + 1:j0 + 2, :]
    gate = mod_ref[0, j0 + 2:j0 + 3, :]
    h = _norm_modulate(x, shift, scale).astype(BF16)
    d_ff = wg_ref.shape[2]
    for c in range(d_ff // FFN_CHUNK):
        lo, hi = c * FFN_CHUNK, (c + 1) * FFN_CHUNK
        g = jnp.dot(h, wg_ref[0, :, lo:hi], preferred_element_type=F32)
        u = jnp.dot(h, wu_ref[0, :, lo:hi], preferred_element_type=F32)
        a = (g * _sigmoid(g) * u).astype(BF16)
        part = jnp.dot(a, wd_ref[0, lo:hi, :], preferred_element_type=F32)
        if c == 0:
            acc_ref[...] = part
        else:
            acc_ref[...] += part
    o_ref[0] = x + (0.5 * gate) * acc_ref[...]


def _ffn(x, mods, mod_row, wg, wu, wd, layer, *, j0, y=None, w_out=None, out_layer=0):
    nb, t, d = x.shape
    tm = min(FFN_TOKEN_TILE if nb > 1 else CTX_FFN_TILE, t)
    d_ff = wg.shape[2]
    fuse_out = y is not None
    tok = lambda b, i: (b, i, 0)
    this_layer = lambda b, i: (layer, 0, 0)
    in_specs = [pl.BlockSpec((1, tm, d), tok)]
    args = [x]
    if fuse_out:
        half = y[0].shape[-1]
        in_specs += [pl.BlockSpec((1, tm, half), tok), pl.BlockSpec((1, tm, half), tok)]
        args += [y[0], y[1]]
    in_specs.append(pl.BlockSpec((1, N_MOD, d), lambda b, i: (mod_row(b), 0, 0)))
    args.append(mods)
    if fuse_out:
        in_specs.append(_resident((1,) + w_out.shape[1:], lambda b, i: (out_layer, 0, 0)))
        args.append(w_out)
    in_specs += [_resident((1, d, d_ff), this_layer), _resident((1, d, d_ff), this_layer),
                 _resident((1, d_ff, d), this_layer)]
    args += [wg, wu, wd]
    return pl.pallas_call(
        functools.partial(_ffn_kernel, j0=j0, fuse_out=fuse_out),
        out_shape=jax.ShapeDtypeStruct(x.shape, F32),
        grid=(nb, t // tm),
        in_specs=in_specs,
        out_specs=pl.BlockSpec((1, tm, d), tok),
        scratch_shapes=[pltpu.VMEM((tm, d), F32)],
        compiler_params=_params(),
        name="ffn_out" if fuse_out else "ffn",
    )(*args)


def _store_token_major(ref, col0, blocks):
    width = sum(b.shape[0] for b in blocks)
    ref[0, :, col0:col0 + width] = jnp.concatenate(blocks, axis=0).T.astype(BF16)


def _store_values_t(ref, zrows, dv):
    ones = jnp.ones((ONES_ROWS, zrows.shape[1]), F32)
    stride = dv + ONES_ROWS
    for hh in range(zrows.shape[0] // dv):
        block = jnp.concatenate([zrows[hh * dv:(hh + 1) * dv], ones], axis=0)
        ref[0, hh * stride:(hh + 1) * stride, :] = block.astype(BF16)


def _modulated_t(x_ref, mod_ref):
    h = _norm_modulate(x_ref[0], mod_ref[0, 3:4, :], mod_ref[0, 4:5, :])
    return h.T.astype(BF16)


def _grouped_projection(wt_ref, ht, groups):
    r0, pending = 0, None
    for n, epilogue in groups:
        z = jnp.dot(wt_ref[r0:r0 + n, :], ht, preferred_element_type=F32)
        if pending is not None:
            pending[0](pending[1])
        pending = (epilogue, z)
        r0 += n
    pending[0](pending[1])


def _proj_ab_kernel(*refs, rope, need_q):
    x_ref, mod_ref, wt_ref, gain_ref = refs[:4]
    refs = refs[4:]
    if rope:
        cos_ref, sin_ref = refs[:2]
        refs = refs[2:]
    if need_q:
        aqt_ref, bqt_ref = refs[:2]
        refs = refs[2:]
    ak_ref, bk_ref, avt_ref, bvt_ref = refs

    ht = _modulated_t(x_ref, mod_ref)
    hd = HEAD_DIM
    g_aq = gain_ref[0 * hd:1 * hd]
    g_bq = gain_ref[1 * hd:2 * hd]
    g_ak = gain_ref[2 * hd:3 * hd]
    g_bk = gain_ref[3 * hd:4 * hd]
    if rope:
        cos = cos_ref[...]
        sin = sin_ref[...]

    def head(z, row0, gain):
        y = _rms_rows(z[row0:row0 + hd], gain)
        if rope:
            y = _rope_rows(y, cos, sin, hd // 4)
        return y

    n_aq = A_HEADS * hd
    n_bq = B_HEADS * 2 * hd
    n_ak = A_KV_HEADS * hd
    qscale = hd ** -0.5 * LOG2E

    def queries(ref, gain):
        def epilogue(z):
            for j in range(z.shape[0] // hd):
                ref[0, j * hd:(j + 1) * hd, :] = (head(z, j * hd, gain) * qscale).astype(BF16)
        return epilogue

    def keys(z):
        _store_token_major(ak_ref, 0, [head(z, j * hd, g_ak) for j in range(A_KV_HEADS)])
        for j in range(n_bq // LANES):
            r0 = n_ak + j * LANES
            _store_token_major(bk_ref, j * LANES, [head(z, r0, g_bk), head(z, r0 + hd, g_bk)])

    def values(z):
        _store_values_t(avt_ref, z[:n_ak], hd)
        _store_values_t(bvt_ref, z[n_ak:], B_V_DIM)

    groups = [(n_aq, queries(aqt_ref, g_aq)), (n_bq, queries(bqt_ref, g_bq))] if need_q else []
    groups += [(n_ak + n_bq, keys), (wt_ref.shape[0] - sum(n for n, _ in groups) - n_ak - n_bq, values)]
    _grouped_projection(wt_ref, ht, groups)


def _proj_ab(x, mods, mod_row, wt, gains, rope_tabs, *, need_q):
    nb, t, d = x.shape
    tm = min(TOKEN_TILE, t)
    rope = rope_tabs is not None
    hd = HEAD_DIM
    n_aq, n_bq, n_ak = A_HEADS * hd, B_HEADS * 2 * hd, A_KV_HEADS * hd
    n_av, n_bv = A_KV_HEADS * (hd + ONES_ROWS), B_HEADS * (B_V_DIM + ONES_ROWS)
    tok = lambda b, i: (b, i, 0)
    ftok = lambda b, i: (b, 0, i)
    const = lambda b, i: (0, 0)
    in_specs = [pl.BlockSpec((1, tm, d), tok), pl.BlockSpec((1, N_MOD, d), lambda b, i: (mod_row(b), 0, 0)),
                _resident(wt.shape, const), _resident(gains.shape, const)]
    args = [x, mods, wt, gains]
    if rope:
        in_specs += [pl.BlockSpec((hd, tm), lambda b, i: (0, i))] * 2
        args += list(rope_tabs)
    out_shape, out_specs = [], []
    if need_q:
        out_shape += [jax.ShapeDtypeStruct((nb, n_aq, t), BF16), jax.ShapeDtypeStruct((nb, n_bq, t), BF16)]
        out_specs += [pl.BlockSpec((1, n_aq, tm), ftok), pl.BlockSpec((1, n_bq, tm), ftok)]
    out_shape += [jax.ShapeDtypeStruct((nb, t, n_ak), BF16), jax.ShapeDtypeStruct((nb, t, n_bq), BF16),
                  jax.ShapeDtypeStruct((nb, n_av, t), BF16), jax.ShapeDtypeStruct((nb, n_bv, t), BF16)]
    out_specs += [pl.BlockSpec((1, tm, n_ak), tok), pl.BlockSpec((1, tm, n_bq), tok),
                  pl.BlockSpec((1, n_av, tm), ftok), pl.BlockSpec((1, n_bv, tm), ftok)]
    return pl.pallas_call(
        functools.partial(_proj_ab_kernel, rope=rope, need_q=need_q),
        out_shape=out_shape,
        grid=(nb, t // tm),
        in_specs=in_specs,
        out_specs=out_specs,
        compiler_params=_params(),
        name="proj_ab",
    )(*args)


def _proj_cd_kernel(*refs, rope, need_q):
    (x_ref, mod_ref, wt_ref, wuq_ref, wuk_ref, wuv_ref, g_qa_ref, g_kva_ref, g_head_ref) = refs[:9]
    refs = refs[9:]
    if rope:
        cos_ref, sin_ref = refs[:2]
        refs = refs[2:]
    if need_q:
        cqt_ref, dqt_ref = refs[:2]
        refs = refs[2:]
    ck_ref, cvt_ref, dk_ref, dvt_ref = refs

    ht = _modulated_t(x_ref, mod_ref)
    tm = ht.shape[1]

    hd = HEAD_DIM
    n_cq = C_Q_RANK
    n_dq = D_HEADS * hd
    n_ckv = C_KV_RANK + C_ROPE
    g_qn = g_head_ref[0:64]
    g_qr = g_head_ref[64:96]
    g_kn = g_head_ref[96:160]
    g_kr = g_head_ref[160:192]
    g_dq = g_head_ref[192:256]
    g_dk = g_head_ref[256:320]
    if rope:
        cos = cos_ref[...]
        sin = sin_ref[...]
    zpad = jnp.zeros((C_PAD - C_NOPE - C_ROPE, tm), F32)

    def latent_queries(z):
        cqn = _rms_rows(z, g_qa_ref[...]).astype(BF16)
        qt = jnp.dot(wuq_ref[...], cqn, preferred_element_type=F32)
        cscale = (C_NOPE + C_ROPE) ** -0.5 * LOG2E
        for hh in range(C_HEADS):
            qn = _rms_rows(qt[hh * C_NOPE:(hh + 1) * C_NOPE], g_qn)
            rr = C_HEADS * C_NOPE + hh * C_ROPE
            qr = _rms_rows(qt[rr:rr + C_ROPE], g_qr)
            if rope:
                qr = _rope_rows(qr, cos, sin, C_ROPE // 4)
            qh = jnp.concatenate([qn, qr, zpad], axis=0) * cscale
            cqt_ref[0, hh * C_PAD:(hh + 1) * C_PAD, :] = qh.astype(BF16)

    def grid_queries(z):
        dscale = hd ** -0.5 * LOG2E
        for hh in range(D_HEADS):
            dqt_ref[0, hh * hd:(hh + 1) * hd, :] = (_rms_rows(z[hh * hd:(hh + 1) * hd], g_dq) * dscale).astype(BF16)

    def latent_keys_values(z):
        cn = _rms_rows(z[:C_KV_RANK], g_kva_ref[...]).astype(BF16)
        kr = _rms_rows(z[C_KV_RANK:], g_kr)
        if rope:
            kr = _rope_rows(kr, cos, sin, C_ROPE // 4)
        knt = jnp.dot(wuk_ref[...], cn, preferred_element_type=F32)
        for hh in range(C_HEADS):
            kn = _rms_rows(knt[hh * C_NOPE:(hh + 1) * C_NOPE], g_kn)
            _store_token_major(ck_ref, hh * C_PAD, [kn, kr, zpad])
        _store_values_t(cvt_ref, jnp.dot(wuv_ref[...], cn, preferred_element_type=F32), C_V)

    def grid_keys(z):
        for j in range(n_dq // LANES):
            rr = j * LANES
            _store_token_major(dk_ref, j * LANES, [_rms_rows(z[rr:rr + hd], g_dk),
                                                   _rms_rows(z[rr + hd:rr + 2 * hd], g_dk)])

    def grid_values(z):
        _store_values_t(dvt_ref, z, hd)

    groups = [(n_cq, latent_queries), (n_dq, grid_queries)] if need_q else []
    groups += [(n_ckv, latent_keys_values), (n_dq, grid_keys), (n_dq, grid_values)]
    _grouped_projection(wt_ref, ht, groups)


def _proj_cd(x, mods, mod_row, wt, wuq, wuk, wuv, g_qa, g_kva, g_head, rope_tabs, *, need_q):
    nb, t, d = x.shape
    tm = min(TOKEN_TILE, t)
    rope = rope_tabs is not None
    hd = HEAD_DIM
    n_c = C_HEADS * C_PAD
    n_d = D_HEADS * hd
    n_cv = C_HEADS * (C_V + ONES_ROWS)
    n_dv = D_HEADS * (hd + ONES_ROWS)
    tok = lambda b, i: (b, i, 0)
    ftok = lambda b, i: (b, 0, i)
    const = lambda b, i: (0, 0)
    consts = [wt, wuq, wuk, wuv, g_qa, g_kva, g_head]
    in_specs = [pl.BlockSpec((1, tm, d), tok), pl.BlockSpec((1, N_MOD, d), lambda b, i: (mod_row(b), 0, 0))]
    in_specs += [_resident(a.shape, const) for a in consts]
    args = [x, mods] + consts
    if rope:
        in_specs += [pl.BlockSpec((C_ROPE, tm), lambda b, i: (0, i))] * 2
        args += list(rope_tabs)
    out_shape, out_specs = [], []
    if need_q:
        out_shape += [jax.ShapeDtypeStruct((nb, n_c, t), BF16), jax.ShapeDtypeStruct((nb, n_d, t), BF16)]
        out_specs += [pl.BlockSpec((1, n_c, tm), ftok), pl.BlockSpec((1, n_d, tm), ftok)]
    out_shape += [jax.ShapeDtypeStruct((nb, t, n_c), BF16), jax.ShapeDtypeStruct((nb, n_cv, t), BF16),
                  jax.ShapeDtypeStruct((nb, t, n_d), BF16), jax.ShapeDtypeStruct((nb, n_dv, t), BF16)]
    out_specs += [pl.BlockSpec((1, tm, n_c), tok), pl.BlockSpec((1, n_cv, tm), ftok),
                  pl.BlockSpec((1, tm, n_d), tok), pl.BlockSpec((1, n_dv, tm), ftok)]
    return pl.pallas_call(
        functools.partial(_proj_cd_kernel, rope=rope, need_q=need_q),
        out_shape=out_shape,
        grid=(nb, t // tm),
        in_specs=in_specs,
        out_specs=out_specs,
        compiler_params=_params(),
        name="proj_cd",
    )(*args)


def _attend_units(n_units, n_chunks, rhs_of, score_chunk, value_chunk, s_ref, p_ref, extra=None):
    stats = {}
    outs = [None] * n_units
    tq = s_ref.shape[2]
    for t in range(n_units + 2):
        ua, ub, uc = t, t - 1, t - 2
        do_a, do_b, do_c = ua < n_units, 0 <= ub < n_units, 0 <= uc < n_units
        if do_a:
            rhs = rhs_of(ua)
            m8 = None
        acc = None
        for c in range(n_chunks):
            rows = slice(c * KEY_CHUNK, (c + 1) * KEY_CHUNK)
            if do_a:
                s = score_chunk(ua, c, rhs)
                s_ref[ua % 2, rows, :] = s
                pm = jnp.max(s.reshape(KEY_CHUNK // 8, 8, tq), axis=0)
                m8 = pm if m8 is None else jnp.maximum(m8, pm)
            if do_b:
                p_ref[ub % 2, rows, :] = jnp.exp2(s_ref[ub % 2, rows, :] - stats[ub]).astype(BF16)
            if do_c:
                part = jnp.dot(value_chunk(uc, c), p_ref[uc % 2, rows, :], preferred_element_type=F32)
                acc = part if acc is None else acc + part
        if do_a:
            m = jnp.max(m8, axis=0, keepdims=True)
            stats[ua] = m if extra is None else jnp.maximum(m, extra(ua))
        if do_c:
            dv = acc.shape[0] - ONES_ROWS
            total = acc[dv:dv + 1]
            if extra is not None:
                total = total + jnp.exp2(extra(uc) - stats[uc])
            outs[uc] = acc[:dv] * (1.0 / total)
    return outs


def _half_rhs(q_head, upper):
    zeros = jnp.zeros_like(q_head)
    return jnp.concatenate([zeros, q_head] if upper else [q_head, zeros], axis=0)


def _store_outputs(o_ref, outs, tok0=0):
    rows = jnp.concatenate(outs, axis=0)
    tq = rows.shape[1]
    for j in range(rows.shape[0] // LANES):
        o_ref[0, tok0:tok0 + tq, j * LANES:(j + 1) * LANES] = rows[j * LANES:(j + 1) * LANES].T.astype(BF16)


def _attn_scratch(n_keys, tq):
    return [pltpu.VMEM((2, n_keys, tq), F32), pltpu.VMEM((2, n_keys, tq), BF16)]


def _attn_a_kernel(*refs, latent, seq):
    if latent:
        (sink_ref, qt_ref, kp_ref, kc_ref, kn_ref, kx_ref, vp_ref, vc_ref, vn_ref, vx_ref,
         o_ref, s_ref, p_ref) = refs
    else:
        sink_ref, qt_ref, kx_ref, vx_ref, o_ref, s_ref, p_ref = refs
    qt = qt_ref[0]
    hd = HEAD_DIM
    tq = s_ref.shape[2]
    n_tiles = qt.shape[1] // tq
    group = A_HEADS // A_KV_HEADS
    kx = kx_ref[0]
    vx = vx_ref[0]
    lat_chunks = 0
    if latent:
        klat = jnp.concatenate([kp_ref[0], kc_ref[0], kn_ref[0]], axis=0)
        vlat = jnp.concatenate([vp_ref[0], vc_ref[0], vn_ref[0]], axis=1)
        n_lat = tq + 2 * A_WINDOW
        lat_chunks = n_lat // KEY_CHUNK
        krow = lax.broadcasted_iota(jnp.int32, (n_lat, tq), 0)
        qcol = lax.broadcasted_iota(jnp.int32, (n_lat, tq), 1)
        in_band = jnp.abs(krow - A_WINDOW - qcol) <= A_WINDOW
        biases = []
        for j in range(n_tiles):
            kpos = krow + (pl.program_id(1) * qt.shape[1] + j * tq - A_WINDOW)
            biases.append(jnp.where(in_band, jnp.where(kpos >= 0, jnp.where(kpos < seq, 0.0, NEG), NEG), NEG))
    n_chunks = lat_chunks + kx.shape[0] // KEY_CHUNK

    def rhs_of(u):
        j, hh = divmod(u, A_HEADS)
        return _half_rhs(qt[hh * hd:(hh + 1) * hd, j * tq:(j + 1) * tq], hh // group == 1)

    def score_chunk(u, c, rhs):
        j = u // A_HEADS
        if c < lat_chunks:
            r0 = j * tq + c * KEY_CHUNK
            return (jnp.dot(klat[r0:r0 + KEY_CHUNK], rhs, preferred_element_type=F32)
                    + biases[j][c * KEY_CHUNK:(c + 1) * KEY_CHUNK])
        r0 = (c - lat_chunks) * KEY_CHUNK
        return jnp.dot(kx[r0:r0 + KEY_CHUNK], rhs, preferred_element_type=F32)

    def value_chunk(u, c):
        j, hh = divmod(u, A_HEADS)
        g = hh // group
        rows = slice(g * (hd + ONES_ROWS), (g + 1) * (hd + ONES_ROWS))
        if c < lat_chunks:
            r0 = j * tq + c * KEY_CHUNK
            return vlat[rows, r0:r0 + KEY_CHUNK]
        r0 = (c - lat_chunks) * KEY_CHUNK
        return vx[rows, r0:r0 + KEY_CHUNK]

    outs = _attend_units(n_tiles * A_HEADS, n_chunks, rhs_of, score_chunk, value_chunk, s_ref, p_ref,
                         extra=lambda u: sink_ref[u % A_HEADS] * LOG2E)
    for j in range(n_tiles):
        _store_outputs(o_ref, outs[j * A_HEADS:(j + 1) * A_HEADS], j * tq)


def _attn_a(sink, qt, k, vt, kx, vxt, *, latent):
    nb, nq, t = qt.shape
    n_kv = A_KV_HEADS * HEAD_DIM
    n_v = vxt.shape[1]
    lx = kx.shape[1]
    smem = pl.BlockSpec(memory_space=pltpu.SMEM)
    if latent:
        tq = LOCAL_TILES * ATTN_Q_TILE
        seq = k.shape[1]
        w = A_WINDOW
        r = tq // w
        last = seq // w - 1
        prev = lambda i: jnp.maximum(i * r - 1, 0)
        nxt = lambda i: jnp.minimum((i + 1) * r, last)
        in_specs = [
            smem,
            pl.BlockSpec((1, nq, tq), lambda b, i: (b, 0, i)),
            pl.BlockSpec((1, w, n_kv), lambda b, i: (b, prev(i), 0)),
            pl.BlockSpec((1, tq, n_kv), lambda b, i: (b, i, 0)),
            pl.BlockSpec((1, w, n_kv), lambda b, i: (b, nxt(i), 0)),
            pl.BlockSpec((1, lx, n_kv), lambda b, i: (b, 0, 0)),
            pl.BlockSpec((1, n_v, w), lambda b, i: (b, 0, prev(i))),
            pl.BlockSpec((1, n_v, tq), lambda b, i: (b, 0, i)),
            pl.BlockSpec((1, n_v, w), lambda b, i: (b, 0, nxt(i))),
            pl.BlockSpec((1, n_v, lx), lambda b, i: (b, 0, 0)),
        ]
        args = [sink, qt, k, k, k, kx, vt, vt, vt, vxt]
        n_keys = ATTN_Q_TILE + 2 * w + lx
    else:
        tq = t
        seq = 0
        in_specs = [
            smem,
            pl.BlockSpec((1, nq, tq), lambda b, i: (b, 0, i)),
            pl.BlockSpec((1, lx, n_kv), lambda b, i: (b, 0, 0)),
            pl.BlockSpec((1, n_v, lx), lambda b, i: (b, 0, 0)),
        ]
        args = [sink, qt, kx, vxt]
        n_keys = lx
    return pl.pallas_call(
        functools.partial(_attn_a_kernel, latent=latent, seq=seq),
        out_shape=jax.ShapeDtypeStruct((nb, t, nq), BF16),
        grid=(nb, t // tq),
        in_specs=in_specs,
        out_specs=pl.BlockSpec((1, tq, nq), lambda b, i: (b, i, 0)),
        scratch_shapes=_attn_scratch(n_keys, min(ATTN_Q_TILE, tq)),
        compiler_params=_params(),
        name="attn_window",
    )(*args)


def _attn_b_kernel(*refs, latent, lam_init):
    if latent:
        lam_ref, gain_ref, qt_ref, k_ref, kx_ref, vt_ref, vxt_ref, o_ref, s_ref, p_ref = refs
    else:
        lam_ref, gain_ref, qt_ref, kx_ref, vxt_ref, o_ref, s_ref, p_ref = refs
    qt = qt_ref[0]
    hd = HEAD_DIM
    lv = lam_ref[...]
    lam = (jnp.exp(jnp.sum(lv[0:1] * lv[1:2], axis=-1, keepdims=True))
           - jnp.exp(jnp.sum(lv[2:3] * lv[3:4], axis=-1, keepdims=True)) + lam_init)
    gain = gain_ref[...] * (1.0 - lam_init)

    n_ctx = kx_ref.shape[1] // KEY_CHUNK

    tq = s_ref.shape[2]
    n_tiles = qt.shape[1] // tq
    n_maps = 2 * B_HEADS

    def rhs_of(u):
        j, mm = divmod(u, n_maps)
        return _half_rhs(qt[mm * hd:(mm + 1) * hd, j * tq:(j + 1) * tq], mm % 2 == 1)

    def score_chunk(u, c, rhs):
        hh = (u % n_maps) // 2
        cols = slice(hh * 2 * hd, (hh + 1) * 2 * hd)
        if c < n_ctx:
            k = kx_ref[0, c * KEY_CHUNK:(c + 1) * KEY_CHUNK, cols]
        else:
            k = k_ref[0, (c - n_ctx) * KEY_CHUNK:(c - n_ctx + 1) * KEY_CHUNK, cols]
        return jnp.dot(k, rhs, preferred_element_type=F32)

    def value_chunk(u, c):
        hh = (u % n_maps) // 2
        stride = B_V_DIM + ONES_ROWS
        rows = slice(hh * stride, (hh + 1) * stride)
        if c < n_ctx:
            return vxt_ref[0, rows, c * KEY_CHUNK:(c + 1) * KEY_CHUNK]
        return vt_ref[0, rows, (c - n_ctx) * KEY_CHUNK:(c - n_ctx + 1) * KEY_CHUNK]

    outs = _attend_units(n_tiles * n_maps, s_ref.shape[1] // KEY_CHUNK, rhs_of, score_chunk, value_chunk,
                         s_ref, p_ref)
    for j in range(n_tiles):
        heads = []
        for hh in range(B_HEADS):
            o = outs[j * n_maps + 2 * hh] - lam * outs[j * n_maps + 2 * hh + 1]
            heads.append(_rms_rows(o, gain))
        _store_outputs(o_ref, heads, j * tq)


def _attn_b(lam_vecs, sub_gain, qt, k, vt, kx, vxt, *, latent, lam_init):
    nb, nq, t = qt.shape
    lx, nk = kx.shape[1], kx.shape[2]
    nv = vxt.shape[1]
    tu = min(FULL_Q_TILE, t)
    tq = min(FULL_TILES * tu, t)
    const = lambda b, i: (0, 0)
    whole = lambda b, i: (b, 0, 0)
    in_specs = [pl.BlockSpec(lam_vecs.shape, const), pl.BlockSpec(sub_gain.shape, const),
                pl.BlockSpec((1, nq, tq), lambda b, i: (b, 0, i))]
    args = [lam_vecs, sub_gain, qt]
    n_keys = lx
    if latent:
        seq = k.shape[1]
        in_specs += [pl.BlockSpec((1, seq, nk), whole), pl.BlockSpec((1, lx, nk), whole),
                     pl.BlockSpec((1, nv, seq), whole), pl.BlockSpec((1, nv, lx), whole)]
        args += [k, kx, vt, vxt]
        n_keys += seq
    else:
        in_specs += [pl.BlockSpec((1, lx, nk), whole), pl.BlockSpec((1, nv, lx), whole)]
        args += [kx, vxt]
    return pl.pallas_call(
        functools.partial(_attn_b_kernel, latent=latent, lam_init=lam_init),
        out_shape=jax.ShapeDtypeStruct((nb, t, B_HEADS * B_V_DIM), BF16),
        grid=(nb, t // tq),
        in_specs=in_specs,
        out_specs=pl.BlockSpec((1, tq, B_HEADS * B_V_DIM), lambda b, i: (b, i, 0)),
        scratch_shapes=_attn_scratch(n_keys, tu),
        compiler_params=_params(),
        name="attn_diff",
    )(*args)


def _attn_c_kernel(qt_ref, k_ref, kx_ref, vt_ref, vxt_ref, o_ref, s_ref, p_ref):
    qt = qt_ref[0]

    n_ctx = kx_ref.shape[1] // KEY_CHUNK

    tq = s_ref.shape[2]
    n_tiles = qt.shape[1] // tq

    def rhs_of(u):
        j, hh = divmod(u, C_HEADS)
        return qt[hh * C_PAD:(hh + 1) * C_PAD, j * tq:(j + 1) * tq]

    def score_chunk(u, c, rhs):
        hh = u % C_HEADS
        cols = slice(hh * C_PAD, (hh + 1) * C_PAD)
        if c < n_ctx:
            k = kx_ref[0, c * KEY_CHUNK:(c + 1) * KEY_CHUNK, cols]
        else:
            k = k_ref[0, (c - n_ctx) * KEY_CHUNK:(c - n_ctx + 1) * KEY_CHUNK, cols]
        return jnp.dot(k, rhs, preferred_element_type=F32)

    def value_chunk(u, c):
        hh = u % C_HEADS
        rows = slice(hh * (C_V + ONES_ROWS), (hh + 1) * (C_V + ONES_ROWS))
        if c < n_ctx:
            return vxt_ref[0, rows, c * KEY_CHUNK:(c + 1) * KEY_CHUNK]
        return vt_ref[0, rows, (c - n_ctx) * KEY_CHUNK:(c - n_ctx + 1) * KEY_CHUNK]

    outs = _attend_units(n_tiles * C_HEADS, s_ref.shape[1] // KEY_CHUNK, rhs_of, score_chunk, value_chunk,
                         s_ref, p_ref)
    for j in range(n_tiles):
        _store_outputs(o_ref, outs[j * C_HEADS:(j + 1) * C_HEADS], j * tq)


def _attn_c(qt, k, vt, kx, vxt):
    nb, nq, t = qt.shape
    lx = kx.shape[1]
    nv = vt.shape[1]
    tq = FULL_TILES * FULL_Q_TILE
    whole = lambda b, i: (b, 0, 0)
    return pl.pallas_call(
        _attn_c_kernel,
        out_shape=jax.ShapeDtypeStruct((nb, t, C_HEADS * C_V), BF16),
        grid=(nb, t // tq),
        in_specs=[pl.BlockSpec((1, nq, tq), lambda b, i: (b, 0, i)),
                  pl.BlockSpec((1, t, nq), whole),
                  pl.BlockSpec((1, lx, nq), whole),
                  pl.BlockSpec((1, nv, t), whole),
                  pl.BlockSpec((1, nv, lx), whole)],
        out_specs=pl.BlockSpec((1, tq, C_HEADS * C_V), lambda b, i: (b, i, 0)),
        scratch_shapes=_attn_scratch(t + lx, FULL_Q_TILE),
        compiler_params=_params(),
        name="attn_latent",
    )(qt, k, kx, vt, vxt)


def _rpb_table_kernel(rpb_ref, o_ref, *, n_dr, n_dc):
    hh = pl.program_id(0)
    shape = (GRID_W, LANES)
    kc = lax.broadcasted_iota(jnp.int32, shape, 0)
    lane = lax.broadcasted_iota(jnp.int32, shape, 1)
    right = lane >= GRID_W
    qc = jnp.where(right, lane - GRID_W, lane)
    dc = jnp.clip(kc - qc, -(D_WIN_COLS - 1), D_WIN_COLS - 1) + (D_WIN_COLS - 1)
    cs = jnp.clip(qc - D_WIN_COLS // 2, 0, GRID_W - D_WIN_COLS)
    in_cols = jnp.where(kc >= cs, jnp.where(kc < cs + D_WIN_COLS, 0.0, NEG), NEG)
    for dd in range(n_dr + 1):
        base_l = (hh * n_dr + min(dd, n_dr - 1)) * n_dc
        base_r = (hh * n_dr + max(dd - 1, 0)) * n_dc
        acc = jnp.zeros(shape, F32)
        for c in range(n_dc):
            val = jnp.where(right, rpb_ref[base_r + c], rpb_ref[base_l + c])
            acc = jnp.where(dc == c, val, acc)
        ok = jnp.where(right, 0.0 if dd >= 1 else NEG, 0.0 if dd <= n_dr - 1 else NEG)
        o_ref[0, dd] = acc * LOG2E + ok + in_cols


def _rpb_table(rpb):
    n_h, n_dr, n_dc = rpb.shape
    return pl.pallas_call(
        functools.partial(_rpb_table_kernel, n_dr=n_dr, n_dc=n_dc),
        out_shape=jax.ShapeDtypeStruct((n_h, n_dr + 1, GRID_W, LANES), F32),
        grid=(n_h,),
        in_specs=[pl.BlockSpec(memory_space=pltpu.SMEM)],
        out_specs=pl.BlockSpec((1, n_dr + 1, GRID_W, LANES), lambda h: (h, 0, 0, 0)),
        name="rpb_table",
    )(rpb.reshape(-1))


def _attn_d_kernel(tab_ref, qt_ref, k_ref, kx_ref, vt_ref, vxt_ref, o_ref, s_ref, p_ref, *, rows):
    qt = qt_ref[0]
    hd = HEAD_DIM
    tq = s_ref.shape[2]
    n_tiles = qt.shape[1] // tq
    kh = min(D_WIN_ROWS, rows)
    nwin = NAT_WIN_ROWS * GRID_W
    lane = lax.broadcasted_iota(jnp.int32, (GRID_W, LANES), 1)
    right = lane >= GRID_W
    n_dd = tab_ref.shape[1]
    q_pairs = NAT_Q_ROWS // 2
    rows_per_chunk = KEY_CHUNK // GRID_W
    n_win = nwin // KEY_CHUNK

    def in_window(kr, qr):
        rs = jnp.clip(qr - kh // 2, 0, rows - kh)
        return jnp.where((kr >= rs) & (kr < rs + kh), 0.0, NEG)

    tiles = []
    for j in range(n_tiles):
        r_first = (pl.program_id(1) * n_tiles + j) * NAT_Q_ROWS
        ws = jnp.clip(r_first - kh // 2, 0, rows - kh)
        wl = jnp.minimum(ws, rows - NAT_WIN_ROWS)
        tok0 = pl.multiple_of(wl * GRID_W, LANES)
        row_add = jnp.concatenate([
            jnp.concatenate([jnp.where(right, in_window(wl + jr, r_first + 2 * a + 1),
                                       in_window(wl + jr, r_first + 2 * a))
                             for a in range(q_pairs)], axis=1)
            for jr in range(NAT_WIN_ROWS)], axis=0)
        tiles.append((r_first, wl, k_ref[0, pl.ds(tok0, nwin), :], vt_ref[0, :, pl.ds(tok0, nwin)], row_add))

    def rhs_of(u):
        j, hh = divmod(u, D_HEADS)
        return _half_rhs(qt[hh * hd:(hh + 1) * hd, j * tq:(j + 1) * tq], hh % 2 == 1)

    def score_chunk(u, c, rhs):
        j, hh = divmod(u, D_HEADS)
        cols = slice((hh // 2) * LANES, (hh // 2 + 1) * LANES)
        if c >= n_win:
            return jnp.dot(kx_ref[0, (c - n_win) * KEY_CHUNK:(c - n_win + 1) * KEY_CHUNK, cols], rhs,
                           preferred_element_type=F32)
        r_first, wl, kwin, _, row_add = tiles[j]
        rows_c = slice(c * KEY_CHUNK, (c + 1) * KEY_CHUNK)
        bias = jnp.concatenate([
            jnp.concatenate([tab_ref[hh, jnp.clip((wl + jr) - (r_first + 2 * a) + (D_WIN_ROWS - 1), 0, n_dd - 1)]
                             for a in range(q_pairs)], axis=1)
            for jr in range(c * rows_per_chunk, (c + 1) * rows_per_chunk)], axis=0)
        return jnp.dot(kwin[rows_c, cols], rhs, preferred_element_type=F32) + (bias + row_add[rows_c])

    def value_chunk(u, c):
        j, hh = divmod(u, D_HEADS)
        rows_u = slice(hh * (hd + ONES_ROWS), (hh + 1) * (hd + ONES_ROWS))
        if c >= n_win:
            return vxt_ref[0, rows_u, (c - n_win) * KEY_CHUNK:(c - n_win + 1) * KEY_CHUNK]
        return tiles[j][3][rows_u, c * KEY_CHUNK:(c + 1) * KEY_CHUNK]

    outs = _attend_units(n_tiles * D_HEADS, s_ref.shape[1] // KEY_CHUNK, rhs_of, score_chunk, value_chunk,
                         s_ref, p_ref)
    for j in range(n_tiles):
        _store_outputs(o_ref, outs[j * D_HEADS:(j + 1) * D_HEADS], j * tq)


def _attn_d(table, qt, k, vt, kx, vxt):
    nb, nq, t = qt.shape
    lx = kx.shape[1]
    tq = LOCAL_TILES * ATTN_Q_TILE
    rows = t // GRID_W
    assert rows >= NAT_WIN_ROWS and rows % NAT_Q_ROWS == 0 and NAT_Q_ROWS % 2 == 0
    whole = lambda b, i: (b, 0, 0)
    return pl.pallas_call(
        functools.partial(_attn_d_kernel, rows=rows),
        out_shape=jax.ShapeDtypeStruct((nb, t, nq), BF16),
        grid=(nb, t // tq),
        in_specs=[_resident(table.shape, lambda b, i: (0, 0, 0, 0)),
                  pl.BlockSpec((1, nq, tq), lambda b, i: (b, 0, i)),
                  pl.BlockSpec((1, t, nq), whole),
                  pl.BlockSpec((1, lx, nq), whole),
                  pl.BlockSpec((1, vt.shape[1], t), whole),
                  pl.BlockSpec((1, vt.shape[1], lx), whole)],
        out_specs=pl.BlockSpec((1, tq, nq), lambda b, i: (b, i, 0)),
        scratch_shapes=_attn_scratch(NAT_WIN_ROWS * GRID_W + lx, ATTN_Q_TILE),
        compiler_params=_params(),
        name="attn_neighbourhood",
    )(table, qt, k, kx, vt, vxt)


def _rope_tables_t(n, rot_dim):
    nf = rot_dim // 4
    inv = ROPE_BASE ** (-jnp.arange(nf, dtype=F32) / nf)
    t = jnp.arange(n)
    row = (t // GRID_W).astype(F32)
    col = (t % GRID_W).astype(F32)
    ang = jnp.concatenate([row[:, None] * inv, col[:, None] * inv], axis=-1)
    cos, sin = jnp.cos(ang).T, jnp.sin(ang).T
    cos_e = jnp.concatenate([cos[:nf], cos[:nf], cos[nf:], cos[nf:]], axis=0)
    sin_e = jnp.concatenate([-sin[:nf], sin[:nf], -sin[nf:], sin[nf:]], axis=0)
    return cos_e, sin_e


def _col(v):
    return v.astype(F32).reshape(-1, 1)


def kernel(x, c, ctx, c_ctx, w_mod, b_mod,
           ffn1_w_gate, ffn1_w_up, ffn1_w_down, ffn2_w_gate, ffn2_w_up, ffn2_w_down,
           ab_w_in, ab_w_out, a_q_norm, a_k_norm, a_sink, b_q_norm, b_k_norm,
           b_lambda_q1, b_lambda_k1, b_lambda_q2, b_lambda_k2, b_sub_norm,
           cd_w_in, cd_w_out, c_q_a_norm, c_kv_a_norm, c_w_uq, c_w_ukv,
           c_q_nope_norm, c_q_rope_norm, c_k_nope_norm, c_k_rope_norm,
           d_q_norm, d_k_norm, d_rpb):
    nb, seq, d = x.shape
    lctx = ctx.shape[1]
    depth = w_mod.shape[0]
    hd = HEAD_DIM

    n_rows = -(-(nb + 1) // 8) * 8
    c_rows = jnp.concatenate([c, c_ctx[None, :], jnp.zeros((n_rows - nb - 1, d), F32)], axis=0)
    mods_all = _mod_vectors(c_rows, w_mod, b_mod).reshape(depth, n_rows, N_MOD, d)
    x_row = lambda b: b
    ctx_row = lambda b: nb

    rope_head = _rope_tables_t(seq, hd)
    rope_mla = _rope_tables_t(seq, C_ROPE)

    w1 = tuple(_cast_bf16(w) for w in (ffn1_w_gate, ffn1_w_up, ffn1_w_down))
    w2 = tuple(_cast_bf16(w) for w in (ffn2_w_gate, ffn2_w_up, ffn2_w_down))
    ab_out = _cast_bf16(ab_w_out)
    cd_out = _cast_bf16(cd_w_out)

    xc = ctx.reshape(1, nb * lctx, d)
    for l in range(depth):
        need_ctx = l < depth - 1
        mods = mods_all[l]
        x = _ffn(x, mods, x_row, *w1, l, j0=0)
        xc = _ffn(xc, mods, ctx_row, *w1, l, j0=0)
        xc_b = xc.reshape(nb, lctx, d)
        i = l // 2
        if l % 2 == 0:
            lam_init = 0.8 - 0.6 * math.exp(-0.3 * l)
            w_in = ab_w_in[i]
            n_q = A_HEADS * hd + B_HEADS * 2 * hd
            n_ak = A_KV_HEADS * hd
            n_bk = B_HEADS * 2 * hd
            o_av = n_q + n_ak
            o_bk = o_av + n_ak
            o_bv = o_bk + n_bk
            wt = jnp.concatenate([w_in[:, :o_av], w_in[:, o_bk:o_bv], w_in[:, o_av:o_bk], w_in[:, o_bv:]],
                                 axis=1).T.astype(BF16)
            gains = jnp.concatenate([_col(a_q_norm[i]), _col(b_q_norm[i]),
                                     _col(a_k_norm[i]), _col(b_k_norm[i])], axis=0)
            aqt, bqt, ak, bk, avt, bvt = _proj_ab(x, mods, x_row, wt, gains, rope_head, need_q=True)
            ctx_out = _proj_ab(xc_b, mods, ctx_row, wt if need_ctx else wt[n_q:], gains, None, need_q=need_ctx)
            akx, bkx, avxt, bvxt = ctx_out[-4:]
            sink = a_sink[i].astype(F32)
            lam_vecs = jnp.stack([b_lambda_q1[i], b_lambda_k1[i], b_lambda_q2[i], b_lambda_k2[i]]).astype(F32)
            sub_gain = _col(b_sub_norm[i])
            y1 = _attn_a(sink, aqt, ak, avt, akx, avxt, latent=True)
            y2 = _attn_b(lam_vecs, sub_gain, bqt, bk, bvt, bkx, bvxt, latent=True, lam_init=lam_init)
            w_out = ab_out
            if need_ctx:
                aqxt, bqxt = ctx_out[:2]
                y1x = _attn_a(sink, aqxt, None, None, akx, avxt, latent=False)
                y2x = _attn_b(lam_vecs, sub_gain, bqxt, None, None, bkx, bvxt, latent=False, lam_init=lam_init)
        else:
            assert not need_ctx, "context queries of an odd layer are only needed for depth > 2"
            w_in = cd_w_in[i]
            n_q = C_Q_RANK + D_HEADS * hd
            wt = w_in.T.astype(BF16)
            uq = c_w_uq[i].reshape(C_Q_RANK, C_HEADS, C_NOPE + C_ROPE)
            wuq = jnp.concatenate([uq[:, :, :C_NOPE].reshape(C_Q_RANK, -1),
                                   uq[:, :, C_NOPE:].reshape(C_Q_RANK, -1)], axis=1).T.astype(BF16)
            ukv = c_w_ukv[i].reshape(C_KV_RANK, C_HEADS, C_NOPE + C_V)
            wuk = ukv[:, :, :C_NOPE].reshape(C_KV_RANK, -1).T.astype(BF16)
            wuv = ukv[:, :, C_NOPE:].reshape(C_KV_RANK, -1).T.astype(BF16)
            g_qa = _col(c_q_a_norm[i])
            g_kva = _col(c_kv_a_norm[i])
            g_head = jnp.concatenate([_col(c_q_nope_norm[i]), _col(c_q_rope_norm[i]),
                                      _col(c_k_nope_norm[i]), _col(c_k_rope_norm[i]),
                                      _col(d_q_norm[i]), _col(d_k_norm[i])], axis=0)
            cqt, dqt, ck, cvt, dk, dvt = _proj_cd(x, mods, x_row, wt, wuq, wuk, wuv,
                                                  g_qa, g_kva, g_head, rope_mla, need_q=True)
            ckx, cvxt, dkx, dvxt = _proj_cd(xc_b, mods, ctx_row, wt[n_q:], wuq, wuk, wuv,
                                            g_qa, g_kva, g_head, None, need_q=False)
            y1 = _attn_c(cqt, ck, cvt, ckx, cvxt)
            y2 = _attn_d(_rpb_table(d_rpb[i].astype(F32)), dqt, dk, dvt, dkx, dvxt)
            w_out = cd_out
        x = _ffn(x, mods, x_row, *w2, l, j0=6, y=(y1, y2), w_out=w_out, out_layer=i)
        if need_ctx:
            half = y1x.shape[-1]
            xc = _ffn(xc, mods, ctx_row, *w2, l, j0=6, w_out=w_out, out_layer=i,
                      y=(y1x.reshape(1, nb * lctx, half), y2x.reshape(1, nb * lctx, half)))
    return x
```
